```python
import math
import jax, jax.numpy as jnp
from jax import lax
import numpy as np

D_MODEL = 2048
BATCH = 8
SEQ = 2048
DEPTH = 2

N_META = 16
BLOCK = 128
WINDOW = 128
HEAD_DIM = 64
A_HEADS = 8
A_KV_HEADS = 2
B_HEADS = 4
B_QK_DIM = 64
B_V_DIM = 2 * B_QK_DIM
C_HEADS = 8
IDX_HEADS = 8
IDX_DIM = 64
TOPK_MAX = 256
N_BUCKETS = 32
MAX_DISTANCE = 128
N_BIAS_HEADS = A_HEADS + B_HEADS + C_HEADS
N_BRANCH = 3
BRANCH_WIDTH = 512
D_FF = 5632
EPS = 1e-6
NEG = -1e30

IN_SPLITS = (
    A_HEADS * HEAD_DIM, A_KV_HEADS * HEAD_DIM, A_KV_HEADS * HEAD_DIM,
    B_HEADS * B_QK_DIM, B_HEADS * B_QK_DIM, B_HEADS * B_QK_DIM, B_HEADS * B_QK_DIM,
    B_HEADS * B_V_DIM,
    C_HEADS * HEAD_DIM, HEAD_DIM, HEAD_DIM,
    IDX_HEADS * IDX_DIM, IDX_DIM, IDX_HEADS,
    N_BRANCH * D_MODEL,
)
D_IN = sum(IN_SPLITS)

kernel_name = "hybrid_gated_swa_diff_dsa_macaron"


def rms_norm(x, g):
    xf = x.astype(jnp.float32)
    y = xf * lax.rsqrt(jnp.mean(xf * xf, axis=-1, keepdims=True) + EPS)
    return (y * g.astype(jnp.float32)).astype(x.dtype)


def swiglu(x, w_gate, w_up, w_down):
    return (jax.nn.silu(x @ w_gate) * (x @ w_up)) @ w_down


def rel_bucket(dist):
    max_exact = N_BUCKETS // 2
    d = jnp.maximum(dist, 0)
    df = jnp.maximum(d, 1).astype(jnp.float32)
    large = max_exact + (jnp.log(df / max_exact) / math.log(MAX_DISTANCE / max_exact)
                         * (N_BUCKETS - max_exact)).astype(jnp.int32)
    return jnp.where(d < max_exact, d, jnp.minimum(large, N_BUCKETS - 1))


def sliding_window_sink_attention(q, k, v, pos, sinks, table):
    bsz, P = q.shape[:2]
    nb = P // BLOCK
    grp = A_HEADS // A_KV_HEADS
    nk = N_META + 2 * BLOCK
    qb = q.reshape(bsz, nb, BLOCK, A_KV_HEADS, grp, HEAD_DIM)

    def keys_for_blocks(t):
        tb = t.reshape(bsz, nb, BLOCK, *t.shape[2:])
        prev = jnp.concatenate([jnp.zeros_like(tb[:, :1]), tb[:, :-1]], axis=1)
        meta = jnp.broadcast_to(t[:, None, BLOCK - N_META:BLOCK], (bsz, nb, N_META) + t.shape[2:])
        return jnp.concatenate([meta, prev, tb], axis=2)

    kk = keys_for_blocks(k)
    vv = keys_for_blocks(v)
    qpos = pos.reshape(nb, BLOCK)
    prev_pos = jnp.concatenate([jnp.full((1, BLOCK), -1, pos.dtype), qpos[:-1]], axis=0)
    meta_pos = jnp.broadcast_to(pos[BLOCK - N_META:BLOCK], (nb, N_META))
    kpos = jnp.concatenate([meta_pos, prev_pos, qpos], axis=1)
    dist = qpos[:, :, None] - kpos[:, None, :]
    is_meta_key = (jnp.arange(nk) < N_META)[None, None, :]
    band_ok = (kpos >= N_META)[:, None, :] & (dist >= 0) & (dist < WINDOW)
    allowed = jnp.where(is_meta_key, dist >= 0, band_ok)

    s = jnp.einsum('bnqkgd,bnskd->bnkgqs', qb, kk).astype(jnp.float32) * HEAD_DIM ** -0.5
    bias = table.astype(jnp.float32)[rel_bucket(dist)]
    bias = bias.reshape(nb, BLOCK, nk, A_KV_HEADS, grp).transpose(0, 3, 4, 1, 2)
    s = jnp.where(allowed[:, None, None], s + bias, NEG)
    sink = sinks.astype(jnp.float32).reshape(A_KV_HEADS, grp)[None, None, :, :, None, None]
    m = jnp.maximum(s.max(axis=-1, keepdims=True), sink)
    p = jnp.exp(s - m)
    w = p / (p.sum(axis=-1, keepdims=True) + jnp.exp(sink - m))
    o = jnp.einsum('bnkgqs,bnskd->bnqkgd', w.astype(v.dtype), vv)
    return o.reshape(bsz, P, A_HEADS * HEAD_DIM)


def differential_attention(q1, q2, k1, k2, v, pos, lam, lam_init, subln_g, table):
    bsz, P = q1.shape[:2]
    nb = P // BLOCK
    scale = B_QK_DIM ** -0.5
    table = table.astype(jnp.float32)
    key_ok = pos >= 0

    def to_blocks(t):
        return jnp.swapaxes(t.reshape(bsz, nb, BLOCK, *t.shape[2:]), 0, 1)

    def block_fn(args):
        qa, qb, qp = args
        dist = qp[:, None] - pos[None, :]
        allowed = (dist >= 0) & key_ok[None, :]
        bias = table[rel_bucket(dist)].transpose(2, 0, 1)

        def probs(qq, kk):
            s = jnp.einsum('bqhd,bshd->bhqs', qq, kk).astype(jnp.float32) * scale + bias
            return jax.nn.softmax(jnp.where(allowed, s, NEG), axis=-1)

        a = probs(qa, k1) - lam * probs(qb, k2)
        return jnp.einsum('bhqs,bshd->bqhd', a.astype(v.dtype), v)

    o = lax.map(block_fn, (to_blocks(q1), to_blocks(q2), pos.reshape(nb, BLOCK)))
    o = jnp.swapaxes(o, 0, 1).reshape(bsz, P, B_HEADS, B_V_DIM)
    o = rms_norm(o, subln_g) * (1.0 - lam_init)
    return o.reshape(bsz, P, B_HEADS * B_V_DIM)


def indexed_sparse_attention(q, k, v, qi, ki, wi, pos, topk, table):
    bsz, P = q.shape[:2]
    nb = P // BLOCK
    table = table.astype(jnp.float32)
    key_ok = pos >= 0
    gather = jax.vmap(lambda t, i: t[i])

    def to_blocks(t):
        return jnp.swapaxes(t.reshape(bsz, nb, BLOCK, *t.shape[2:]), 0, 1)

    def block_fn(args):
        qq, qiq, wq, qp = args
        dist = qp[:, None] - pos[None, :]
        admissible = (dist >= 0) & key_ok[None, :]
        rel = jax.nn.relu(jnp.einsum('bqhd,bsd->bqhs', qiq, ki).astype(jnp.float32) * IDX_DIM ** -0.5)
        score = jnp.einsum('bqhs,bqh->bqs', rel, wq.astype(jnp.float32)) * IDX_HEADS ** -0.5
        score = jnp.where(admissible[None], score, NEG)
        _, idx = lax.top_k(score, topk)
        k_sel = gather(k, idx)
        v_sel = gather(v, idx)
        kpos = pos[idx]
        dsel = qp[None, :, None] - kpos
        ok = (dsel >= 0) & (kpos >= 0)
        bias = jnp.swapaxes(table[rel_bucket(dsel)], 2, 3)
        s = jnp.einsum('bqhd,bqkd->bqhk', qq, k_sel).astype(jnp.float32) * HEAD_DIM ** -0.5 + bias
        p = jax.nn.softmax(jnp.where(ok[:, :, None, :], s, NEG), axis=-1)
        return jnp.einsum('bqhk,bqkd->bqhd', p.astype(v.dtype), v_sel)

    o = lax.map(block_fn, (to_blocks(q), to_blocks(qi), to_blocks(wi), pos.reshape(nb, BLOCK)))
    return jnp.swapaxes(o, 0, 1).reshape(bsz, P, C_HEADS * HEAD_DIM)


def hybrid_mixer(u, pos, topk, layer, w_in, sinks, diff_lambda, subln_g, w_branch, w_out, bias_table):
    bsz, P, _ = u.shape
    offsets = np.cumsum(IN_SPLITS)[:-1].tolist()
    (aq, ak, av, bq1, bq2, bk1, bk2, bv, cq, ck, cv, iq, ik, iw, gates) = jnp.split(u @ w_in, offsets, axis=-1)

    o_a = sliding_window_sink_attention(
        aq.reshape(bsz, P, A_HEADS, HEAD_DIM), ak.reshape(bsz, P, A_KV_HEADS, HEAD_DIM),
        av.reshape(bsz, P, A_KV_HEADS, HEAD_DIM), pos, sinks, bias_table[:, :A_HEADS])

    lam_init = 0.8 - 0.6 * math.exp(-0.3 * layer)
    lf = diff_lambda.astype(jnp.float32)
    lam = jnp.exp(jnp.sum(lf[0] * lf[1])) - jnp.exp(jnp.sum(lf[2] * lf[3])) + lam_init
    hb = lambda t, d: t.reshape(bsz, P, B_HEADS, d)
    o_b = differential_attention(
        hb(bq1, B_QK_DIM), hb(bq2, B_QK_DIM), hb(bk1, B_QK_DIM), hb(bk2, B_QK_DIM), hb(bv, B_V_DIM),
        pos, lam, lam_init, subln_g, bias_table[:, A_HEADS:A_HEADS + B_HEADS])

    o_c = indexed_sparse_attention(
        cq.reshape(bsz, P, C_HEADS, HEAD_DIM), ck, cv,
        iq.reshape(bsz, P, IDX_HEADS, IDX_DIM), ik, iw, pos, topk,
        bias_table[:, A_HEADS + B_HEADS:])

    g = jax.nn.sigmoid(gates.astype(jnp.float32)).astype(u.dtype).reshape(bsz, P, N_BRANCH, D_MODEL)
    y = (g[:, :, 0] * (o_a @ w_branch[0])
         + g[:, :, 1] * (o_b @ w_branch[1])
         + g[:, :, 2] * (o_c @ w_branch[2]))
    return y @ w_out


def setup_inputs(seed: int = 0) -> dict:
    key = jax.random.key(seed)
    ks = jax.random.split(key, 19)
    f32 = jnp.float32

    def nrm(k, shape, scale):
        return jax.random.normal(k, shape, f32) * scale

    def gain(k, shape):
        return 1.0 + 0.02 * jax.random.normal(k, shape, f32)

    return {
        "x": nrm(ks[0], (BATCH, SEQ, D_MODEL), 1.0),
        "meta_tokens": nrm(ks[1], (N_META, D_MODEL), 1.0),
        "rel_bias_table": nrm(ks[2], (N_BUCKETS, N_BIAS_HEADS), 0.5),
        "norm_ffn1": gain(ks[3], (DEPTH, D_MODEL)),
        "ffn1_w_gate": nrm(ks[4], (DEPTH, D_MODEL, D_FF), D_MODEL ** -0.5),
        "ffn1_w_up": nrm(ks[5], (DEPTH, D_MODEL, D_FF), D_MODEL ** -0.5),
        "ffn1_w_down": nrm(ks[6], (DEPTH, D_FF, D_MODEL), D_FF ** -0.5),
        "norm_mix": gain(ks[7], (DEPTH, D_MODEL)),
        "w_in": nrm(ks[8], (DEPTH, D_MODEL, D_IN), D_MODEL ** -0.5),
        "attn_sinks": nrm(ks[9], (DEPTH, A_HEADS), 0.5),
        "diff_lambda": nrm(ks[10], (DEPTH, 4, B_QK_DIM), 0.1),
        "diff_subln": gain(ks[11], (DEPTH, B_V_DIM)),
        "w_branch": nrm(ks[12], (DEPTH, N_BRANCH, BRANCH_WIDTH, D_MODEL), BRANCH_WIDTH ** -0.5),
        "w_out": nrm(ks[13], (DEPTH, D_MODEL, D_MODEL), D_MODEL ** -0.5),
        "norm_ffn2": gain(ks[14], (DEPTH, D_MODEL)),
        "ffn2_w_gate": nrm(ks[15], (DEPTH, D_MODEL, D_FF), D_MODEL ** -0.5),
        "ffn2_w_up": nrm(ks[16], (DEPTH, D_MODEL, D_FF), D_MODEL ** -0.5),
        "ffn2_w_down": nrm(ks[17], (DEPTH, D_FF, D_MODEL), D_FF ** -0.5),
        "norm_final": gain(ks[18], (D_MODEL,)),
    }


def reference(x, meta_tokens, rel_bias_table, norm_ffn1, ffn1_w_gate, ffn1_w_up, ffn1_w_down,
              norm_mix, w_in, attn_sinks, diff_lambda, diff_subln, w_branch, w_out,
              norm_ffn2, ffn2_w_gate, ffn2_w_up, ffn2_w_down, norm_final):
    bsz, seq_len, _ = x.shape
    n_pad = BLOCK - N_META
    h = jnp.concatenate([
        jnp.zeros((bsz, n_pad, D_MODEL), x.dtype),
        jnp.broadcast_to(meta_tokens.astype(x.dtype)[None], (bsz, N_META, D_MODEL)),
        x], axis=1)
    pos = jnp.arange(seq_len + BLOCK, dtype=jnp.int32) - n_pad
    topk = min(TOPK_MAX, seq_len // 4)

    for l in range(DEPTH):
        h = h + 0.5 * swiglu(rms_norm(h, norm_ffn1[l]), ffn1_w_gate[l], ffn1_w_up[l], ffn1_w_down[l])
        h = h + hybrid_mixer(rms_norm(h, norm_mix[l]), pos, topk, l, w_in[l], attn_sinks[l],
                             diff_lambda[l], diff_subln[l], w_branch[l], w_out[l], rel_bias_table)
        h = h + 0.5 * swiglu(rms_norm(h, norm_ffn2[l]), ffn2_w_gate[l], ffn2_w_up[l], ffn2_w_down[l])

    return rms_norm(h, norm_final)[:, BLOCK:]
```

```python
import functools
import math

import jax
import jax.numpy as jnp
import numpy as np
from jax import lax
from jax.experimental import pallas as pl
from jax.experimental.pallas import tpu as pltpu

F32 = jnp.float32
BF16 = jnp.bfloat16

D_MODEL = 2048
D_FF = 5632
DEPTH = 2
BLK = 128
N_META = 16
N_PAD = BLK - N_META
WINDOW = 128
HEAD_DIM = 64
A_HEADS, A_KV_HEADS = 8, 2
B_HEADS, B_QK_DIM, B_V_DIM = 4, 64, 128
C_HEADS = 8
IDX_HEADS, IDX_DIM = 8, 64
TOPK_MAX = 256
N_BUCKETS, MAX_DISTANCE = 32, 128
EPS = 1e-6
NEG = -1e30

COL_AQ, COL_CQ, COL_IQ = 0, 512, 1024
COL_BQ1, COL_BQ2, COL_BK1, COL_BK2 = 1536, 1792, 2048, 2304
COL_AK, COL_CKIK = 2560, 2688
N_PROJ = 3072
ROW_AV, ROW_BV, ROW_CV, ROW_IW, N_PROJ_T = 0, 128, 640, 704, 720

TM = 512
TF = 512
TN = 512
VMEM_LIMIT = 56 * 1024 * 1024


def _nt(a, b):
    return lax.dot_general(a, b, (((1,), (1,)), ((), ())), preferred_element_type=F32)


def _mm(a, b):
    return jnp.dot(a, b, preferred_element_type=F32)


def _rms(x, g):
    ms = jnp.mean(x * x, axis=-1, keepdims=True)
    return x * lax.rsqrt(ms + EPS) * g


def _ffn_kernel(h_ref, g_ref, wg_ref, wu_ref, wd_ref, o_ref, xn_ref, acc_ref, *, nk):
    k = pl.program_id(1)

    @pl.when(k == 0)
    def _():
        xn_ref[...] = _rms(h_ref[...], g_ref[...]).astype(BF16)
        acc_ref[...] = jnp.zeros_like(acc_ref)

    xn = xn_ref[...]
    a = _mm(xn, wg_ref[...])
    b = _mm(xn, wu_ref[...])
    t = (a * jax.nn.sigmoid(a)) * b
    acc_ref[...] += _mm(t.astype(BF16), wd_ref[...])

    @pl.when(k == nk - 1)
    def _():
        o_ref[...] = h_ref[...] + 0.5 * acc_ref[...]


def _ffn(h, gain, wg, wu, wd, layer):
    n = h.shape[0]
    nk = D_FF // TF
    return pl.pallas_call(
        functools.partial(_ffn_kernel, nk=nk),
        grid=(n // TM, nk),
        in_specs=[
            pl.BlockSpec((TM, D_MODEL), lambda i, k: (i, 0)),
            pl.BlockSpec((None, 1, D_MODEL), lambda i, k: (layer, 0, 0)),
            pl.BlockSpec((None, D_MODEL, TF), lambda i, k: (layer, 0, k)),
            pl.BlockSpec((None, D_MODEL, TF), lambda i, k: (layer, 0, k)),
            pl.BlockSpec((None, TF, D_MODEL), lambda i, k: (layer, k, 0)),
        ],
        out_specs=pl.BlockSpec((TM, D_MODEL), lambda i, k: (i, 0)),
        out_shape=jax.ShapeDtypeStruct((n, D_MODEL), F32),
        scratch_shapes=[pltpu.VMEM((TM, D_MODEL), BF16), pltpu.VMEM((TM, D_MODEL), F32)],
        input_output_aliases={0: 0},
        name="ffn",
        compiler_params=pltpu.CompilerParams(
            dimension_semantics=("parallel", "arbitrary"), vmem_limit_bytes=VMEM_LIMIT),
    )(h, gain, wg, wu, wd)


def _proj_kernel(h_ref, g_ref, w_ref, wt_ref, o_ref, avt_ref, bvt_ref, cvt_ref, iwt_ref, xn_ref):
    j = pl.program_id(1)

    @pl.when(j == 0)
    def _():
        xn = _rms(h_ref[...], g_ref[...]).astype(BF16)
        xn_ref[...] = xn
        vt = _nt(wt_ref[...], xn)
        for t in range(TM // BLK):
            cols = slice(t * BLK, (t + 1) * BLK)
            avt_ref[t] = vt[ROW_AV:ROW_BV, cols].astype(BF16)
            bvt_ref[t] = vt[ROW_BV:ROW_CV, cols].astype(BF16)
            cvt_ref[t] = vt[ROW_CV:ROW_IW, cols].astype(BF16)
            iwt_ref[t] = vt[ROW_IW:ROW_IW + IDX_HEADS, cols]

    o_ref[...] = _mm(xn_ref[...], w_ref[...]).astype(BF16)


def _proj(h, gain, w, wt, layer):
    n = h.shape[0]
    nt = n // BLK
    tpb = TM // BLK
    return pl.pallas_call(
        _proj_kernel,
        grid=(n // TM, N_PROJ // TN),
        in_specs=[
            pl.BlockSpec((TM, D_MODEL), lambda i, j: (i, 0)),
            pl.BlockSpec((None, 1, D_MODEL), lambda i, j: (layer, 0, 0)),
            pl.BlockSpec((None, D_MODEL, TN), lambda i, j: (layer, 0, j)),
            pl.BlockSpec((None, N_PROJ_T, D_MODEL), lambda i, j: (layer, 0, 0)),
        ],
        out_specs=[
            pl.BlockSpec((TM, TN), lambda i, j: (i, j)),
            pl.BlockSpec((tpb, 2 * HEAD_DIM, BLK), lambda i, j: (i, 0, 0)),
            pl.BlockSpec((tpb, B_HEADS * B_V_DIM, BLK), lambda i, j: (i, 0, 0)),
            pl.BlockSpec((tpb, HEAD_DIM, BLK), lambda i, j: (i, 0, 0)),
            pl.BlockSpec((tpb, IDX_HEADS, BLK), lambda i, j: (i, 0, 0)),
        ],
        out_shape=[
            jax.ShapeDtypeStruct((n, N_PROJ), BF16),
            jax.ShapeDtypeStruct((nt, 2 * HEAD_DIM, BLK), BF16),
            jax.ShapeDtypeStruct((nt, B_HEADS * B_V_DIM, BLK), BF16),
            jax.ShapeDtypeStruct((nt, HEAD_DIM, BLK), BF16),
            jax.ShapeDtypeStruct((nt, IDX_HEADS, BLK), F32),
        ],
        scratch_shapes=[pltpu.VMEM((TM, D_MODEL), BF16)],
        name="mixer_proj",
        compiler_params=pltpu.CompilerParams(
            dimension_semantics=("parallel", "arbitrary"), vmem_limit_bytes=VMEM_LIMIT),
    )(h, gain, w, wt)


def _gates_kernel(h_ref, g_ref, w_ref, o_ref, xn_ref):
    @pl.when(pl.program_id(1) == 0)
    def _():
        xn_ref[...] = _rms(h_ref[...], g_ref[...]).astype(BF16)

    o_ref[...] = jax.nn.sigmoid(_mm(xn_ref[...], w_ref[...]))


def _gates(h, gain, w, layer):
    n = h.shape[0]
    ncol = 3 * D_MODEL
    tn = 1024
    return pl.pallas_call(
        _gates_kernel,
        grid=(n // TM, ncol // tn),
        in_specs=[
            pl.BlockSpec((TM, D_MODEL), lambda i, j: (i, 0)),
            pl.BlockSpec((None, 1, D_MODEL), lambda i, j: (layer, 0, 0)),
            pl.BlockSpec((None, D_MODEL, tn), lambda i, j: (layer, 0, j)),
        ],
        out_specs=pl.BlockSpec((TM, tn), lambda i, j: (i, j)),
        out_shape=jax.ShapeDtypeStruct((n, ncol), F32),
        scratch_shapes=[pltpu.VMEM((TM, D_MODEL), BF16)],
        name="branch_gates",
        compiler_params=pltpu.CompilerParams(
            dimension_semantics=("parallel", "arbitrary"), vmem_limit_bytes=VMEM_LIMIT),
    )(h, gain, w)


def _tile_iotas():
    kl = lax.broadcasted_iota(jnp.int32, (BLK, BLK), 0)
    ql = lax.broadcasted_iota(jnp.int32, (BLK, BLK), 1)
    return kl, ql


def _colmax(x):
    return jnp.max(x, axis=0, keepdims=True)


def _colsum(x):
    return jnp.sum(x, axis=0, keepdims=True)


def _mixer_a_kernel(q_ref, k0_ref, k1_ref, k2_ref, v0_ref, v1_ref, v2_ref, bias_ref, sink_ref, o_ref):
    i = pl.program_id(1)
    kl, ql = _tile_iotas()
    qpos = BLK * i - N_PAD + ql
    ok_meta = (kl >= N_PAD) & (qpos >= kl - N_PAD)
    ok_prev = (BLK * (i - 1) - N_PAD + kl >= N_META) & (ql < kl)
    ok_cur = (BLK * i - N_PAD + kl >= N_META) & (ql >= kl)
    t_meta = jnp.minimum(i, 2)
    scale = HEAD_DIM ** -0.5
    grp = A_HEADS // A_KV_HEADS
    outs = []
    for h in range(A_HEADS):
        g = h // grp
        qh = q_ref[:, h * HEAD_DIM:(h + 1) * HEAD_DIM]
        ks = slice(g * HEAD_DIM, (g + 1) * HEAD_DIM)
        s0 = jnp.where(ok_meta, _nt(k0_ref[:, ks], qh) * scale + bias_ref[h, t_meta], NEG)
        s1 = jnp.where(ok_prev, _nt(k1_ref[:, ks], qh) * scale + bias_ref[h, 1], NEG)
        s2 = jnp.where(ok_cur, _nt(k2_ref[:, ks], qh) * scale + bias_ref[h, 0], NEG)
        sink = sink_ref[h:h + 1, :]
        m = jnp.maximum(jnp.maximum(_colmax(s0), _colmax(s1)), jnp.maximum(_colmax(s2), sink))
        p0 = jnp.exp(s0 - m)
        p1 = jnp.exp(s1 - m)
        p2 = jnp.exp(s2 - m)
        den = _colsum(p0) + _colsum(p1) + _colsum(p2) + jnp.exp(sink - m)
        ot = (_mm(v0_ref[ks, :], p0.astype(BF16)) + _mm(v1_ref[ks, :], p1.astype(BF16))
              + _mm(v2_ref[ks, :], p2.astype(BF16)))
        outs.append(ot / den)
    o_ref[...] = jnp.concatenate(outs, axis=0).T.astype(BF16)


def _mixer_a(proj, avt, bias, sinks, bsz, nb):
    n = proj.shape[0]
    kcol = COL_AK // BLK
    rowq = lambda b, i: (b * nb + i, 0)
    spec_k = lambda f: pl.BlockSpec((BLK, BLK), lambda b, i: (b * nb + f(i), kcol))
    spec_v = lambda f: pl.BlockSpec((None, 2 * HEAD_DIM, BLK), lambda b, i: (b * nb + f(i), 0, 0))
    first = lambda i: 0
    prev = lambda i: jnp.maximum(i - 1, 0)
    cur = lambda i: i
    return pl.pallas_call(
        _mixer_a_kernel,
        grid=(bsz, nb),
        in_specs=[
            pl.BlockSpec((BLK, A_HEADS * HEAD_DIM), rowq),
            spec_k(first), spec_k(prev), spec_k(cur),
            spec_v(first), spec_v(prev), spec_v(cur),
            pl.BlockSpec((A_HEADS, 3, BLK, BLK), lambda b, i: (0, 0, 0, 0)),
            pl.BlockSpec((A_HEADS, BLK), lambda b, i: (0, 0)),
        ],
        out_specs=pl.BlockSpec((BLK, A_HEADS * HEAD_DIM), rowq),
        out_shape=jax.ShapeDtypeStruct((n, A_HEADS * HEAD_DIM), BF16),
        name="mixer_a",
        compiler_params=pltpu.CompilerParams(dimension_semantics=("parallel", "parallel")),
    )(proj, proj, proj, proj, avt, avt, avt, bias, sinks)


def _mixer_b_kernel(q1_ref, q2_ref, k1_ref, k2_ref, vt_ref, bias_ref, lam_ref, subln_ref, o_ref,
                    m_ref, l_ref, acc_ref, *, lam_init):
    i = pl.program_id(1)
    kl, ql = _tile_iotas()
    qpos = BLK * i - N_PAD + ql
    scale = B_QK_DIM ** -0.5
    m_ref[...] = jnp.full_like(m_ref, NEG)
    l_ref[...] = jnp.zeros_like(l_ref)
    acc_ref[...] = jnp.zeros_like(acc_ref)

    def tile(j, carry):
        kpos = BLK * j - N_PAD + kl
        ok = (kpos >= 0) & (qpos >= kpos)
        t = jnp.minimum(i - j, 2)
        rows = pl.ds(pl.multiple_of(j * BLK, BLK), BLK)
        for h in range(B_HEADS):
            bias = bias_ref[h, t]
            cs = slice(h * B_QK_DIM, (h + 1) * B_QK_DIM)
            vt = vt_ref[j, h * B_V_DIM:(h + 1) * B_V_DIM, :]
            for mp, (q_ref, k_ref) in enumerate(((q1_ref, k1_ref), (q2_ref, k2_ref))):
                r = 2 * h + mp
                s = jnp.where(ok, _nt(k_ref[rows, cs], q_ref[:, cs]) * scale + bias, NEG)
                m_old = m_ref[r:r + 1, :]
                m_new = jnp.maximum(m_old, _colmax(s))
                alpha = jnp.exp(m_old - m_new)
                p = jnp.exp(s - m_new)
                l_ref[r:r + 1, :] = alpha * l_ref[r:r + 1, :] + _colsum(p)
                m_ref[r:r + 1, :] = m_new
                acc_ref[r] = alpha * acc_ref[r] + _mm(vt, p.astype(BF16))
        return carry

    lax.fori_loop(0, i + 1, tile, 0)

    lf = lam_ref[...]
    lam = (jnp.exp(jnp.sum(lf[0:1] * lf[1:2], axis=-1, keepdims=True))
           - jnp.exp(jnp.sum(lf[2:3] * lf[3:4], axis=-1, keepdims=True)) + lam_init)
    outs = []
    for h in range(B_HEADS):
        o = acc_ref[2 * h] / l_ref[2 * h:2 * h + 1, :] - lam * (acc_ref[2 * h + 1] / l_ref[2 * h + 1:2 * h + 2, :])
        ms = jnp.mean(o * o, axis=0, keepdims=True)
        outs.append(o * lax.rsqrt(ms + EPS) * subln_ref[...] * (1.0 - lam_init))
    o_ref[...] = jnp.concatenate(outs, axis=0).T.astype(BF16)


def _mixer_b(proj, bvt, bias, lam, subln, bsz, nb, lam_init):
    n = proj.shape[0]
    p_rows = nb * BLK
    w = B_HEADS * B_QK_DIM
    rowq = lambda c: (lambda b, i: (b * nb + i, c))
    return pl.pallas_call(
        functools.partial(_mixer_b_kernel, lam_init=lam_init),
        grid=(bsz, nb),
        in_specs=[
            pl.BlockSpec((BLK, w), rowq(COL_BQ1 // w)),
            pl.BlockSpec((BLK, w), rowq(COL_BQ2 // w)),
            pl.BlockSpec((p_rows, w), lambda b, i: (b, COL_BK1 // w)),
            pl.BlockSpec((p_rows, w), lambda b, i: (b, COL_BK2 // w)),
            pl.BlockSpec((nb, B_HEADS * B_V_DIM, BLK), lambda b, i: (b, 0, 0)),
            pl.BlockSpec((B_HEADS, 3, BLK, BLK), lambda b, i: (0, 0, 0, 0)),
            pl.BlockSpec((None, 4, B_QK_DIM), lambda b, i: (0, 0, 0)),
            pl.BlockSpec((None, B_V_DIM, BLK), lambda b, i: (0, 0, 0)),
        ],
        out_specs=pl.BlockSpec((BLK, B_HEADS * B_V_DIM), lambda b, i: (b * nb + i, 0)),
        out_shape=jax.ShapeDtypeStruct((n, B_HEADS * B_V_DIM), BF16),
        scratch_shapes=[
            pltpu.VMEM((2 * B_HEADS, BLK), F32),
            pltpu.VMEM((2 * B_HEADS, BLK), F32),
            pltpu.VMEM((2 * B_HEADS, B_V_DIM, BLK), F32),
        ],
        name="mixer_b",
        compiler_params=pltpu.CompilerParams(dimension_semantics=("parallel", "arbitrary")),
    )(proj, proj, proj, proj, bvt, bias, lam, subln)


def _ordered_bits_to_float(u):
    bits = jnp.where(u < 0, u ^ jnp.int32(-2 ** 31), ~u)
    return lax.bitcast_convert_type(bits, F32)


def _mixer_c_kernel(cq_ref, iq_ref, kk_ref, vt_ref, iwt_ref, bias_ref, o_ref,
                    score_ref, sel_ref, m_ref, l_ref, acc_ref, *, topk):
    i = pl.program_id(1)
    kl, ql = _tile_iotas()
    qpos = BLK * i - N_PAD + ql
    nt = i + 1
    kf = jnp.float32(topk)

    iw = iwt_ref[...]

    def score_tile(j, carry):
        rows = pl.ds(pl.multiple_of(j * BLK, BLK), BLK)
        ik = kk_ref[rows, HEAD_DIM:HEAD_DIM + IDX_DIM]
        sc = jnp.zeros((BLK, BLK), F32)
        for h in range(IDX_HEADS):
            rel = jnp.maximum(_nt(ik, iq_ref[:, h * IDX_DIM:(h + 1) * IDX_DIM]) * IDX_DIM ** -0.5, 0.0)
            sc = sc + rel * iw[h:h + 1, :]
        sc = sc * IDX_HEADS ** -0.5
        kpos = BLK * j - N_PAD + kl
        score_ref[j] = jnp.where((kpos >= 0) & (qpos >= kpos), sc, NEG)
        return carry

    lax.fori_loop(0, nt, score_tile, 0)

    def count_ge(thr):
        def body(j, c):
            return c + _colsum(jnp.where(score_ref[j] >= thr, 1.0, 0.0))
        return lax.fori_loop(0, nt, body, jnp.zeros((1, BLK), F32))

    def search(it, u):
        cand = u | lax.shift_left(jnp.int32(1), 31 - it)
        return jnp.where(count_ge(_ordered_bits_to_float(cand)) >= kf, cand, u)

    u = lax.fori_loop(0, 32, search, jnp.zeros((1, BLK), jnp.int32))
    found = (u < 0) | (u >= jnp.int32(0x00800000))
    thr = jnp.where(found, _ordered_bits_to_float(u), -jnp.inf)

    def count_gt(j, c):
        return c + _colsum(jnp.where(score_ref[j] > thr, 1.0, 0.0))

    need = kf - lax.fori_loop(0, nt, count_gt, jnp.zeros((1, BLK), F32))

    tri = (kl >= ql).astype(BF16)

    def select_tile(j, seen):
        sc = score_ref[j]
        eq = sc == thr
        eqf = jnp.where(eq, 1.0, 0.0)
        rank = seen + _mm(tri, eqf.astype(BF16))
        kpos = BLK * j - N_PAD + kl
        adm = (kpos >= 0) & (qpos >= kpos)
        sel = adm & ((sc > thr) | (eq & (rank <= need)))
        sel_ref[j] = jnp.where(sel, 1.0, 0.0)
        return seen + _colsum(eqf)

    lax.fori_loop(0, nt, select_tile, jnp.zeros((1, BLK), F32))

    scale = HEAD_DIM ** -0.5
    m_ref[...] = jnp.full_like(m_ref, NEG)
    l_ref[...] = jnp.zeros_like(l_ref)
    acc_ref[...] = jnp.zeros_like(acc_ref)

    def attend_tile(j, carry):
        rows = pl.ds(pl.multiple_of(j * BLK, BLK), BLK)
        ck = kk_ref[rows, 0:HEAD_DIM]
        vt = vt_ref[j]
        sel = sel_ref[j] > 0.5
        t = jnp.minimum(i - j, 2)
        for h in range(C_HEADS):
            s = _nt(ck, cq_ref[:, h * HEAD_DIM:(h + 1) * HEAD_DIM]) * scale + bias_ref[h, t]
            s = jnp.where(sel, s, NEG)
            m_old = m_ref[h:h + 1, :]
            m_new = jnp.maximum(m_old, _colmax(s))
            alpha = jnp.exp(m_old - m_new)
            p = jnp.exp(s - m_new)
            l_ref[h:h + 1, :] = alpha * l_ref[h:h + 1, :] + _colsum(p)
            m_ref[h:h + 1, :] = m_new
            acc_ref[h] = alpha * acc_ref[h] + _mm(vt, p.astype(BF16))
        return carry

    lax.fori_loop(0, nt, attend_tile, 0)
    outs = [acc_ref[h] / l_ref[h:h + 1, :] for h in range(C_HEADS)]
    o_ref[...] = jnp.concatenate(outs, axis=0).T.astype(BF16)


def _mixer_c(proj, cvt, iwt, bias, bsz, nb, topk):
    n = proj.shape[0]
    p_rows = nb * BLK
    w = C_HEADS * HEAD_DIM
    return pl.pallas_call(
        functools.partial(_mixer_c_kernel, topk=topk),
        grid=(bsz, nb),
        in_specs=[
            pl.BlockSpec((BLK, w), lambda b, i: (b * nb + i, COL_CQ // w)),
            pl.BlockSpec((BLK, w), lambda b, i: (b * nb + i, COL_IQ // w)),
            pl.BlockSpec((p_rows, BLK), lambda b, i: (b, COL_CKIK // BLK)),
            pl.BlockSpec((nb, HEAD_DIM, BLK), lambda b, i: (b, 0, 0)),
            pl.BlockSpec((None, IDX_HEADS, BLK), lambda b, i: (b * nb + i, 0, 0)),
            pl.BlockSpec((C_HEADS, 3, BLK, BLK), lambda b, i: (0, 0, 0, 0)),
        ],
        out_specs=pl.BlockSpec((BLK, w), lambda b, i: (b * nb + i, 0)),
        out_shape=jax.ShapeDtypeStruct((n, w), BF16),
        scratch_shapes=[
            pltpu.VMEM((nb, BLK, BLK), F32),
            pltpu.VMEM((nb, BLK, BLK), F32),
            pltpu.VMEM((C_HEADS, BLK), F32),
            pltpu.VMEM((C_HEADS, BLK), F32),
            pltpu.VMEM((C_HEADS, HEAD_DIM, BLK), F32),
        ],
        name="mixer_c",
        compiler_params=pltpu.CompilerParams(dimension_semantics=("parallel", "arbitrary")),
    )(proj, proj, proj, cvt, iwt, bias)


def _merge_kernel(h_ref, oa_ref, ob_ref, oc_ref, ga_ref, gb_ref, gc_ref, wa_ref, wb_ref, wc_ref, wo_ref,
                  o_ref, acc_ref, *, nc):
    c = pl.program_id(1)

    @pl.when(c == 0)
    def _():
        acc_ref[...] = jnp.zeros_like(acc_ref)

    y = (ga_ref[...] * _mm(oa_ref[...], wa_ref[...]) + gb_ref[...] * _mm(ob_ref[...], wb_ref[...])
         + gc_ref[...] * _mm(oc_ref[...], wc_ref[...]))
    acc_ref[...] += _mm(y.astype(BF16), wo_ref[...])

    @pl.when(c == nc - 1)
    def _():
        o_ref[...] = h_ref[...] + acc_ref[...]


def _merge(h, oa, ob, oc, gates, wbr, wout, layer):
    n = h.shape[0]
    tc = 512
    nc = D_MODEL // tc
    bw = 512
    row = lambda i, c: (i, 0)
    gate = lambda br: pl.BlockSpec((TM, tc), lambda i, c: (i, br * nc + c))
    wbranch = lambda br: pl.BlockSpec((None, None, bw, tc), lambda i, c: (layer, br, 0, c))
    return pl.pallas_call(
        functools.partial(_merge_kernel, nc=nc),
        grid=(n // TM, nc),
        in_specs=[
            pl.BlockSpec((TM, D_MODEL), row),
            pl.BlockSpec((TM, bw), row), pl.BlockSpec((TM, bw), row), pl.BlockSpec((TM, bw), row),
            gate(0), gate(1), gate(2),
            wbranch(0), wbranch(1), wbranch(2),
            pl.BlockSpec((None, tc, D_MODEL), lambda i, c: (layer, c, 0)),
        ],
        out_specs=pl.BlockSpec((TM, D_MODEL), row),
        out_shape=jax.ShapeDtypeStruct((n, D_MODEL), F32),
        scratch_shapes=[pltpu.VMEM((TM, D_MODEL), F32)],
        input_output_aliases={0: 0},
        name="branch_merge",
        compiler_params=pltpu.CompilerParams(
            dimension_semantics=("parallel", "arbitrary"), vmem_limit_bytes=VMEM_LIMIT),
    )(h, oa, ob, oc, gates, gates, gates, wbr, wbr, wbr, wout)


def _final_kernel(h_ref, g_ref, o_ref):
    o_ref[...] = _rms(h_ref[...], g_ref[...])


def _final_norm(h, gain, bsz, nb):
    seq = (nb - 1) * BLK
    return pl.pallas_call(
        _final_kernel,
        grid=(bsz, nb - 1),
        in_specs=[
            pl.BlockSpec((BLK, D_MODEL), lambda b, r: (b * nb + 1 + r, 0)),
            pl.BlockSpec((1, D_MODEL), lambda b, r: (0, 0)),
        ],
        out_specs=pl.BlockSpec((None, BLK, D_MODEL), lambda b, r: (b, r, 0)),
        out_shape=jax.ShapeDtypeStruct((bsz, seq, D_MODEL), F32),
        name="final_norm",
        compiler_params=pltpu.CompilerParams(dimension_semantics=("parallel", "parallel")),
    )(h, gain)


def _bucket_np(d):
    max_exact = N_BUCKETS // 2
    d = np.maximum(d, 0)
    df = np.maximum(d, 1).astype(np.float32)
    large = max_exact + (np.log(df / max_exact) / math.log(MAX_DISTANCE / max_exact)
                         * (N_BUCKETS - max_exact)).astype(np.int32)
    return np.where(d < max_exact, d, np.minimum(large, N_BUCKETS - 1)).astype(np.int32)


def _bias_tiles(table):
    kl = np.arange(BLK)[:, None]
    ql = np.arange(BLK)[None, :]
    idx = np.stack([_bucket_np(ql - kl), _bucket_np(BLK + ql - kl),
                    np.full((BLK, BLK), N_BUCKETS - 1, np.int32)])
    return jnp.transpose(table.astype(F32)[idx], (3, 0, 1, 2))


def _pack_w_in(w_in):
    o = np.cumsum([0, 512, 128, 128, 256, 256, 256, 256, 512, 512, 64, 64, 512, 64, 8, 3 * D_MODEL])
    (aq, ak, av, bq1, bq2, bk1, bk2, bv, cq, ck, cv, iq, ik, iw, gates) = [
        w_in[:, :, o[t]:o[t + 1]] for t in range(15)]
    pad = jnp.zeros(w_in.shape[:2] + (N_PROJ - COL_CKIK - BLK,), w_in.dtype)
    w = jnp.concatenate([aq, cq, iq, bq1, bq2, bk1, bk2, ak, ck, ik, pad], axis=-1).astype(BF16)
    padt = jnp.zeros(w_in.shape[:2] + (N_PROJ_T - ROW_IW - IDX_HEADS,), w_in.dtype)
    wt = jnp.swapaxes(jnp.concatenate([av, bv, cv, iw, padt], axis=-1), 1, 2).astype(BF16)
    return w, wt, gates.astype(BF16)


def kernel(x, meta_tokens, rel_bias_table, norm_ffn1, ffn1_w_gate, ffn1_w_up, ffn1_w_down, norm_mix, w_in,
           attn_sinks, diff_lambda, diff_subln, w_branch, w_out, norm_ffn2, ffn2_w_gate, ffn2_w_up,
           ffn2_w_down, norm_final):
    bsz, seq, _ = x.shape
    nb = seq // BLK + 1
    p_rows = nb * BLK
    n = bsz * p_rows
    assert seq % BLK == 0 and n % TM == 0
    topk = min(TOPK_MAX, seq // 4)

    h = jnp.concatenate([
        jnp.zeros((bsz, N_PAD, D_MODEL), x.dtype),
        jnp.broadcast_to(meta_tokens.astype(x.dtype)[None], (bsz, N_META, D_MODEL)),
        x], axis=1).reshape(n, D_MODEL)

    bias = _bias_tiles(rel_bias_table)
    bias_a, bias_b, bias_c = bias[:A_HEADS], bias[A_HEADS:A_HEADS + B_HEADS], bias[A_HEADS + B_HEADS:]
    w_proj, w_proj_t, w_gates = _pack_w_in(w_in)
    g1, gm, g2 = (t.reshape(DEPTH, 1, D_MODEL) for t in (norm_ffn1, norm_mix, norm_ffn2))
    f1 = [t.astype(BF16) for t in (ffn1_w_gate, ffn1_w_up, ffn1_w_down)]
    f2 = [t.astype(BF16) for t in (ffn2_w_gate, ffn2_w_up, ffn2_w_down)]
    wbr, wout = w_branch.astype(BF16), w_out.astype(BF16)
    sinks = jnp.broadcast_to(attn_sinks.astype(F32)[:, :, None], (DEPTH, A_HEADS, BLK))
    subln = jnp.broadcast_to(diff_subln.astype(F32)[:, :, None], (DEPTH, B_V_DIM, BLK))
    lam_f = diff_lambda.astype(F32)

    for l in range(DEPTH):
        h = _ffn(h, g1, *f1, l)
        proj, avt, bvt, cvt, iwt = _proj(h, gm, w_proj, w_proj_t, l)
        gates = _gates(h, gm, w_gates, l)
        o_a = _mixer_a(proj, avt, bias_a, sinks[l], bsz, nb)
        lam_init = 0.8 - 0.6 * math.exp(-0.3 * l)
        o_b = _mixer_b(proj, bvt, bias_b, lam_f[l:l + 1], subln[l:l + 1], bsz, nb, lam_init)
        o_c = _mixer_c(proj, cvt, iwt, bias_c, bsz, nb, topk)
        h = _merge(h, o_a, o_b, o_c, gates, wbr, wout, l)
        h = _ffn(h, g2, *f2, l)

    return _final_norm(h, norm_final.reshape(1, D_MODEL), bsz, nb)
```

```python
import functools
import math

import jax
import jax.numpy as jnp
import numpy as np
from jax import lax
from jax.experimental import pallas as pl
from jax.experimental.pallas import tpu as pltpu

F32 = jnp.float32
BF16 = jnp.bfloat16

D_MODEL = 2048
D_FF = 5632
DEPTH = 2
BLK = 128
N_META = 16
N_PAD = BLK - N_META
WINDOW = 128
HEAD_DIM = 64
A_HEADS, A_KV_HEADS = 8, 2
B_HEADS, B_QK_DIM, B_V_DIM = 4, 64, 128
C_HEADS = 8
IDX_HEADS, IDX_DIM = 8, 64
TOPK_MAX = 256
N_BUCKETS, MAX_DISTANCE = 32, 128
EPS = 1e-6
NEG = -1e30

COL_AQ, COL_CQ, COL_IQ = 0, 512, 1024
COL_BQ1, COL_BQ2, COL_BK1, COL_BK2 = 1536, 1792, 2048, 2304
COL_AK, COL_CKIK = 2560, 2688
N_PROJ = 3072
ROW_AV, ROW_BV, ROW_CV, ROW_IW, N_PROJ_T = 0, 128, 640, 704, 720

TM = 512
TF = 512
TN = 512
VMEM_LIMIT = 56 * 1024 * 1024


def _nt(a, b):
    return lax.dot_general(a, b, (((1,), (1,)), ((), ())), preferred_element_type=F32)


def _mm(a, b):
    return jnp.dot(a, b, preferred_element_type=F32)


def _rms(x, g):
    ms = jnp.mean(x * x, axis=-1, keepdims=True)
    return x * lax.rsqrt(ms + EPS) * g


def _ffn_kernel(h_ref, g_ref, wg_ref, wu_ref, wd_ref, o_ref, xn_ref, acc_ref, *, nk):
    k = pl.program_id(1)

    @pl.when(k == 0)
    def _():
        xn_ref[...] = _rms(h_ref[...], g_ref[...]).astype(BF16)
        acc_ref[...] = jnp.zeros_like(acc_ref)

    xn = xn_ref[...]
    a = _mm(xn, wg_ref[...])
    b = _mm(xn, wu_ref[...])
    t = (a * jax.nn.sigmoid(a)) * b
    acc_ref[...] += _mm(t.astype(BF16), wd_ref[...])

    @pl.when(k == nk - 1)
    def _():
        o_ref[...] = h_ref[...] + 0.5 * acc_ref[...]


def _ffn(h, gain, wg, wu, wd, layer):
    n = h.shape[0]
    nk = D_FF // TF
    return pl.pallas_call(
        functools.partial(_ffn_kernel, nk=nk),
        grid=(n // TM, nk),
        in_specs=[
            pl.BlockSpec((TM, D_MODEL), lambda i, k: (i, 0)),
            pl.BlockSpec((None, 1, D_MODEL), lambda i, k: (layer, 0, 0)),
            pl.BlockSpec((None, D_MODEL, TF), lambda i, k: (layer, 0, k)),
            pl.BlockSpec((None, D_MODEL, TF), lambda i, k: (layer, 0, k)),
            pl.BlockSpec((None, TF, D_MODEL), lambda i, k: (layer, k, 0)),
        ],
        out_specs=pl.BlockSpec((TM, D_MODEL), lambda i, k: (i, 0)),
        out_shape=jax.ShapeDtypeStruct((n, D_MODEL), F32),
        scratch_shapes=[pltpu.VMEM((TM, D_MODEL), BF16), pltpu.VMEM((TM, D_MODEL), F32)],
        input_output_aliases={0: 0},
        name="ffn",
        compiler_params=pltpu.CompilerParams(
            dimension_semantics=("parallel", "arbitrary"), vmem_limit_bytes=VMEM_LIMIT),
    )(h, gain, wg, wu, wd)


def _proj_kernel(h_ref, g_ref, w_ref, wt_ref, o_ref, avt_ref, bvt_ref, cvt_ref, iwt_ref, xn_ref):
    j = pl.program_id(1)

    @pl.when(j == 0)
    def _():
        xn = _rms(h_ref[...], g_ref[...]).astype(BF16)
        xn_ref[...] = xn
        vt = _nt(wt_ref[...], xn)
        for t in range(TM // BLK):
            cols = slice(t * BLK, (t + 1) * BLK)
            avt_ref[t] = vt[ROW_AV:ROW_BV, cols].astype(BF16)
            bvt_ref[t] = vt[ROW_BV:ROW_CV, cols].astype(BF16)
            cvt_ref[t] = vt[ROW_CV:ROW_IW, cols].astype(BF16)
            iwt_ref[t] = vt[ROW_IW:ROW_IW + IDX_HEADS, cols]

    o_ref[...] = _mm(xn_ref[...], w_ref[...]).astype(BF16)


def _proj(h, gain, w, wt, layer):
    n = h.shape[0]
    nt = n // BLK
    tpb = TM // BLK
    return pl.pallas_call(
        _proj_kernel,
        grid=(n // TM, N_PROJ // TN),
        in_specs=[
            pl.BlockSpec((TM, D_MODEL), lambda i, j: (i, 0)),
            pl.BlockSpec((None, 1, D_MODEL), lambda i, j: (layer, 0, 0)),
            pl.BlockSpec((None, D_MODEL, TN), lambda i, j: (layer, 0, j)),
            pl.BlockSpec((None, N_PROJ_T, D_MODEL), lambda i, j: (layer, 0, 0)),
        ],
        out_specs=[
            pl.BlockSpec((TM, TN), lambda i, j: (i, j)),
            pl.BlockSpec((tpb, 2 * HEAD_DIM, BLK), lambda i, j: (i, 0, 0)),
            pl.BlockSpec((tpb, B_HEADS * B_V_DIM, BLK), lambda i, j: (i, 0, 0)),
            pl.BlockSpec((tpb, HEAD_DIM, BLK), lambda i, j: (i, 0, 0)),
            pl.BlockSpec((tpb, IDX_HEADS, BLK), lambda i, j: (i, 0, 0)),
        ],
        out_shape=[
            jax.ShapeDtypeStruct((n, N_PROJ), BF16),
            jax.ShapeDtypeStruct((nt, 2 * HEAD_DIM, BLK), BF16),
            jax.ShapeDtypeStruct((nt, B_HEADS * B_V_DIM, BLK), BF16),
            jax.ShapeDtypeStruct((nt, HEAD_DIM, BLK), BF16),
            jax.ShapeDtypeStruct((nt, IDX_HEADS, BLK), F32),
        ],
        scratch_shapes=[pltpu.VMEM((TM, D_MODEL), BF16)],
        name="mixer_proj",
        compiler_params=pltpu.CompilerParams(
            dimension_semantics=("parallel", "arbitrary"), vmem_limit_bytes=VMEM_LIMIT),
    )(h, gain, w, wt)


def _gates_kernel(h_ref, g_ref, w_ref, o_ref, xn_ref):
    @pl.when(pl.program_id(1) == 0)
    def _():
        xn_ref[...] = _rms(h_ref[...], g_ref[...]).astype(BF16)

    o_ref[...] = jax.nn.sigmoid(_mm(xn_ref[...], w_ref[...]))


def _gates(h, gain, w, layer):
    n = h.shape[0]
    ncol = 3 * D_MODEL
    tn = 1024
    return pl.pallas_call(
        _gates_kernel,
        grid=(n // TM, ncol // tn),
        in_specs=[
            pl.BlockSpec((TM, D_MODEL), lambda i, j: (i, 0)),
            pl.BlockSpec((None, 1, D_MODEL), lambda i, j: (layer, 0, 0)),
            pl.BlockSpec((None, D_MODEL, tn), lambda i, j: (layer, 0, j)),
        ],
        out_specs=pl.BlockSpec((TM, tn), lambda i, j: (i, j)),
        out_shape=jax.ShapeDtypeStruct((n, ncol), F32),
        scratch_shapes=[pltpu.VMEM((TM, D_MODEL), BF16)],
        name="branch_gates",
        compiler_params=pltpu.CompilerParams(
            dimension_semantics=("parallel", "arbitrary"), vmem_limit_bytes=VMEM_LIMIT),
    )(h, gain, w)


def _tile_iotas():
    kl = lax.broadcasted_iota(jnp.int32, (BLK, BLK), 0)
    ql = lax.broadcasted_iota(jnp.int32, (BLK, BLK), 1)
    return kl, ql


def _colmax(x):
    return jnp.max(x, axis=0, keepdims=True)


def _colsum(x):
    return jnp.sum(x, axis=0, keepdims=True)


def _pair_tiles(p, nb):
    ja = 2 * p
    return ((ja, ja), (ja + 1, jnp.minimum(ja + 1, nb - 1)))


def _mixer_a_kernel(q_ref, k0_ref, k1_ref, k2_ref, v0_ref, v1_ref, v2_ref, bias_ref, sink_ref, o_ref):
    i = pl.program_id(1)
    kl, ql = _tile_iotas()
    qpos = BLK * i - N_PAD + ql
    ok_meta = (kl >= N_PAD) & (qpos >= kl - N_PAD)
    ok_prev = (BLK * (i - 1) - N_PAD + kl >= N_META) & (ql < kl)
    ok_cur = (BLK * i - N_PAD + kl >= N_META) & (ql >= kl)
    t_meta = jnp.minimum(i, 2)
    scale = HEAD_DIM ** -0.5
    grp = A_HEADS // A_KV_HEADS
    outs = []
    for h in range(A_HEADS):
        g = h // grp
        qh = q_ref[:, h * HEAD_DIM:(h + 1) * HEAD_DIM]
        ks = slice(g * HEAD_DIM, (g + 1) * HEAD_DIM)
        s0 = jnp.where(ok_meta, _nt(k0_ref[:, ks], qh) * scale + bias_ref[h, t_meta], NEG)
        s1 = jnp.where(ok_prev, _nt(k1_ref[:, ks], qh) * scale + bias_ref[h, 1], NEG)
        s2 = jnp.where(ok_cur, _nt(k2_ref[:, ks], qh) * scale + bias_ref[h, 0], NEG)
        sink = sink_ref[h:h + 1, :]
        m = jnp.maximum(jnp.maximum(_colmax(s0), _colmax(s1)), jnp.maximum(_colmax(s2), sink))
        p0 = jnp.exp(s0 - m)
        p1 = jnp.exp(s1 - m)
        p2 = jnp.exp(s2 - m)
        den = _colsum(p0) + _colsum(p1) + _colsum(p2) + jnp.exp(sink - m)
        ot = (_mm(v0_ref[ks, :], p0.astype(BF16)) + _mm(v1_ref[ks, :], p1.astype(BF16))
              + _mm(v2_ref[ks, :], p2.astype(BF16)))
        outs.append(ot / den)
    o_ref[...] = jnp.concatenate(outs, axis=0).T.astype(BF16)


def _mixer_a(proj, avt, bias, sinks, bsz, nb):
    n = proj.shape[0]
    kcol = COL_AK // BLK
    rowq = lambda b, i: (b * nb + i, 0)
    spec_k = lambda f: pl.BlockSpec((BLK, BLK), lambda b, i: (b * nb + f(i), kcol))
    spec_v = lambda f: pl.BlockSpec((None, 2 * HEAD_DIM, BLK), lambda b, i: (b * nb + f(i), 0, 0))
    first = lambda i: 0
    prev = lambda i: jnp.maximum(i - 1, 0)
    cur = lambda i: i
    return pl.pallas_call(
        _mixer_a_kernel,
        grid=(bsz, nb),
        in_specs=[
            pl.BlockSpec((BLK, A_HEADS * HEAD_DIM), rowq),
            spec_k(first), spec_k(prev), spec_k(cur),
            spec_v(first), spec_v(prev), spec_v(cur),
            pl.BlockSpec((A_HEADS, 3, BLK, BLK), lambda b, i: (0, 0, 0, 0)),
            pl.BlockSpec((A_HEADS, BLK), lambda b, i: (0, 0)),
        ],
        out_specs=pl.BlockSpec((BLK, A_HEADS * HEAD_DIM), rowq),
        out_shape=jax.ShapeDtypeStruct((n, A_HEADS * HEAD_DIM), BF16),
        name="mixer_a",
        compiler_params=pltpu.CompilerParams(dimension_semantics=("parallel", "parallel")),
    )(proj, proj, proj, proj, avt, avt, avt, bias, sinks)


B_GROUPS = B_HEADS // 2


def _mixer_b_kernel(q1_ref, q2_ref, k1_ref, k2_ref, vt_ref, bias_ref, lam_ref, subln_ref, o_ref,
                    qbd_ref, acc_ref, *, lam_init, nb):
    i = pl.program_id(1)
    npair = (i + 2) // 2
    w2 = 2 * BLK
    kl = lax.broadcasted_iota(jnp.int32, (BLK, w2), 0)
    qpos = BLK * i - N_PAD + (lax.broadcasted_iota(jnp.int32, (BLK, w2), 1) & (BLK - 1))
    scale = B_QK_DIM ** -0.5

    qbd_ref[...] = jnp.zeros_like(qbd_ref)
    for mp, q_ref in enumerate((q1_ref, q2_ref)):
        for g in range(B_GROUPS):
            lo = g * BLK
            qbd_ref[mp, g, 0:BLK, 0:B_QK_DIM] = q_ref[:, lo:lo + B_QK_DIM]
            qbd_ref[mp, g, BLK:w2, B_QK_DIM:BLK] = q_ref[:, lo + B_QK_DIM:lo + BLK]
    acc_ref[...] = jnp.zeros_like(acc_ref)

    def pair(p, carry):
        ms, ls = carry
        tiles = _pair_tiles(p, nb)
        oks, ts, rows = [], [], []
        for jl, jk in tiles:
            kpos = BLK * jl - N_PAD + kl
            oks.append((kpos >= 0) & (qpos >= kpos))
            ts.append(jnp.clip(i - jl, 0, 2))
            rows.append(pl.ds(pl.multiple_of(jk * BLK, BLK), BLK))
        new_ms, new_ls = [], []
        ps = [[None] * B_GROUPS for _ in range(2)]
        alphas = [[None] * B_GROUPS for _ in range(2)]
        for mp, k_ref in enumerate((k1_ref, k2_ref)):
            row_m, row_l = [], []
            for g in range(B_GROUPS):
                ss = []
                for tl in range(2):
                    s = _nt(k_ref[rows[tl], g * BLK:(g + 1) * BLK], qbd_ref[mp, g]) * scale
                    ss.append(jnp.where(oks[tl], s + bias_ref[g, ts[tl]], NEG))
                m_old = ms[mp][g]
                m_new = jnp.maximum(m_old, jnp.maximum(_colmax(ss[0]), _colmax(ss[1])))
                alpha = jnp.exp(m_old - m_new)
                pa = jnp.exp(ss[0] - m_new)
                pb = jnp.exp(ss[1] - m_new)
                row_m.append(m_new)
                row_l.append(alpha * ls[mp][g] + _colsum(pa) + _colsum(pb))
                ps[mp][g] = (pa.astype(BF16), pb.astype(BF16))
                alphas[mp][g] = alpha
            new_ms.append(row_m)
            new_ls.append(row_l)
        for h in range(B_HEADS):
            g, hs = h // 2, slice((h % 2) * BLK, (h % 2 + 1) * BLK)
            alpha = jnp.concatenate([alphas[0][g][:, hs], alphas[1][g][:, hs]], axis=1)
            upd = alpha * acc_ref[h]
            for tl, (jl, jk) in enumerate(tiles):
                p12 = jnp.concatenate([ps[0][g][tl][:, hs], ps[1][g][tl][:, hs]], axis=1)
                upd = upd + _mm(vt_ref[jk, h * B_V_DIM:(h + 1) * B_V_DIM, :], p12)
            acc_ref[h] = upd
        return new_ms, new_ls

    init_m = [[jnp.full((1, w2), NEG, F32) for _ in range(B_GROUPS)] for _ in range(2)]
    init_l = [[jnp.zeros((1, w2), F32) for _ in range(B_GROUPS)] for _ in range(2)]
    _, ls = lax.fori_loop(0, npair, pair, (init_m, init_l))

    lf = lam_ref[...]
    lam = (jnp.exp(jnp.sum(lf[0:1] * lf[1:2], axis=-1, keepdims=True))
           - jnp.exp(jnp.sum(lf[2:3] * lf[3:4], axis=-1, keepdims=True)) + lam_init)
    outs = []
    for h in range(B_HEADS):
        g, hs = h // 2, slice((h % 2) * BLK, (h % 2 + 1) * BLK)
        acc = acc_ref[h]
        o = acc[:, :BLK] / ls[0][g][:, hs] - lam * (acc[:, BLK:] / ls[1][g][:, hs])
        ms_ = jnp.mean(o * o, axis=0, keepdims=True)
        outs.append(o * lax.rsqrt(ms_ + EPS) * subln_ref[...] * (1.0 - lam_init))
    o_ref[...] = jnp.concatenate(outs, axis=0).T.astype(BF16)


def _mixer_b(proj, bvt, bias, lam, subln, bsz, nb, lam_init):
    n = proj.shape[0]
    p_rows = nb * BLK
    w = B_HEADS * B_QK_DIM
    rowq = lambda c: (lambda b, i: (b * nb + i, c))
    return pl.pallas_call(
        functools.partial(_mixer_b_kernel, lam_init=lam_init, nb=nb),
        grid=(bsz, nb),
        in_specs=[
            pl.BlockSpec((BLK, w), rowq(COL_BQ1 // w)),
            pl.BlockSpec((BLK, w), rowq(COL_BQ2 // w)),
            pl.BlockSpec((p_rows, w), lambda b, i: (b, COL_BK1 // w)),
            pl.BlockSpec((p_rows, w), lambda b, i: (b, COL_BK2 // w)),
            pl.BlockSpec((nb, B_HEADS * B_V_DIM, BLK), lambda b, i: (b, 0, 0)),
            pl.BlockSpec((B_GROUPS, 3, BLK, 2 * BLK), lambda b, i: (0, 0, 0, 0)),
            pl.BlockSpec((None, 4, B_QK_DIM), lambda b, i: (0, 0, 0)),
            pl.BlockSpec((None, B_V_DIM, BLK), lambda b, i: (0, 0, 0)),
        ],
        out_specs=pl.BlockSpec((BLK, B_HEADS * B_V_DIM), lambda b, i: (b * nb + i, 0)),
        out_shape=jax.ShapeDtypeStruct((n, B_HEADS * B_V_DIM), BF16),
        scratch_shapes=[
            pltpu.VMEM((2, B_GROUPS, 2 * BLK, BLK), BF16),
            pltpu.VMEM((B_HEADS, B_V_DIM, 2 * BLK), F32),
        ],
        name="mixer_b",
        compiler_params=pltpu.CompilerParams(dimension_semantics=("parallel", "arbitrary")),
    )(proj, proj, proj, proj, bvt, bias, lam, subln)


C_LANES = C_HEADS * BLK


def _ordered_bits_to_float(u):
    bits = jnp.where(u < 0, u ^ jnp.int32(-2 ** 31), ~u)
    return lax.bitcast_convert_type(bits, F32)


def _fold8(x):
    parts = [x[r:r + 8, :] for r in range(0, BLK, 8)]
    while len(parts) > 1:
        parts = [parts[t] + parts[t + 1] for t in range(0, len(parts), 2)]
    return parts[0]


def _mixer_c_kernel(cq_ref, iq_ref, kk_ref, vt_ref, iwt_ref, bias_ref, o_ref,
                    score_ref, sel_ref, qst_ref, iqst_ref, acc_ref, *, topk, nb):
    i = pl.program_id(1)
    npair = (i + 2) // 2
    kl, ql = _tile_iotas()
    qpos = BLK * i - N_PAD + ql
    kf = jnp.float32(topk)

    qst_ref[...] = jnp.zeros_like(qst_ref)
    iqst_ref[...] = jnp.zeros_like(iqst_ref)
    for h in range(C_HEADS):
        qst_ref[h * BLK:(h + 1) * BLK, 0:HEAD_DIM] = cq_ref[:, h * HEAD_DIM:(h + 1) * HEAD_DIM]
        iqst_ref[h * BLK:(h + 1) * BLK, HEAD_DIM:BLK] = iq_ref[:, h * IDX_DIM:(h + 1) * IDX_DIM]
    iw = iwt_ref[...]
    iw_all = jnp.concatenate([iw[h:h + 1, :] for h in range(IDX_HEADS)], axis=1)

    def admissible(jl):
        kpos = BLK * jl - N_PAD + kl
        return (kpos >= 0) & (qpos >= kpos)

    def score_pair(p, carry):
        for jl, jk in _pair_tiles(p, nb):
            rows = pl.ds(pl.multiple_of(jk * BLK, BLK), BLK)
            rel = jnp.maximum(_nt(kk_ref[rows, :], iqst_ref[...]) * IDX_DIM ** -0.5, 0.0) * iw_all
            sc = rel[:, 0:BLK]
            for h in range(1, IDX_HEADS):
                sc = sc + rel[:, h * BLK:(h + 1) * BLK]
            score_ref[jl] = jnp.where(admissible(jl), sc * IDX_HEADS ** -0.5, NEG)
        return carry

    lax.fori_loop(0, npair, score_pair, 0)

    def count(pred):
        def body(p, c):
            xa = jnp.where(pred(score_ref[2 * p]), 1.0, 0.0)
            xb = jnp.where(pred(score_ref[2 * p + 1]), 1.0, 0.0)
            return c + (_fold8(xa) + _fold8(xb))
        return _colsum(lax.fori_loop(0, npair, body, jnp.zeros((8, BLK), F32)))

    def search(it, u):
        cand = u | lax.shift_left(jnp.int32(1), 31 - it)
        thr_c = _ordered_bits_to_float(cand)
        return jnp.where(count(lambda s: s >= thr_c) >= kf, cand, u)

    u = lax.fori_loop(0, 32, search, jnp.zeros((1, BLK), jnp.int32))
    found = (u < 0) | (u >= jnp.int32(0x00800000))
    thr = jnp.where(found, _ordered_bits_to_float(u), -jnp.inf)
    need = kf - count(lambda s: s > thr)
    n_ge = count(lambda s: s >= thr)
    has_tie = jnp.max(jnp.where((n_ge > kf) & (thr > NEG), 1.0, 0.0)) > 0.5

    def select_plain():
        def body(p, carry):
            for jl in (2 * p, 2 * p + 1):
                sel_ref[jl] = jnp.where(admissible(jl) & (score_ref[jl] >= thr), 1.0, 0.0)
            return carry
        lax.fori_loop(0, npair, body, 0)

    def select_ties():
        tri = (kl >= ql).astype(BF16)

        def body(jl, seen):
            sc = score_ref[jl]
            eq = sc == thr
            eqf = jnp.where(eq, 1.0, 0.0)
            rank = seen + _mm(tri, eqf.astype(BF16))
            sel = admissible(jl) & ((sc > thr) | (eq & (rank <= need)))
            sel_ref[jl] = jnp.where(sel, 1.0, 0.0)
            return seen + _colsum(eqf)
        lax.fori_loop(0, 2 * npair, body, jnp.zeros((1, BLK), F32))

    lax.cond(has_tie, select_ties, select_plain)

    scale = HEAD_DIM ** -0.5
    acc_ref[...] = jnp.zeros_like(acc_ref)

    def attend_pair(p, carry):
        m_old, l_old = carry
        tiles = _pair_tiles(p, nb)
        ss = []
        for jl, jk in tiles:
            rows = pl.ds(pl.multiple_of(jk * BLK, BLK), BLK)
            s = _nt(kk_ref[rows, :], qst_ref[...]) * scale + bias_ref[jnp.clip(i - jl, 0, 2)]
            sel = jnp.tile(sel_ref[jl], (1, C_HEADS)) > 0.5
            ss.append(jnp.where(sel, s, NEG))
        m_new = jnp.maximum(m_old, jnp.maximum(_colmax(ss[0]), _colmax(ss[1])))
        alpha = jnp.exp(m_old - m_new)
        pa = jnp.exp(ss[0] - m_new)
        pb = jnp.exp(ss[1] - m_new)
        l_new = alpha * l_old + _colsum(pa) + _colsum(pb)
        acc_ref[...] = (alpha * acc_ref[...] + _mm(vt_ref[tiles[0][1]], pa.astype(BF16))
                        + _mm(vt_ref[tiles[1][1]], pb.astype(BF16)))
        return m_new, l_new

    init = (jnp.full((1, C_LANES), NEG, F32), jnp.zeros((1, C_LANES), F32))
    _, l_fin = lax.fori_loop(0, npair, attend_pair, init)
    o = acc_ref[...] / l_fin
    o_rows = jnp.concatenate([o[:, h * BLK:(h + 1) * BLK] for h in range(C_HEADS)], axis=0)
    o_ref[...] = o_rows.T.astype(BF16)


def _mixer_c(proj, cvt, iwt, bias, bsz, nb, topk):
    n = proj.shape[0]
    p_rows = nb * BLK
    w = C_HEADS * HEAD_DIM
    return pl.pallas_call(
        functools.partial(_mixer_c_kernel, topk=topk, nb=nb),
        grid=(bsz, nb),
        in_specs=[
            pl.BlockSpec((BLK, w), lambda b, i: (b * nb + i, COL_CQ // w)),
            pl.BlockSpec((BLK, w), lambda b, i: (b * nb + i, COL_IQ // w)),
            pl.BlockSpec((p_rows, BLK), lambda b, i: (b, COL_CKIK // BLK)),
            pl.BlockSpec((nb, HEAD_DIM, BLK), lambda b, i: (b, 0, 0)),
            pl.BlockSpec((None, IDX_HEADS, BLK), lambda b, i: (b * nb + i, 0, 0)),
            pl.BlockSpec((3, BLK, C_LANES), lambda b, i: (0, 0, 0)),
        ],
        out_specs=pl.BlockSpec((BLK, w), lambda b, i: (b * nb + i, 0)),
        out_shape=jax.ShapeDtypeStruct((n, w), BF16),
        scratch_shapes=[
            pltpu.VMEM((nb + 1, BLK, BLK), F32),
            pltpu.VMEM((nb + 1, BLK, BLK), F32),
            pltpu.VMEM((C_LANES, BLK), BF16),
            pltpu.VMEM((C_LANES, BLK), BF16),
            pltpu.VMEM((HEAD_DIM, C_LANES), F32),
        ],
        name="mixer_c",
        compiler_params=pltpu.CompilerParams(dimension_semantics=("parallel", "arbitrary")),
    )(proj, proj, proj, cvt, iwt, bias)


def _merge_kernel(h_ref, oa_ref, ob_ref, oc_ref, ga_ref, gb_ref, gc_ref, wa_ref, wb_ref, wc_ref, wo_ref,
                  o_ref, acc_ref, *, nc):
    c = pl.program_id(1)

    @pl.when(c == 0)
    def _():
        acc_ref[...] = jnp.zeros_like(acc_ref)

    y = (ga_ref[...] * _mm(oa_ref[...], wa_ref[...]) + gb_ref[...] * _mm(ob_ref[...], wb_ref[...])
         + gc_ref[...] * _mm(oc_ref[...], wc_ref[...]))
    acc_ref[...] += _mm(y.astype(BF16), wo_ref[...])

    @pl.when(c == nc - 1)
    def _():
        o_ref[...] = h_ref[...] + acc_ref[...]


def _merge(h, oa, ob, oc, gates, wbr, wout, layer):
    n = h.shape[0]
    tc = 512
    nc = D_MODEL // tc
    bw = 512
    row = lambda i, c: (i, 0)
    gate = lambda br: pl.BlockSpec((TM, tc), lambda i, c: (i, br * nc + c))
    wbranch = lambda br: pl.BlockSpec((None, None, bw, tc), lambda i, c: (layer, br, 0, c))
    return pl.pallas_call(
        functools.partial(_merge_kernel, nc=nc),
        grid=(n // TM, nc),
        in_specs=[
            pl.BlockSpec((TM, D_MODEL), row),
            pl.BlockSpec((TM, bw), row), pl.BlockSpec((TM, bw), row), pl.BlockSpec((TM, bw), row),
            gate(0), gate(1), gate(2),
            wbranch(0), wbranch(1), wbranch(2),
            pl.BlockSpec((None, tc, D_MODEL), lambda i, c: (layer, c, 0)),
        ],
        out_specs=pl.BlockSpec((TM, D_MODEL), row),
        out_shape=jax.ShapeDtypeStruct((n, D_MODEL), F32),
        scratch_shapes=[pltpu.VMEM((TM, D_MODEL), F32)],
        input_output_aliases={0: 0},
        name="branch_merge",
        compiler_params=pltpu.CompilerParams(
            dimension_semantics=("parallel", "arbitrary"), vmem_limit_bytes=VMEM_LIMIT),
    )(h, oa, ob, oc, gates, gates, gates, wbr, wbr, wbr, wout)


def _final_kernel(h_ref, g_ref, o_ref):
    o_ref[...] = _rms(h_ref[...], g_ref[...])


def _final_norm(h, gain, bsz, nb):
    seq = (nb - 1) * BLK
    return pl.pallas_call(
        _final_kernel,
        grid=(bsz, nb - 1),
        in_specs=[
            pl.BlockSpec((BLK, D_MODEL), lambda b, r: (b * nb + 1 + r, 0)),
            pl.BlockSpec((1, D_MODEL), lambda b, r: (0, 0)),
        ],
        out_specs=pl.BlockSpec((None, BLK, D_MODEL), lambda b, r: (b, r, 0)),
        out_shape=jax.ShapeDtypeStruct((bsz, seq, D_MODEL), F32),
        name="final_norm",
        compiler_params=pltpu.CompilerParams(dimension_semantics=("parallel", "parallel")),
    )(h, gain)


def _bucket_np(d):
    max_exact = N_BUCKETS // 2
    d = np.maximum(d, 0)
    df = np.maximum(d, 1).astype(np.float32)
    large = max_exact + (np.log(df / max_exact) / math.log(MAX_DISTANCE / max_exact)
                         * (N_BUCKETS - max_exact)).astype(np.int32)
    return np.where(d < max_exact, d, np.minimum(large, N_BUCKETS - 1)).astype(np.int32)


def _bias_tiles(table):
    kl = np.arange(BLK)[:, None]
    ql = np.arange(BLK)[None, :]
    idx = np.stack([_bucket_np(ql - kl), _bucket_np(BLK + ql - kl),
                    np.full((BLK, BLK), N_BUCKETS - 1, np.int32)])
    return jnp.transpose(table.astype(F32)[idx], (3, 0, 1, 2))


def _pack_w_in(w_in):
    o = np.cumsum([0, 512, 128, 128, 256, 256, 256, 256, 512, 512, 64, 64, 512, 64, 8, 3 * D_MODEL])
    (aq, ak, av, bq1, bq2, bk1, bk2, bv, cq, ck, cv, iq, ik, iw, gates) = [
        w_in[:, :, o[t]:o[t + 1]] for t in range(15)]
    pad = jnp.zeros(w_in.shape[:2] + (N_PROJ - COL_CKIK - BLK,), w_in.dtype)
    w = jnp.concatenate([aq, cq, iq, bq1, bq2, bk1, bk2, ak, ck, ik, pad], axis=-1).astype(BF16)
    padt = jnp.zeros(w_in.shape[:2] + (N_PROJ_T - ROW_IW - IDX_HEADS,), w_in.dtype)
    wt = jnp.swapaxes(jnp.concatenate([av, bv, cv, iw, padt], axis=-1), 1, 2).astype(BF16)
    return w, wt, gates.astype(BF16)


def kernel(x, meta_tokens, rel_bias_table, norm_ffn1, ffn1_w_gate, ffn1_w_up, ffn1_w_down, norm_mix, w_in,
           attn_sinks, diff_lambda, diff_subln, w_branch, w_out, norm_ffn2, ffn2_w_gate, ffn2_w_up,
           ffn2_w_down, norm_final):
    bsz, seq, _ = x.shape
    nb = seq // BLK + 1
    p_rows = nb * BLK
    n = bsz * p_rows
    assert seq % BLK == 0 and n % TM == 0
    topk = min(TOPK_MAX, seq // 4)

    h = jnp.concatenate([
        jnp.zeros((bsz, N_PAD, D_MODEL), x.dtype),
        jnp.broadcast_to(meta_tokens.astype(x.dtype)[None], (bsz, N_META, D_MODEL)),
        x], axis=1).reshape(n, D_MODEL)

    bias = _bias_tiles(rel_bias_table)
    bias_a = bias[:A_HEADS]
    bias_b = jnp.transpose(bias[A_HEADS:A_HEADS + B_HEADS].reshape(B_GROUPS, 2, 3, BLK, BLK),
                           (0, 2, 3, 1, 4)).reshape(B_GROUPS, 3, BLK, 2 * BLK)
    bias_c = jnp.transpose(bias[A_HEADS + B_HEADS:], (1, 2, 0, 3)).reshape(3, BLK, C_LANES)
    w_proj, w_proj_t, w_gates = _pack_w_in(w_in)
    g1, gm, g2 = (t.reshape(DEPTH, 1, D_MODEL) for t in (norm_ffn1, norm_mix, norm_ffn2))
    f1 = [t.astype(BF16) for t in (ffn1_w_gate, ffn1_w_up, ffn1_w_down)]
    f2 = [t.astype(BF16) for t in (ffn2_w_gate, ffn2_w_up, ffn2_w_down)]
    wbr, wout = w_branch.astype(BF16), w_out.astype(BF16)
    sinks = jnp.broadcast_to(attn_sinks.astype(F32)[:, :, None], (DEPTH, A_HEADS, BLK))
    subln = jnp.broadcast_to(diff_subln.astype(F32)[:, :, None], (DEPTH, B_V_DIM, BLK))
    lam_f = diff_lambda.astype(F32)

    for l in range(DEPTH):
        h = _ffn(h, g1, *f1, l)
        proj, avt, bvt, cvt, iwt = _proj(h, gm, w_proj, w_proj_t, l)
        gates = _gates(h, gm, w_gates, l)
        o_a = _mixer_a(proj, avt, bias_a, sinks[l], bsz, nb)
        lam_init = 0.8 - 0.6 * math.exp(-0.3 * l)
        o_b = _mixer_b(proj, bvt, bias_b, lam_f[l:l + 1], subln[l:l + 1], bsz, nb, lam_init)
        o_c = _mixer_c(proj, cvt, iwt, bias_c, bsz, nb, topk)
        h = _merge(h, o_a, o_b, o_c, gates, wbr, wout, l)
        h = _ffn(h, g2, *f2, l)

    return _final_norm(h, norm_final.reshape(1, D_MODEL), bsz, nb)
```

```python
import functools
import math

import jax
import jax.numpy as jnp
import numpy as np
from jax import lax
from jax.experimental import pallas as pl
from jax.experimental.pallas import tpu as pltpu

F32 = jnp.float32
BF16 = jnp.bfloat16

D_MODEL = 2048
D_FF = 5632
DEPTH = 2
BLK = 128
N_META = 16
N_PAD = BLK - N_META
WINDOW = 128
HEAD_DIM = 64
A_HEADS, A_KV_HEADS = 8, 2
B_HEADS, B_QK_DIM, B_V_DIM = 4, 64, 128
C_HEADS = 8
IDX_HEADS, IDX_DIM = 8, 64
TOPK_MAX = 256
N_BUCKETS, MAX_DISTANCE = 32, 128
EPS = 1e-6
NEG = -1e30

COL_AQ, COL_CQ, COL_IQ = 0, 512, 1024
COL_BQ1, COL_BQ2, COL_BK1, COL_BK2 = 1536, 1792, 2048, 2304
COL_AK, COL_CKIK = 2560, 2688
N_PROJ = 3072
ROW_AV, ROW_BV, ROW_CV, ROW_IW, N_PROJ_T = 0, 128, 640, 704, 720

TM = 512
TF = 512
TN = 1024
assert HEAD_DIM == B_QK_DIM == IDX_DIM == 64
QK_SCALE = 0.125
VMEM_LIMIT = 56 * 1024 * 1024


def _nt(a, b):
    return lax.dot_general(a, b, (((1,), (1,)), ((), ())), preferred_element_type=F32)


def _mm(a, b):
    return jnp.dot(a, b, preferred_element_type=F32)


def _rms(x, g):
    ms = jnp.mean(x * x, axis=-1, keepdims=True)
    return x * lax.rsqrt(ms + EPS) * g


def _ffn_kernel(h_ref, g_ref, wg_ref, wu_ref, wd_ref, o_ref, xn_ref, acc_ref, *, nk):
    k = pl.program_id(1)

    @pl.when(k == 0)
    def _():
        xn_ref[...] = _rms(h_ref[...], g_ref[...]).astype(BF16)
        acc_ref[...] = jnp.zeros_like(acc_ref)

    xn = xn_ref[...]
    a = _mm(xn, wg_ref[...])
    b = _mm(xn, wu_ref[...])
    t = (a * jax.nn.sigmoid(a)) * b
    acc_ref[...] += _mm(t.astype(BF16), wd_ref[...])

    @pl.when(k == nk - 1)
    def _():
        o_ref[...] = h_ref[...] + 0.5 * acc_ref[...]


def _ffn(h, gain, wg, wu, wd, layer):
    n = h.shape[0]
    nk = D_FF // TF
    return pl.pallas_call(
        functools.partial(_ffn_kernel, nk=nk),
        grid=(n // TM, nk),
        in_specs=[
            pl.BlockSpec((TM, D_MODEL), lambda i, k: (i, 0)),
            pl.BlockSpec((None, 1, D_MODEL), lambda i, k: (layer, 0, 0)),
            pl.BlockSpec((None, D_MODEL, TF), lambda i, k: (layer, 0, k)),
            pl.BlockSpec((None, D_MODEL, TF), lambda i, k: (layer, 0, k)),
            pl.BlockSpec((None, TF, D_MODEL), lambda i, k: (layer, k, 0)),
        ],
        out_specs=pl.BlockSpec((TM, D_MODEL), lambda i, k: (i, 0)),
        out_shape=jax.ShapeDtypeStruct((n, D_MODEL), F32),
        scratch_shapes=[pltpu.VMEM((TM, D_MODEL), BF16), pltpu.VMEM((TM, D_MODEL), F32)],
        input_output_aliases={0: 0},
        name="ffn",
        compiler_params=pltpu.CompilerParams(
            dimension_semantics=("parallel", "arbitrary"), vmem_limit_bytes=VMEM_LIMIT),
    )(h, gain, wg, wu, wd)


def _proj_kernel(h_ref, g_ref, w_ref, wt_ref, o_ref, avt_ref, bvt_ref, cvt_ref, iwt_ref, xn_ref):
    j = pl.program_id(1)

    @pl.when(j == 0)
    def _():
        xn = _rms(h_ref[...], g_ref[...]).astype(BF16)
        xn_ref[...] = xn
        vt = _nt(wt_ref[...], xn)
        for t in range(TM // BLK):
            cols = slice(t * BLK, (t + 1) * BLK)
            avt_ref[t] = vt[ROW_AV:ROW_BV, cols].astype(BF16)
            bvt_ref[t] = vt[ROW_BV:ROW_CV, cols].astype(BF16)
            cvt_ref[t] = vt[ROW_CV:ROW_IW, cols].astype(BF16)
            iwt_ref[t] = vt[ROW_IW:ROW_IW + IDX_HEADS, cols]

    o_ref[...] = _mm(xn_ref[...], w_ref[...]).astype(BF16)


def _proj(h, gain, w, wt, layer):
    n = h.shape[0]
    nt = n // BLK
    tpb = TM // BLK
    return pl.pallas_call(
        _proj_kernel,
        grid=(n // TM, N_PROJ // TN),
        in_specs=[
            pl.BlockSpec((TM, D_MODEL), lambda i, j: (i, 0)),
            pl.BlockSpec((None, 1, D_MODEL), lambda i, j: (layer, 0, 0)),
            pl.BlockSpec((None, D_MODEL, TN), lambda i, j: (layer, 0, j)),
            pl.BlockSpec((None, N_PROJ_T, D_MODEL), lambda i, j: (layer, 0, 0)),
        ],
        out_specs=[
            pl.BlockSpec((TM, TN), lambda i, j: (i, j)),
            pl.BlockSpec((tpb, 2 * HEAD_DIM, BLK), lambda i, j: (i, 0, 0)),
            pl.BlockSpec((tpb, B_HEADS * B_V_DIM, BLK), lambda i, j: (i, 0, 0)),
            pl.BlockSpec((tpb, HEAD_DIM, BLK), lambda i, j: (i, 0, 0)),
            pl.BlockSpec((tpb, IDX_HEADS, BLK), lambda i, j: (i, 0, 0)),
        ],
        out_shape=[
            jax.ShapeDtypeStruct((n, N_PROJ), BF16),
            jax.ShapeDtypeStruct((nt, 2 * HEAD_DIM, BLK), BF16),
            jax.ShapeDtypeStruct((nt, B_HEADS * B_V_DIM, BLK), BF16),
            jax.ShapeDtypeStruct((nt, HEAD_DIM, BLK), BF16),
            jax.ShapeDtypeStruct((nt, IDX_HEADS, BLK), F32),
        ],
        scratch_shapes=[pltpu.VMEM((TM, D_MODEL), BF16)],
        name="mixer_proj",
        compiler_params=pltpu.CompilerParams(
            dimension_semantics=("parallel", "arbitrary"), vmem_limit_bytes=VMEM_LIMIT),
    )(h, gain, w, wt)


def _tile_iotas():
    kl = lax.broadcasted_iota(jnp.int32, (BLK, BLK), 0)
    ql = lax.broadcasted_iota(jnp.int32, (BLK, BLK), 1)
    return kl, ql


def _colmax(x):
    return jnp.max(x, axis=0, keepdims=True)


def _colsum(x):
    return jnp.sum(x, axis=0, keepdims=True)


def _pair_tiles(p, nb):
    ja = 2 * p
    return ((ja, ja), (ja + 1, jnp.minimum(ja + 1, nb - 1)))


A_GRP = A_HEADS // A_KV_HEADS
A_LANES = A_GRP * BLK


def _mixer_a_kernel(q_ref, k0_ref, k1_ref, k2_ref, v0_ref, v1_ref, v2_ref, bias_ref, sink_ref, o_ref, qst_ref):
    i = pl.program_id(1)
    kl = lax.broadcasted_iota(jnp.int32, (BLK, A_LANES), 0)
    ql = lax.broadcasted_iota(jnp.int32, (BLK, A_LANES), 1) & (BLK - 1)
    qpos = BLK * i - N_PAD + ql
    ok_meta = (kl >= N_PAD) & (qpos >= kl - N_PAD)
    ok_prev = (BLK * (i - 1) - N_PAD + kl >= N_META) & (ql < kl)
    ok_cur = (BLK * i - N_PAD + kl >= N_META) & (ql >= kl)
    t_meta = jnp.minimum(i, 2)
    qst_ref[...] = jnp.zeros_like(qst_ref)
    for h in range(A_HEADS):
        g, r = divmod(h, A_GRP)
        qst_ref[g, r * BLK:(r + 1) * BLK, g * HEAD_DIM:(g + 1) * HEAD_DIM] = (
            q_ref[:, h * HEAD_DIM:(h + 1) * HEAD_DIM] * QK_SCALE)
    outs = []
    for g in range(A_KV_HEADS):
        qs = qst_ref[g]
        s0 = jnp.where(ok_meta, _nt(k0_ref[...], qs) + bias_ref[g, t_meta], NEG)
        s1 = jnp.where(ok_prev, _nt(k1_ref[...], qs) + bias_ref[g, 1], NEG)
        s2 = jnp.where(ok_cur, _nt(k2_ref[...], qs) + bias_ref[g, 0], NEG)
        sink = sink_ref[g]
        m = jnp.maximum(jnp.maximum(_colmax(s0), _colmax(s1)), jnp.maximum(_colmax(s2), sink))
        p0 = jnp.exp(s0 - m)
        p1 = jnp.exp(s1 - m)
        p2 = jnp.exp(s2 - m)
        den = _colsum(p0) + _colsum(p1) + _colsum(p2) + jnp.exp(sink - m)
        vs = slice(g * HEAD_DIM, (g + 1) * HEAD_DIM)
        ot = (_mm(v0_ref[vs, :], p0.astype(BF16)) + _mm(v1_ref[vs, :], p1.astype(BF16))
              + _mm(v2_ref[vs, :], p2.astype(BF16))) / den
        outs.extend(ot[:, r * BLK:(r + 1) * BLK] for r in range(A_GRP))
    o_ref[...] = jnp.concatenate(outs, axis=0).T.astype(BF16)


def _mixer_a(proj, avt, bias, sinks, bsz, nb):
    n = proj.shape[0]
    kcol = COL_AK // BLK
    rowq = lambda b, i: (b * nb + i, 0)
    spec_k = lambda f: pl.BlockSpec((BLK, BLK), lambda b, i: (b * nb + f(i), kcol))
    spec_v = lambda f: pl.BlockSpec((None, 2 * HEAD_DIM, BLK), lambda b, i: (b * nb + f(i), 0, 0))
    first = lambda i: 0
    prev = lambda i: jnp.maximum(i - 1, 0)
    cur = lambda i: i
    return pl.pallas_call(
        _mixer_a_kernel,
        grid=(bsz, nb),
        in_specs=[
            pl.BlockSpec((BLK, A_HEADS * HEAD_DIM), rowq),
            spec_k(first), spec_k(prev), spec_k(cur),
            spec_v(first), spec_v(prev), spec_v(cur),
            pl.BlockSpec((A_KV_HEADS, 3, BLK, A_LANES), lambda b, i: (0, 0, 0, 0)),
            pl.BlockSpec((A_KV_HEADS, 1, A_LANES), lambda b, i: (0, 0, 0)),
        ],
        out_specs=pl.BlockSpec((BLK, A_HEADS * HEAD_DIM), rowq),
        out_shape=jax.ShapeDtypeStruct((n, A_HEADS * HEAD_DIM), BF16),
        scratch_shapes=[pltpu.VMEM((A_KV_HEADS, A_LANES, BLK), BF16)],
        name="mixer_a",
        compiler_params=pltpu.CompilerParams(dimension_semantics=("parallel", "parallel")),
    )(proj, proj, proj, proj, avt, avt, avt, bias, sinks)


B_GROUPS = B_HEADS // 2


def _mixer_b_kernel(q1_ref, q2_ref, k1_ref, k2_ref, vt_ref, bias_ref, lam_ref, subln_ref, o_ref,
                    qbd_ref, acc_ref, *, lam_init, nb):
    i = pl.program_id(1)
    npair = (i + 2) // 2
    w2 = 2 * BLK
    kl = lax.broadcasted_iota(jnp.int32, (BLK, w2), 0)
    qpos = BLK * i - N_PAD + (lax.broadcasted_iota(jnp.int32, (BLK, w2), 1) & (BLK - 1))

    qbd_ref[...] = jnp.zeros_like(qbd_ref)
    for mp, q_ref in enumerate((q1_ref, q2_ref)):
        for g in range(B_GROUPS):
            lo = g * BLK
            qbd_ref[mp, g, 0:BLK, 0:B_QK_DIM] = q_ref[:, lo:lo + B_QK_DIM] * QK_SCALE
            qbd_ref[mp, g, BLK:w2, B_QK_DIM:BLK] = q_ref[:, lo + B_QK_DIM:lo + BLK] * QK_SCALE
    acc_ref[...] = jnp.zeros_like(acc_ref)

    def pair(p, carry):
        ms, ls = carry
        tiles = _pair_tiles(p, nb)
        oks, ts, rows = [], [], []
        for jl, jk in tiles:
            kpos = BLK * jl - N_PAD + kl
            oks.append((kpos >= 0) & (qpos >= kpos))
            ts.append(jnp.clip(i - jl, 0, 2))
            rows.append(pl.ds(pl.multiple_of(jk * BLK, BLK), BLK))
        new_ms, new_ls = [], []
        ps = [[None] * B_GROUPS for _ in range(2)]
        alphas = [[None] * B_GROUPS for _ in range(2)]
        for mp, k_ref in enumerate((k1_ref, k2_ref)):
            row_m, row_l = [], []
            for g in range(B_GROUPS):
                ss = []
                for tl in range(2):
                    s = _nt(k_ref[rows[tl], g * BLK:(g + 1) * BLK], qbd_ref[mp, g])
                    ss.append(jnp.where(oks[tl], s + bias_ref[g, ts[tl]], NEG))
                m_old = ms[mp][g]
                m_new = jnp.maximum(m_old, jnp.maximum(_colmax(ss[0]), _colmax(ss[1])))
                alpha = jnp.exp(m_old - m_new)
                pa = jnp.exp(ss[0] - m_new)
                pb = jnp.exp(ss[1] - m_new)
                row_m.append(m_new)
                row_l.append(alpha * ls[mp][g] + _colsum(pa) + _colsum(pb))
                ps[mp][g] = (pa.astype(BF16), pb.astype(BF16))
                alphas[mp][g] = alpha
            new_ms.append(row_m)
            new_ls.append(row_l)
        for h in range(B_HEADS):
            g, hs = h // 2, slice((h % 2) * BLK, (h % 2 + 1) * BLK)
            alpha = jnp.concatenate([alphas[0][g][:, hs], alphas[1][g][:, hs]], axis=1)
            upd = alpha * acc_ref[h]
            for tl, (jl, jk) in enumerate(tiles):
                p12 = jnp.concatenate([ps[0][g][tl][:, hs], ps[1][g][tl][:, hs]], axis=1)
                upd = upd + _mm(vt_ref[jk, h * B_V_DIM:(h + 1) * B_V_DIM, :], p12)
            acc_ref[h] = upd
        return new_ms, new_ls

    init_m = [[jnp.full((1, w2), NEG, F32) for _ in range(B_GROUPS)] for _ in range(2)]
    init_l = [[jnp.zeros((1, w2), F32) for _ in range(B_GROUPS)] for _ in range(2)]
    _, ls = lax.fori_loop(0, npair, pair, (init_m, init_l))

    lf = lam_ref[...]
    lam = (jnp.exp(jnp.sum(lf[0:1] * lf[1:2], axis=-1, keepdims=True))
           - jnp.exp(jnp.sum(lf[2:3] * lf[3:4], axis=-1, keepdims=True)) + lam_init)
    outs = []
    for h in range(B_HEADS):
        g, hs = h // 2, slice((h % 2) * BLK, (h % 2 + 1) * BLK)
        acc = acc_ref[h]
        o = acc[:, :BLK] / ls[0][g][:, hs] - lam * (acc[:, BLK:] / ls[1][g][:, hs])
        ms_ = jnp.mean(o * o, axis=0, keepdims=True)
        outs.append(o * lax.rsqrt(ms_ + EPS) * subln_ref[...] * (1.0 - lam_init))
    o_ref[...] = jnp.concatenate(outs, axis=0).T.astype(BF16)


def _mixer_b(proj, bvt, bias, lam, subln, bsz, nb, lam_init):
    n = proj.shape[0]
    p_rows = nb * BLK
    w = B_HEADS * B_QK_DIM
    rowq = lambda c: (lambda b, i: (b * nb + i, c))
    return pl.pallas_call(
        functools.partial(_mixer_b_kernel, lam_init=lam_init, nb=nb),
        grid=(bsz, nb),
        in_specs=[
            pl.BlockSpec((BLK, w), rowq(COL_BQ1 // w)),
            pl.BlockSpec((BLK, w), rowq(COL_BQ2 // w)),
            pl.BlockSpec((p_rows, w), lambda b, i: (b, COL_BK1 // w)),
            pl.BlockSpec((p_rows, w), lambda b, i: (b, COL_BK2 // w)),
            pl.BlockSpec((nb, B_HEADS * B_V_DIM, BLK), lambda b, i: (b, 0, 0)),
            pl.BlockSpec((B_GROUPS, 3, BLK, 2 * BLK), lambda b, i: (0, 0, 0, 0)),
            pl.BlockSpec((None, 4, B_QK_DIM), lambda b, i: (0, 0, 0)),
            pl.BlockSpec((None, B_V_DIM, BLK), lambda b, i: (0, 0, 0)),
        ],
        out_specs=pl.BlockSpec((BLK, B_HEADS * B_V_DIM), lambda b, i: (b * nb + i, 0)),
        out_shape=jax.ShapeDtypeStruct((n, B_HEADS * B_V_DIM), BF16),
        scratch_shapes=[
            pltpu.VMEM((2, B_GROUPS, 2 * BLK, BLK), BF16),
            pltpu.VMEM((B_HEADS, B_V_DIM, 2 * BLK), F32),
        ],
        name="mixer_b",
        compiler_params=pltpu.CompilerParams(dimension_semantics=("parallel", "arbitrary")),
    )(proj, proj, proj, proj, bvt, bias, lam, subln)


C_LANES = C_HEADS * BLK
C_CHUNK = 2 * BLK


def _ordered_bits_to_float(u):
    bits = jnp.where(u < 0, u ^ jnp.int32(-2 ** 31), ~u)
    return lax.bitcast_convert_type(bits, F32)


def _fold8(x):
    parts = [x[r:r + 8, :] for r in range(0, BLK, 8)]
    while len(parts) > 1:
        parts = [parts[t] + parts[t + 1] for t in range(0, len(parts), 2)]
    return parts[0]


def _mixer_c_kernel(cq_ref, iq_ref, kk_ref, vt_ref, iwt_ref, bias_ref, o_ref,
                    score_ref, sel_ref, qst_ref, iqst_ref, acc_ref, *, topk, nb):
    i = pl.program_id(1)
    npair = (i + 2) // 2
    kl, ql = _tile_iotas()
    qpos = BLK * i - N_PAD + ql
    kf = jnp.float32(topk)

    qst_ref[...] = jnp.zeros_like(qst_ref)
    iqst_ref[...] = jnp.zeros_like(iqst_ref)
    for h in range(C_HEADS):
        qst_ref[h * BLK:(h + 1) * BLK, 0:HEAD_DIM] = cq_ref[:, h * HEAD_DIM:(h + 1) * HEAD_DIM] * QK_SCALE
        iqst_ref[h * BLK:(h + 1) * BLK, HEAD_DIM:BLK] = iq_ref[:, h * IDX_DIM:(h + 1) * IDX_DIM]
    iw = iwt_ref[...] * QK_SCALE
    iw_all = jnp.concatenate([iw[h:h + 1, :] for h in range(IDX_HEADS)], axis=1)

    def admissible(jl):
        kpos = BLK * jl - N_PAD + kl
        return (kpos >= 0) & (qpos >= kpos)

    def score_pair(p, carry):
        for jl, jk in _pair_tiles(p, nb):
            rows = pl.ds(pl.multiple_of(jk * BLK, BLK), BLK)
            kk = kk_ref[rows, :]
            sc = None
            for c in range(C_LANES // C_CHUNK):
                ln = slice(c * C_CHUNK, (c + 1) * C_CHUNK)
                rel = jnp.maximum(_nt(kk, iqst_ref[ln, :]), 0.0) * iw_all[:, ln]
                for h in range(C_CHUNK // BLK):
                    part = rel[:, h * BLK:(h + 1) * BLK]
                    sc = part if sc is None else sc + part
            score_ref[jl] = jnp.where(admissible(jl), sc * IDX_HEADS ** -0.5, NEG)
        return carry

    lax.fori_loop(0, npair, score_pair, 0)

    def count(pred):
        def body(p, c):
            xa = jnp.where(pred(score_ref[2 * p]), 1.0, 0.0)
            xb = jnp.where(pred(score_ref[2 * p + 1]), 1.0, 0.0)
            return c + (_fold8(xa) + _fold8(xb))
        return _colsum(lax.fori_loop(0, npair, body, jnp.zeros((8, BLK), F32)))

    def search(it, u):
        cand = u | lax.shift_left(jnp.int32(1), 31 - it)
        thr_c = _ordered_bits_to_float(cand)
        return jnp.where(count(lambda s: s >= thr_c) >= kf, cand, u)

    u = lax.fori_loop(0, 32, search, jnp.zeros((1, BLK), jnp.int32))
    found = (u < 0) | (u >= jnp.int32(0x00800000))
    thr = jnp.where(found, _ordered_bits_to_float(u), -jnp.inf)
    need = kf - count(lambda s: s > thr)
    n_ge = count(lambda s: s >= thr)
    has_tie = jnp.max(jnp.where((n_ge > kf) & (thr > NEG), 1.0, 0.0)) > 0.5

    def select_plain():
        def body(p, carry):
            for jl in (2 * p, 2 * p + 1):
                sel_ref[jl] = jnp.where(admissible(jl) & (score_ref[jl] >= thr), 1.0, 0.0)
            return carry
        lax.fori_loop(0, npair, body, 0)

    def select_ties():
        tri = (kl >= ql).astype(BF16)

        def body(jl, seen):
            sc = score_ref[jl]
            eq = sc == thr
            eqf = jnp.where(eq, 1.0, 0.0)
            rank = seen + _mm(tri, eqf.astype(BF16))
            sel = admissible(jl) & ((sc > thr) | (eq & (rank <= need)))
            sel_ref[jl] = jnp.where(sel, 1.0, 0.0)
            return seen + _colsum(eqf)
        lax.fori_loop(0, 2 * npair, body, jnp.zeros((1, BLK), F32))

    lax.cond(has_tie, select_ties, select_plain)

    acc_ref[...] = jnp.zeros_like(acc_ref)

    def attend_pair(p, carry):
        m_old, l_old = carry
        tiles = _pair_tiles(p, nb)
        ss = []
        for jl, jk in tiles:
            rows = pl.ds(pl.multiple_of(jk * BLK, BLK), BLK)
            s = _nt(kk_ref[rows, :], qst_ref[...]) + bias_ref[jnp.clip(i - jl, 0, 2)]
            sel = jnp.tile(sel_ref[jl], (1, C_HEADS)) > 0.5
            ss.append(jnp.where(sel, s, NEG))
        m_new = jnp.maximum(m_old, jnp.maximum(_colmax(ss[0]), _colmax(ss[1])))
        alpha = jnp.exp(m_old - m_new)
        pa = jnp.exp(ss[0] - m_new)
        pb = jnp.exp(ss[1] - m_new)
        l_new = alpha * l_old + _colsum(pa) + _colsum(pb)
        acc_ref[...] = (alpha * acc_ref[...] + _mm(vt_ref[tiles[0][1]], pa.astype(BF16))
                        + _mm(vt_ref[tiles[1][1]], pb.astype(BF16)))
        return m_new, l_new

    init = (jnp.full((1, C_LANES), NEG, F32), jnp.zeros((1, C_LANES), F32))
    _, l_fin = lax.fori_loop(0, npair, attend_pair, init)
    o = acc_ref[...] / l_fin
    o_rows = jnp.concatenate([o[:, h * BLK:(h + 1) * BLK] for h in range(C_HEADS)], axis=0)
    o_ref[...] = o_rows.T.astype(BF16)


def _mixer_c(proj, cvt, iwt, bias, bsz, nb, topk):
    n = proj.shape[0]
    p_rows = nb * BLK
    w = C_HEADS * HEAD_DIM
    return pl.pallas_call(
        functools.partial(_mixer_c_kernel, topk=topk, nb=nb),
        grid=(bsz, nb),
        in_specs=[
            pl.BlockSpec((BLK, w), lambda b, i: (b * nb + i, COL_CQ // w)),
            pl.BlockSpec((BLK, w), lambda b, i: (b * nb + i, COL_IQ // w)),
            pl.BlockSpec((p_rows, BLK), lambda b, i: (b, COL_CKIK // BLK)),
            pl.BlockSpec((nb, HEAD_DIM, BLK), lambda b, i: (b, 0, 0)),
            pl.BlockSpec((None, IDX_HEADS, BLK), lambda b, i: (b * nb + i, 0, 0)),
            pl.BlockSpec((3, BLK, C_LANES), lambda b, i: (0, 0, 0)),
        ],
        out_specs=pl.BlockSpec((BLK, w), lambda b, i: (b * nb + i, 0)),
        out_shape=jax.ShapeDtypeStruct((n, w), BF16),
        scratch_shapes=[
            pltpu.VMEM((nb + 1, BLK, BLK), F32),
            pltpu.VMEM((nb + 1, BLK, BLK), F32),
            pltpu.VMEM((C_LANES, BLK), BF16),
            pltpu.VMEM((C_LANES, BLK), BF16),
            pltpu.VMEM((HEAD_DIM, C_LANES), F32),
        ],
        name="mixer_c",
        compiler_params=pltpu.CompilerParams(dimension_semantics=("parallel", "arbitrary")),
    )(proj, proj, proj, cvt, iwt, bias)


def _merge_kernel(h_ref, g_ref, oa_ref, ob_ref, oc_ref, ga_ref, gb_ref, gc_ref, wa_ref, wb_ref, wc_ref, wo_ref,
                  o_ref, xn_ref, acc_ref, *, nc):
    c = pl.program_id(1)

    @pl.when(c == 0)
    def _():
        xn_ref[...] = _rms(h_ref[...], g_ref[...]).astype(BF16)
        acc_ref[...] = jnp.zeros_like(acc_ref)

    xn = xn_ref[...]
    y = (jax.nn.sigmoid(_mm(xn, ga_ref[...])) * _mm(oa_ref[...], wa_ref[...])
         + jax.nn.sigmoid(_mm(xn, gb_ref[...])) * _mm(ob_ref[...], wb_ref[...])
         + jax.nn.sigmoid(_mm(xn, gc_ref[...])) * _mm(oc_ref[...], wc_ref[...]))
    acc_ref[...] += _mm(y.astype(BF16), wo_ref[...])

    @pl.when(c == nc - 1)
    def _():
        o_ref[...] = h_ref[...] + acc_ref[...]


def _merge(h, gain, oa, ob, oc, wgates, wbr, wout, layer):
    n = h.shape[0]
    tc = 512
    nc = D_MODEL // tc
    bw = 512
    row = lambda i, c: (i, 0)
    wgate = lambda br: pl.BlockSpec((None, D_MODEL, tc), lambda i, c: (layer, 0, br * nc + c))
    wbranch = lambda br: pl.BlockSpec((None, None, bw, tc), lambda i, c: (layer, br, 0, c))
    return pl.pallas_call(
        functools.partial(_merge_kernel, nc=nc),
        grid=(n // TM, nc),
        in_specs=[
            pl.BlockSpec((TM, D_MODEL), row),
            pl.BlockSpec((None, 1, D_MODEL), lambda i, c: (layer, 0, 0)),
            pl.BlockSpec((TM, bw), row), pl.BlockSpec((TM, bw), row), pl.BlockSpec((TM, bw), row),
            wgate(0), wgate(1), wgate(2),
            wbranch(0), wbranch(1), wbranch(2),
            pl.BlockSpec((None, tc, D_MODEL), lambda i, c: (layer, c, 0)),
        ],
        out_specs=pl.BlockSpec((TM, D_MODEL), row),
        out_shape=jax.ShapeDtypeStruct((n, D_MODEL), F32),
        scratch_shapes=[pltpu.VMEM((TM, D_MODEL), BF16), pltpu.VMEM((TM, D_MODEL), F32)],
        input_output_aliases={0: 0},
        name="branch_merge",
        compiler_params=pltpu.CompilerParams(
            dimension_semantics=("parallel", "arbitrary"), vmem_limit_bytes=VMEM_LIMIT),
    )(h, gain, oa, ob, oc, wgates, wgates, wgates, wbr, wbr, wbr, wout)


def _final_kernel(h_ref, g_ref, o_ref):
    o_ref[...] = _rms(h_ref[...], g_ref[...])


def _final_norm(h, gain, bsz, nb):
    seq = (nb - 1) * BLK
    return pl.pallas_call(
        _final_kernel,
        grid=(bsz, nb - 1),
        in_specs=[
            pl.BlockSpec((BLK, D_MODEL), lambda b, r: (b * nb + 1 + r, 0)),
            pl.BlockSpec((1, D_MODEL), lambda b, r: (0, 0)),
        ],
        out_specs=pl.BlockSpec((None, BLK, D_MODEL), lambda b, r: (b, r, 0)),
        out_shape=jax.ShapeDtypeStruct((bsz, seq, D_MODEL), F32),
        name="final_norm",
        compiler_params=pltpu.CompilerParams(dimension_semantics=("parallel", "parallel")),
    )(h, gain)


def _bucket_np(d):
    max_exact = N_BUCKETS // 2
    d = np.maximum(d, 0)
    df = np.maximum(d, 1).astype(np.float32)
    large = max_exact + (np.log(df / max_exact) / math.log(MAX_DISTANCE / max_exact)
                         * (N_BUCKETS - max_exact)).astype(np.int32)
    return np.where(d < max_exact, d, np.minimum(large, N_BUCKETS - 1)).astype(np.int32)


def _bias_tiles(table):
    kl = np.arange(BLK)[:, None]
    ql = np.arange(BLK)[None, :]
    idx = np.stack([_bucket_np(ql - kl), _bucket_np(BLK + ql - kl),
                    np.full((BLK, BLK), N_BUCKETS - 1, np.int32)])
    onehot = (idx.reshape(-1, 1) == np.arange(N_BUCKETS)[None, :]).astype(np.float32)
    tiles = jnp.dot(onehot, table.astype(F32), precision=lax.Precision.HIGHEST)
    return jnp.transpose(tiles.reshape(3, BLK, BLK, -1), (3, 0, 1, 2))


def _pack_w_in(w_in):
    o = np.cumsum([0, 512, 128, 128, 256, 256, 256, 256, 512, 512, 64, 64, 512, 64, 8, 3 * D_MODEL])
    (aq, ak, av, bq1, bq2, bk1, bk2, bv, cq, ck, cv, iq, ik, iw, gates) = [
        w_in[:, :, o[t]:o[t + 1]] for t in range(15)]
    pad = jnp.zeros(w_in.shape[:2] + (N_PROJ - COL_CKIK - BLK,), w_in.dtype)
    w = jnp.concatenate([aq, cq, iq, bq1, bq2, bk1, bk2, ak, ck, ik, pad], axis=-1).astype(BF16)
    padt = jnp.zeros(w_in.shape[:2] + (N_PROJ_T - ROW_IW - IDX_HEADS,), w_in.dtype)
    wt = jnp.swapaxes(jnp.concatenate([av, bv, cv, iw, padt], axis=-1), 1, 2).astype(BF16)
    return w, wt, gates.astype(BF16)


def kernel(x, meta_tokens, rel_bias_table, norm_ffn1, ffn1_w_gate, ffn1_w_up, ffn1_w_down, norm_mix, w_in,
           attn_sinks, diff_lambda, diff_subln, w_branch, w_out, norm_ffn2, ffn2_w_gate, ffn2_w_up,
           ffn2_w_down, norm_final):
    bsz, seq, _ = x.shape
    nb = seq // BLK + 1
    p_rows = nb * BLK
    n = bsz * p_rows
    assert seq % BLK == 0 and n % TM == 0
    topk = min(TOPK_MAX, seq // 4)

    h = jnp.concatenate([
        jnp.zeros((bsz, N_PAD, D_MODEL), x.dtype),
        jnp.broadcast_to(meta_tokens.astype(x.dtype)[None], (bsz, N_META, D_MODEL)),
        x], axis=1).reshape(n, D_MODEL)

    bias = _bias_tiles(rel_bias_table)
    bias_a = jnp.transpose(bias[:A_HEADS].reshape(A_KV_HEADS, A_GRP, 3, BLK, BLK),
                           (0, 2, 3, 1, 4)).reshape(A_KV_HEADS, 3, BLK, A_LANES)
    bias_b = jnp.transpose(bias[A_HEADS:A_HEADS + B_HEADS].reshape(B_GROUPS, 2, 3, BLK, BLK),
                           (0, 2, 3, 1, 4)).reshape(B_GROUPS, 3, BLK, 2 * BLK)
    bias_c = jnp.transpose(bias[A_HEADS + B_HEADS:], (1, 2, 0, 3)).reshape(3, BLK, C_LANES)
    w_proj, w_proj_t, w_gates = _pack_w_in(w_in)
    g1, gm, g2 = (t.reshape(DEPTH, 1, D_MODEL) for t in (norm_ffn1, norm_mix, norm_ffn2))
    f1 = [t.astype(BF16) for t in (ffn1_w_gate, ffn1_w_up, ffn1_w_down)]
    f2 = [t.astype(BF16) for t in (ffn2_w_gate, ffn2_w_up, ffn2_w_down)]
    wbr, wout = w_branch.astype(BF16), w_out.astype(BF16)
    sinks = jnp.broadcast_to(attn_sinks.astype(F32)[:, :, None],
                             (DEPTH, A_HEADS, BLK)).reshape(DEPTH, A_KV_HEADS, 1, A_LANES)
    subln = jnp.broadcast_to(diff_subln.astype(F32)[:, :, None], (DEPTH, B_V_DIM, BLK))
    lam_f = diff_lambda.astype(F32)

    for l in range(DEPTH):
        h = _ffn(h, g1, *f1, l)
        proj, avt, bvt, cvt, iwt = _proj(h, gm, w_proj, w_proj_t, l)
        o_a = _mixer_a(proj, avt, bias_a, sinks[l], bsz, nb)
        lam_init = 0.8 - 0.6 * math.exp(-0.3 * l)
        o_b = _mixer_b(proj, bvt, bias_b, lam_f[l:l + 1], subln[l:l + 1], bsz, nb, lam_init)
        o_c = _mixer_c(proj, cvt, iwt, bias_c, bsz, nb, topk)
        h = _merge(h, gm, o_a, o_b, o_c, w_gates, wbr, wout, l)
        h = _ffn(h, g2, *f2, l)

    return _final_norm(h, norm_final.reshape(1, D_MODEL), bsz, nb)
```

```python
import functools
import math

import jax
import jax.numpy as jnp
import numpy as np
from jax import lax
from jax.experimental import pallas as pl
from jax.experimental.pallas import tpu as pltpu

F32 = jnp.float32
BF16 = jnp.bfloat16

D_MODEL = 2048
D_FF = 5632
DEPTH = 2
BLK = 128
N_META = 16
N_PAD = BLK - N_META
WINDOW = 128
HEAD_DIM = 64
A_HEADS, A_KV_HEADS = 8, 2
B_HEADS, B_QK_DIM, B_V_DIM = 4, 64, 128
C_HEADS = 8
IDX_HEADS, IDX_DIM = 8, 64
TOPK_MAX = 256
N_BUCKETS, MAX_DISTANCE = 32, 128
EPS = 1e-6
NEG = -1e30

COL_AQ, COL_CQ, COL_IQ = 0, 512, 1024
COL_BQ1, COL_BQ2, COL_BK1, COL_BK2 = 1536, 1792, 2048, 2304
COL_AK, COL_CKIK = 2560, 2688
N_PROJ = 3072
ROW_AV, ROW_BV, ROW_CV, ROW_IW, N_PROJ_T = 0, 128, 640, 704, 720

TM = 512
TM_FFN = 1024
TF = 512
FFN_VMEM_LIMIT = 62 * 1024 * 1024
TN = 1024
assert HEAD_DIM == B_QK_DIM == IDX_DIM == 64
QK_SCALE = 0.125
VMEM_LIMIT = 56 * 1024 * 1024


def _nt(a, b):
    return lax.dot_general(a, b, (((1,), (1,)), ((), ())), preferred_element_type=F32)


def _mm(a, b):
    return jnp.dot(a, b, preferred_element_type=F32)


def _rms(x, g):
    ms = jnp.mean(x * x, axis=-1, keepdims=True)
    return x * lax.rsqrt(ms + EPS) * g


def _ffn_accumulate(xn_ref, wg_ref, wu_ref, wd_ref, o_ref):
    for r in range(TM_FFN // TM):
        rows = slice(r * TM, (r + 1) * TM)
        xn = xn_ref[rows, :]
        a = _mm(xn, wg_ref[...])
        b = _mm(xn, wu_ref[...])
        t = (a * jax.nn.sigmoid(a)) * b
        o_ref[rows, :] += 0.5 * _mm(t.astype(BF16), wd_ref[...])


def _ffn_kernel(h_ref, g_ref, wg_ref, wu_ref, wd_ref, o_ref, xn_ref):
    @pl.when(pl.program_id(1) == 0)
    def _():
        h = h_ref[...]
        xn_ref[...] = _rms(h, g_ref[...]).astype(BF16)
        o_ref[...] = h

    _ffn_accumulate(xn_ref, wg_ref, wu_ref, wd_ref, o_ref)


def _ffn_first_kernel(*refs, nb):
    nblk = TM_FFN // BLK
    x_refs = refs[:nblk]
    lead_ref, g_ref, wg_ref, wu_ref, wd_ref, o_ref, xn_ref = refs[nblk:]

    @pl.when(pl.program_id(1) == 0)
    def _():
        first = pl.program_id(0) * nblk
        for t in range(nblk):
            rows = slice(t * BLK, (t + 1) * BLK)
            blk = jnp.where(lax.rem(first + t, nb) == 0, lead_ref[...], x_refs[t][...])
            xn_ref[rows, :] = _rms(blk, g_ref[...]).astype(BF16)
            o_ref[rows, :] = blk

    _ffn_accumulate(xn_ref, wg_ref, wu_ref, wd_ref, o_ref)


def _ffn_weight_specs(layer):
    return [
        pl.BlockSpec((None, 1, D_MODEL), lambda i, k: (layer, 0, 0)),
        pl.BlockSpec((None, D_MODEL, TF), lambda i, k: (layer, 0, k)),
        pl.BlockSpec((None, D_MODEL, TF), lambda i, k: (layer, 0, k)),
        pl.BlockSpec((None, TF, D_MODEL), lambda i, k: (layer, k, 0)),
    ]


def _ffn(h, gain, wg, wu, wd, layer):
    n = h.shape[0]
    return pl.pallas_call(
        _ffn_kernel,
        grid=(n // TM_FFN, D_FF // TF),
        in_specs=[pl.BlockSpec((TM_FFN, D_MODEL), lambda i, k: (i, 0))] + _ffn_weight_specs(layer),
        out_specs=pl.BlockSpec((TM_FFN, D_MODEL), lambda i, k: (i, 0)),
        out_shape=jax.ShapeDtypeStruct((n, D_MODEL), F32),
        scratch_shapes=[pltpu.VMEM((TM_FFN, D_MODEL), BF16)],
        input_output_aliases={0: 0},
        name="ffn",
        compiler_params=pltpu.CompilerParams(
            dimension_semantics=("parallel", "arbitrary"), vmem_limit_bytes=FFN_VMEM_LIMIT),
    )(h, gain, wg, wu, wd)


def _ffn_first(x, lead, gain, wg, wu, wd, nb):
    bsz = x.shape[0]
    n = bsz * nb * BLK
    nblk = TM_FFN // BLK

    def x_spec(t):
        return pl.BlockSpec((None, BLK, D_MODEL),
                            lambda i, k: ((i * nblk + t) // nb, jnp.maximum((i * nblk + t) % nb - 1, 0), 0))

    return pl.pallas_call(
        functools.partial(_ffn_first_kernel, nb=nb),
        grid=(n // TM_FFN, D_FF // TF),
        in_specs=[x_spec(t) for t in range(nblk)]
        + [pl.BlockSpec((BLK, D_MODEL), lambda i, k: (0, 0))] + _ffn_weight_specs(0),
        out_specs=pl.BlockSpec((TM_FFN, D_MODEL), lambda i, k: (i, 0)),
        out_shape=jax.ShapeDtypeStruct((n, D_MODEL), F32),
        scratch_shapes=[pltpu.VMEM((TM_FFN, D_MODEL), BF16)],
        name="ffn_first",
        compiler_params=pltpu.CompilerParams(
            dimension_semantics=("parallel", "arbitrary"), vmem_limit_bytes=FFN_VMEM_LIMIT),
    )(*([x] * nblk), lead, gain, wg, wu, wd)


def _proj_kernel(h_ref, g_ref, w_ref, wt_ref, o_ref, avt_ref, bvt_ref, cvt_ref, iwt_ref, xn_ref):
    j = pl.program_id(1)

    @pl.when(j == 0)
    def _():
        xn = _rms(h_ref[...], g_ref[...]).astype(BF16)
        xn_ref[...] = xn
        vt = _nt(wt_ref[...], xn)
        for t in range(TM // BLK):
            cols = slice(t * BLK, (t + 1) * BLK)
            avt_ref[t] = vt[ROW_AV:ROW_BV, cols].astype(BF16)
            bvt_ref[t] = vt[ROW_BV:ROW_CV, cols].astype(BF16)
            cvt_ref[t] = vt[ROW_CV:ROW_IW, cols].astype(BF16)
            iwt_ref[t] = vt[ROW_IW:ROW_IW + IDX_HEADS, cols]

    o_ref[...] = _mm(xn_ref[...], w_ref[...]).astype(BF16)


def _proj(h, gain, w, wt, layer):
    n = h.shape[0]
    nt = n // BLK
    tpb = TM // BLK
    return pl.pallas_call(
        _proj_kernel,
        grid=(n // TM, N_PROJ // TN),
        in_specs=[
            pl.BlockSpec((TM, D_MODEL), lambda i, j: (i, 0)),
            pl.BlockSpec((None, 1, D_MODEL), lambda i, j: (layer, 0, 0)),
            pl.BlockSpec((None, D_MODEL, TN), lambda i, j: (layer, 0, j)),
            pl.BlockSpec((None, N_PROJ_T, D_MODEL), lambda i, j: (layer, 0, 0)),
        ],
        out_specs=[
            pl.BlockSpec((TM, TN), lambda i, j: (i, j)),
            pl.BlockSpec((tpb, 2 * HEAD_DIM, BLK), lambda i, j: (i, 0, 0)),
            pl.BlockSpec((tpb, B_HEADS * B_V_DIM, BLK), lambda i, j: (i, 0, 0)),
            pl.BlockSpec((tpb, HEAD_DIM, BLK), lambda i, j: (i, 0, 0)),
            pl.BlockSpec((tpb, IDX_HEADS, BLK), lambda i, j: (i, 0, 0)),
        ],
        out_shape=[
            jax.ShapeDtypeStruct((n, N_PROJ), BF16),
            jax.ShapeDtypeStruct((nt, 2 * HEAD_DIM, BLK), BF16),
            jax.ShapeDtypeStruct((nt, B_HEADS * B_V_DIM, BLK), BF16),
            jax.ShapeDtypeStruct((nt, HEAD_DIM, BLK), BF16),
            jax.ShapeDtypeStruct((nt, IDX_HEADS, BLK), F32),
        ],
        scratch_shapes=[pltpu.VMEM((TM, D_MODEL), BF16)],
        name="mixer_proj",
        compiler_params=pltpu.CompilerParams(
            dimension_semantics=("parallel", "arbitrary"), vmem_limit_bytes=VMEM_LIMIT),
    )(h, gain, w, wt)


def _tile_iotas():
    kl = lax.broadcasted_iota(jnp.int32, (BLK, BLK), 0)
    ql = lax.broadcasted_iota(jnp.int32, (BLK, BLK), 1)
    return kl, ql


def _colmax(x):
    return jnp.max(x, axis=0, keepdims=True)


def _colsum(x):
    return jnp.sum(x, axis=0, keepdims=True)


def _pair_tiles(p, nb):
    ja = 2 * p
    return ((ja, ja), (ja + 1, jnp.minimum(ja + 1, nb - 1)))


A_GRP = A_HEADS // A_KV_HEADS
A_LANES = A_GRP * BLK


def _mixer_a_kernel(q_ref, k0_ref, k1_ref, k2_ref, v0_ref, v1_ref, v2_ref, bias_ref, sink_ref, o_ref, qst_ref):
    i = pl.program_id(1)
    kl = lax.broadcasted_iota(jnp.int32, (BLK, A_LANES), 0)
    ql = lax.broadcasted_iota(jnp.int32, (BLK, A_LANES), 1) & (BLK - 1)
    qpos = BLK * i - N_PAD + ql
    ok_meta = (kl >= N_PAD) & (qpos >= kl - N_PAD)
    ok_prev = (BLK * (i - 1) - N_PAD + kl >= N_META) & (ql < kl)
    ok_cur = (BLK * i - N_PAD + kl >= N_META) & (ql >= kl)
    t_meta = jnp.minimum(i, 2)
    qst_ref[...] = jnp.zeros_like(qst_ref)
    for h in range(A_HEADS):
        g, r = divmod(h, A_GRP)
        qst_ref[g, r * BLK:(r + 1) * BLK, g * HEAD_DIM:(g + 1) * HEAD_DIM] = (
            q_ref[:, h * HEAD_DIM:(h + 1) * HEAD_DIM] * QK_SCALE)
    outs = []
    for g in range(A_KV_HEADS):
        qs = qst_ref[g]
        s0 = jnp.where(ok_meta, _nt(k0_ref[...], qs) + bias_ref[g, t_meta], NEG)
        s1 = jnp.where(ok_prev, _nt(k1_ref[...], qs) + bias_ref[g, 1], NEG)
        s2 = jnp.where(ok_cur, _nt(k2_ref[...], qs) + bias_ref[g, 0], NEG)
        sink = sink_ref[g]
        m = jnp.maximum(jnp.maximum(_colmax(s0), _colmax(s1)), jnp.maximum(_colmax(s2), sink))
        p0 = jnp.exp(s0 - m)
        p1 = jnp.exp(s1 - m)
        p2 = jnp.exp(s2 - m)
        den = _colsum(p0) + _colsum(p1) + _colsum(p2) + jnp.exp(sink - m)
        vs = slice(g * HEAD_DIM, (g + 1) * HEAD_DIM)
        ot = (_mm(v0_ref[vs, :], p0.astype(BF16)) + _mm(v1_ref[vs, :], p1.astype(BF16))
              + _mm(v2_ref[vs, :], p2.astype(BF16))) / den
        outs.extend(ot[:, r * BLK:(r + 1) * BLK] for r in range(A_GRP))
    o_ref[...] = jnp.concatenate(outs, axis=0).T.astype(BF16)


def _mixer_a(proj, avt, bias, sinks, bsz, nb):
    n = proj.shape[0]
    kcol = COL_AK // BLK
    rowq = lambda b, i: (b * nb + i, 0)
    spec_k = lambda f: pl.BlockSpec((BLK, BLK), lambda b, i: (b * nb + f(i), kcol))
    spec_v = lambda f: pl.BlockSpec((None, 2 * HEAD_DIM, BLK), lambda b, i: (b * nb + f(i), 0, 0))
    first = lambda i: 0
    prev = lambda i: jnp.maximum(i - 1, 0)
    cur = lambda i: i
    return pl.pallas_call(
        _mixer_a_kernel,
        grid=(bsz, nb),
        in_specs=[
            pl.BlockSpec((BLK, A_HEADS * HEAD_DIM), rowq),
            spec_k(first), spec_k(prev), spec_k(cur),
            spec_v(first), spec_v(prev), spec_v(cur),
            pl.BlockSpec((A_KV_HEADS, 3, BLK, A_LANES), lambda b, i: (0, 0, 0, 0)),
            pl.BlockSpec((A_KV_HEADS, 1, A_LANES), lambda b, i: (0, 0, 0)),
        ],
        out_specs=pl.BlockSpec((BLK, A_HEADS * HEAD_DIM), rowq),
        out_shape=jax.ShapeDtypeStruct((n, A_HEADS * HEAD_DIM), BF16),
        scratch_shapes=[pltpu.VMEM((A_KV_HEADS, A_LANES, BLK), BF16)],
        name="mixer_a",
        compiler_params=pltpu.CompilerParams(dimension_semantics=("parallel", "parallel")),
    )(proj, proj, proj, proj, avt, avt, avt, bias, sinks)


B_GROUPS = B_HEADS // 2


def _mixer_b_kernel(q1_ref, q2_ref, k1_ref, k2_ref, vt_ref, bias_ref, lam_ref, subln_ref, o_ref,
                    qbd_ref, acc_ref, *, lam_init, nb):
    i = pl.program_id(1)
    npair = (i + 2) // 2
    w2 = 2 * BLK
    kl = lax.broadcasted_iota(jnp.int32, (BLK, w2), 0)
    qpos = BLK * i - N_PAD + (lax.broadcasted_iota(jnp.int32, (BLK, w2), 1) & (BLK - 1))

    qbd_ref[...] = jnp.zeros_like(qbd_ref)
    for mp, q_ref in enumerate((q1_ref, q2_ref)):
        for g in range(B_GROUPS):
            lo = g * BLK
            qbd_ref[mp, g, 0:BLK, 0:B_QK_DIM] = q_ref[:, lo:lo + B_QK_DIM] * QK_SCALE
            qbd_ref[mp, g, BLK:w2, B_QK_DIM:BLK] = q_ref[:, lo + B_QK_DIM:lo + BLK] * QK_SCALE
    acc_ref[...] = jnp.zeros_like(acc_ref)

    def pair(p, carry):
        ms, ls = carry
        tiles = _pair_tiles(p, nb)
        oks, ts, rows = [], [], []
        for jl, jk in tiles:
            kpos = BLK * jl - N_PAD + kl
            oks.append((kpos >= 0) & (qpos >= kpos))
            ts.append(jnp.clip(i - jl, 0, 2))
            rows.append(pl.ds(pl.multiple_of(jk * BLK, BLK), BLK))
        new_ms, new_ls = [], []
        ps = [[None] * B_GROUPS for _ in range(2)]
        alphas = [[None] * B_GROUPS for _ in range(2)]
        for mp, k_ref in enumerate((k1_ref, k2_ref)):
            row_m, row_l = [], []
            for g in range(B_GROUPS):
                ss = []
                for tl in range(2):
                    s = _nt(k_ref[rows[tl], g * BLK:(g + 1) * BLK], qbd_ref[mp, g])
                    ss.append(jnp.where(oks[tl], s + bias_ref[g, ts[tl]], NEG))
                m_old = ms[mp][g]
                m_new = jnp.maximum(m_old, jnp.maximum(_colmax(ss[0]), _colmax(ss[1])))
                alpha = jnp.exp(m_old - m_new)
                pa = jnp.exp(ss[0] - m_new)
                pb = jnp.exp(ss[1] - m_new)
                row_m.append(m_new)
                row_l.append(alpha * ls[mp][g] + _colsum(pa) + _colsum(pb))
                ps[mp][g] = (pa.astype(BF16), pb.astype(BF16))
                alphas[mp][g] = alpha
            new_ms.append(row_m)
            new_ls.append(row_l)
        for h in range(B_HEADS):
            g, hs = h // 2, slice((h % 2) * BLK, (h % 2 + 1) * BLK)
            alpha = jnp.concatenate([alphas[0][g][:, hs], alphas[1][g][:, hs]], axis=1)
            upd = alpha * acc_ref[h]
            for tl, (jl, jk) in enumerate(tiles):
                p12 = jnp.concatenate([ps[0][g][tl][:, hs], ps[1][g][tl][:, hs]], axis=1)
                upd = upd + _mm(vt_ref[jk, h * B_V_DIM:(h + 1) * B_V_DIM, :], p12)
            acc_ref[h] = upd
        return new_ms, new_ls

    init_m = [[jnp.full((1, w2), NEG, F32) for _ in range(B_GROUPS)] for _ in range(2)]
    init_l = [[jnp.zeros((1, w2), F32) for _ in range(B_GROUPS)] for _ in range(2)]
    _, ls = lax.fori_loop(0, npair, pair, (init_m, init_l))

    lf = lam_ref[...]
    lam = (jnp.exp(jnp.sum(lf[0:1] * lf[1:2], axis=-1, keepdims=True))
           - jnp.exp(jnp.sum(lf[2:3] * lf[3:4], axis=-1, keepdims=True)) + lam_init)
    outs = []
    for h in range(B_HEADS):
        g, hs = h // 2, slice((h % 2) * BLK, (h % 2 + 1) * BLK)
        acc = acc_ref[h]
        o = acc[:, :BLK] / ls[0][g][:, hs] - lam * (acc[:, BLK:] / ls[1][g][:, hs])
        ms_ = jnp.mean(o * o, axis=0, keepdims=True)
        outs.append(o * lax.rsqrt(ms_ + EPS) * subln_ref[...] * (1.0 - lam_init))
    o_ref[...] = jnp.concatenate(outs, axis=0).T.astype(BF16)


def _mixer_b(proj, bvt, bias, lam, subln, bsz, nb, lam_init):
    n = proj.shape[0]
    p_rows = nb * BLK
    w = B_HEADS * B_QK_DIM
    rowq = lambda c: (lambda b, i: (b * nb + i, c))
    return pl.pallas_call(
        functools.partial(_mixer_b_kernel, lam_init=lam_init, nb=nb),
        grid=(bsz, nb),
        in_specs=[
            pl.BlockSpec((BLK, w), rowq(COL_BQ1 // w)),
            pl.BlockSpec((BLK, w), rowq(COL_BQ2 // w)),
            pl.BlockSpec((p_rows, w), lambda b, i: (b, COL_BK1 // w)),
            pl.BlockSpec((p_rows, w), lambda b, i: (b, COL_BK2 // w)),
            pl.BlockSpec((nb, B_HEADS * B_V_DIM, BLK), lambda b, i: (b, 0, 0)),
            pl.BlockSpec((B_GROUPS, 3, BLK, 2 * BLK), lambda b, i: (0, 0, 0, 0)),
            pl.BlockSpec((None, 4, B_QK_DIM), lambda b, i: (0, 0, 0)),
            pl.BlockSpec((None, B_V_DIM, BLK), lambda b, i: (0, 0, 0)),
        ],
        out_specs=pl.BlockSpec((BLK, B_HEADS * B_V_DIM), lambda b, i: (b * nb + i, 0)),
        out_shape=jax.ShapeDtypeStruct((n, B_HEADS * B_V_DIM), BF16),
        scratch_shapes=[
            pltpu.VMEM((2, B_GROUPS, 2 * BLK, BLK), BF16),
            pltpu.VMEM((B_HEADS, B_V_DIM, 2 * BLK), F32),
        ],
        name="mixer_b",
        compiler_params=pltpu.CompilerParams(dimension_semantics=("parallel", "arbitrary")),
    )(proj, proj, proj, proj, bvt, bias, lam, subln)


C_LANES = C_HEADS * BLK
C_CHUNK = 2 * BLK


def _ordered_bits_to_float(u):
    bits = jnp.where(u < 0, u ^ jnp.int32(-2 ** 31), ~u)
    return lax.bitcast_convert_type(bits, F32)


def _fold8(x):
    parts = [x[r:r + 8, :] for r in range(0, BLK, 8)]
    while len(parts) > 1:
        parts = [parts[t] + parts[t + 1] for t in range(0, len(parts), 2)]
    return parts[0]


def _mixer_c_kernel(cq_ref, iq_ref, kk_ref, vt_ref, iwt_ref, bias_ref, o_ref,
                    score_ref, sel_ref, qst_ref, iqst_ref, acc_ref, *, topk, nb):
    i = pl.program_id(1)
    npair = (i + 2) // 2
    kl, ql = _tile_iotas()
    qpos = BLK * i - N_PAD + ql
    kf = jnp.float32(topk)

    qst_ref[...] = jnp.zeros_like(qst_ref)
    iqst_ref[...] = jnp.zeros_like(iqst_ref)
    for h in range(C_HEADS):
        qst_ref[h * BLK:(h + 1) * BLK, 0:HEAD_DIM] = cq_ref[:, h * HEAD_DIM:(h + 1) * HEAD_DIM] * QK_SCALE
        iqst_ref[h * BLK:(h + 1) * BLK, HEAD_DIM:BLK] = iq_ref[:, h * IDX_DIM:(h + 1) * IDX_DIM]
    iw = iwt_ref[...] * QK_SCALE
    iw_all = jnp.concatenate([iw[h:h + 1, :] for h in range(IDX_HEADS)], axis=1)

    def admissible(jl):
        kpos = BLK * jl - N_PAD + kl
        return (kpos >= 0) & (qpos >= kpos)

    def score_pair(p, carry):
        for jl, jk in _pair_tiles(p, nb):
            rows = pl.ds(pl.multiple_of(jk * BLK, BLK), BLK)
            kk = kk_ref[rows, :]
            sc = None
            for c in range(C_LANES // C_CHUNK):
                ln = slice(c * C_CHUNK, (c + 1) * C_CHUNK)
                rel = jnp.maximum(_nt(kk, iqst_ref[ln, :]), 0.0) * iw_all[:, ln]
                for h in range(C_CHUNK // BLK):
                    part = rel[:, h * BLK:(h + 1) * BLK]
                    sc = part if sc is None else sc + part
            score_ref[jl] = jnp.where(admissible(jl), sc * IDX_HEADS ** -0.5, NEG)
        return carry

    lax.fori_loop(0, npair, score_pair, 0)

    def count(pred):
        def body(p, c):
            xa = jnp.where(pred(score_ref[2 * p]), 1.0, 0.0)
            xb = jnp.where(pred(score_ref[2 * p + 1]), 1.0, 0.0)
            return c + (_fold8(xa) + _fold8(xb))
        return _colsum(lax.fori_loop(0, npair, body, jnp.zeros((8, BLK), F32)))

    def search(it, u):
        cand = u | lax.shift_left(jnp.int32(1), 31 - it)
        thr_c = _ordered_bits_to_float(cand)
        return jnp.where(count(lambda s: s >= thr_c) >= kf, cand, u)

    u = lax.fori_loop(0, 32, search, jnp.zeros((1, BLK), jnp.int32))
    found = (u < 0) | (u >= jnp.int32(0x00800000))
    thr = jnp.where(found, _ordered_bits_to_float(u), -jnp.inf)
    need = kf - count(lambda s: s > thr)
    n_ge = count(lambda s: s >= thr)
    has_tie = jnp.max(jnp.where((n_ge > kf) & (thr > NEG), 1.0, 0.0)) > 0.5

    def select_plain():
        def body(p, carry):
            for jl in (2 * p, 2 * p + 1):
                sel_ref[jl] = jnp.where(admissible(jl) & (score_ref[jl] >= thr), 1.0, 0.0)
            return carry
        lax.fori_loop(0, npair, body, 0)

    def select_ties():
        tri = (kl >= ql).astype(BF16)

        def body(jl, seen):
            sc = score_ref[jl]
            eq = sc == thr
            eqf = jnp.where(eq, 1.0, 0.0)
            rank = seen + _mm(tri, eqf.astype(BF16))
            sel = admissible(jl) & ((sc > thr) | (eq & (rank <= need)))
            sel_ref[jl] = jnp.where(sel, 1.0, 0.0)
            return seen + _colsum(eqf)
        lax.fori_loop(0, 2 * npair, body, jnp.zeros((1, BLK), F32))

    lax.cond(has_tie, select_ties, select_plain)

    acc_ref[...] = jnp.zeros_like(acc_ref)

    def attend_pair(p, carry):
        m_old, l_old = carry
        tiles = _pair_tiles(p, nb)
        ss = []
        for jl, jk in tiles:
            rows = pl.ds(pl.multiple_of(jk * BLK, BLK), BLK)
            s = _nt(kk_ref[rows, :], qst_ref[...]) + bias_ref[jnp.clip(i - jl, 0, 2)]
            sel = jnp.tile(sel_ref[jl], (1, C_HEADS)) > 0.5
            ss.append(jnp.where(sel, s, NEG))
        m_new = jnp.maximum(m_old, jnp.maximum(_colmax(ss[0]), _colmax(ss[1])))
        alpha = jnp.exp(m_old - m_new)
        pa = jnp.exp(ss[0] - m_new)
        pb = jnp.exp(ss[1] - m_new)
        l_new = alpha * l_old + _colsum(pa) + _colsum(pb)
        acc_ref[...] = (alpha * acc_ref[...] + _mm(vt_ref[tiles[0][1]], pa.astype(BF16))
                        + _mm(vt_ref[tiles[1][1]], pb.astype(BF16)))
        return m_new, l_new

    init = (jnp.full((1, C_LANES), NEG, F32), jnp.zeros((1, C_LANES), F32))
    _, l_fin = lax.fori_loop(0, npair, attend_pair, init)
    o = acc_ref[...] / l_fin
    o_rows = jnp.concatenate([o[:, h * BLK:(h + 1) * BLK] for h in range(C_HEADS)], axis=0)
    o_ref[...] = o_rows.T.astype(BF16)


def _mixer_c(proj, cvt, iwt, bias, bsz, nb, topk):
    n = proj.shape[0]
    p_rows = nb * BLK
    w = C_HEADS * HEAD_DIM
    return pl.pallas_call(
        functools.partial(_mixer_c_kernel, topk=topk, nb=nb),
        grid=(bsz, nb),
        in_specs=[
            pl.BlockSpec((BLK, w), lambda b, i: (b * nb + i, COL_CQ // w)),
            pl.BlockSpec((BLK, w), lambda b, i: (b * nb + i, COL_IQ // w)),
            pl.BlockSpec((p_rows, BLK), lambda b, i: (b, COL_CKIK // BLK)),
            pl.BlockSpec((nb, HEAD_DIM, BLK), lambda b, i: (b, 0, 0)),
            pl.BlockSpec((None, IDX_HEADS, BLK), lambda b, i: (b * nb + i, 0, 0)),
            pl.BlockSpec((3, BLK, C_LANES), lambda b, i: (0, 0, 0)),
        ],
        out_specs=pl.BlockSpec((BLK, w), lambda b, i: (b * nb + i, 0)),
        out_shape=jax.ShapeDtypeStruct((n, w), BF16),
        scratch_shapes=[
            pltpu.VMEM((nb + 1, BLK, BLK), F32),
            pltpu.VMEM((nb + 1, BLK, BLK), F32),
            pltpu.VMEM((C_LANES, BLK), BF16),
            pltpu.VMEM((C_LANES, BLK), BF16),
            pltpu.VMEM((HEAD_DIM, C_LANES), F32),
        ],
        name="mixer_c",
        compiler_params=pltpu.CompilerParams(dimension_semantics=("parallel", "arbitrary")),
    )(proj, proj, proj, cvt, iwt, bias)


def _merge_kernel(h_ref, g_ref, oa_ref, ob_ref, oc_ref, ga_ref, gb_ref, gc_ref, wa_ref, wb_ref, wc_ref, wo_ref,
                  o_ref, xn_ref, acc_ref, *, nc):
    c = pl.program_id(1)

    @pl.when(c == 0)
    def _():
        xn_ref[...] = _rms(h_ref[...], g_ref[...]).astype(BF16)
        acc_ref[...] = jnp.zeros_like(acc_ref)

    xn = xn_ref[...]
    y = (jax.nn.sigmoid(_mm(xn, ga_ref[...])) * _mm(oa_ref[...], wa_ref[...])
         + jax.nn.sigmoid(_mm(xn, gb_ref[...])) * _mm(ob_ref[...], wb_ref[...])
         + jax.nn.sigmoid(_mm(xn, gc_ref[...])) * _mm(oc_ref[...], wc_ref[...]))
    acc_ref[...] += _mm(y.astype(BF16), wo_ref[...])

    @pl.when(c == nc - 1)
    def _():
        o_ref[...] = h_ref[...] + acc_ref[...]


def _merge(h, gain, oa, ob, oc, wgates, wbr, wout, layer):
    n = h.shape[0]
    tc = 512
    nc = D_MODEL // tc
    bw = 512
    row = lambda i, c: (i, 0)
    wgate = lambda br: pl.BlockSpec((None, D_MODEL, tc), lambda i, c: (layer, 0, br * nc + c))
    wbranch = lambda br: pl.BlockSpec((None, None, bw, tc), lambda i, c: (layer, br, 0, c))
    return pl.pallas_call(
        functools.partial(_merge_kernel, nc=nc),
        grid=(n // TM, nc),
        in_specs=[
            pl.BlockSpec((TM, D_MODEL), row),
            pl.BlockSpec((None, 1, D_MODEL), lambda i, c: (layer, 0, 0)),
            pl.BlockSpec((TM, bw), row), pl.BlockSpec((TM, bw), row), pl.BlockSpec((TM, bw), row),
            wgate(0), wgate(1), wgate(2),
            wbranch(0), wbranch(1), wbranch(2),
            pl.BlockSpec((None, tc, D_MODEL), lambda i, c: (layer, c, 0)),
        ],
        out_specs=pl.BlockSpec((TM, D_MODEL), row),
        out_shape=jax.ShapeDtypeStruct((n, D_MODEL), F32),
        scratch_shapes=[pltpu.VMEM((TM, D_MODEL), BF16), pltpu.VMEM((TM, D_MODEL), F32)],
        input_output_aliases={0: 0},
        name="branch_merge",
        compiler_params=pltpu.CompilerParams(
            dimension_semantics=("parallel", "arbitrary"), vmem_limit_bytes=VMEM_LIMIT),
    )(h, gain, oa, ob, oc, wgates, wgates, wgates, wbr, wbr, wbr, wout)


def _final_kernel(h_ref, g_ref, o_ref):
    o_ref[...] = _rms(h_ref[...], g_ref[...])


def _final_norm(h, gain, bsz, nb):
    seq = (nb - 1) * BLK
    tr = 512 if seq % 512 == 0 else BLK
    return pl.pallas_call(
        _final_kernel,
        grid=(bsz, seq // tr),
        in_specs=[
            pl.BlockSpec((pl.Element(tr), pl.Element(D_MODEL)),
                         lambda b, r: (pl.multiple_of((b * nb + 1) * BLK + r * tr, BLK), 0)),
            pl.BlockSpec((1, D_MODEL), lambda b, r: (0, 0)),
        ],
        out_specs=pl.BlockSpec((None, tr, D_MODEL), lambda b, r: (b, r, 0)),
        out_shape=jax.ShapeDtypeStruct((bsz, seq, D_MODEL), F32),
        name="final_norm",
        compiler_params=pltpu.CompilerParams(dimension_semantics=("parallel", "parallel")),
    )(h, gain)


def _bucket_np(d):
    max_exact = N_BUCKETS // 2
    d = np.maximum(d, 0)
    df = np.maximum(d, 1).astype(np.float32)
    large = max_exact + (np.log(df / max_exact) / math.log(MAX_DISTANCE / max_exact)
                         * (N_BUCKETS - max_exact)).astype(np.int32)
    return np.where(d < max_exact, d, np.minimum(large, N_BUCKETS - 1)).astype(np.int32)


def _bias_tiles(table):
    kl = np.arange(BLK)[:, None]
    ql = np.arange(BLK)[None, :]
    idx = np.stack([_bucket_np(ql - kl), _bucket_np(BLK + ql - kl),
                    np.full((BLK, BLK), N_BUCKETS - 1, np.int32)])
    onehot = (idx.reshape(-1, 1) == np.arange(N_BUCKETS)[None, :]).astype(np.float32)
    tiles = jnp.dot(onehot, table.astype(F32), precision=lax.Precision.HIGHEST)
    return jnp.transpose(tiles.reshape(3, BLK, BLK, -1), (3, 0, 1, 2))


def _pack_w_in(w_in):
    o = np.cumsum([0, 512, 128, 128, 256, 256, 256, 256, 512, 512, 64, 64, 512, 64, 8, 3 * D_MODEL])
    (aq, ak, av, bq1, bq2, bk1, bk2, bv, cq, ck, cv, iq, ik, iw, gates) = [
        w_in[:, :, o[t]:o[t + 1]] for t in range(15)]
    pad = jnp.zeros(w_in.shape[:2] + (N_PROJ - COL_CKIK - BLK,), w_in.dtype)
    w = jnp.concatenate([aq, cq, iq, bq1, bq2, bk1, bk2, ak, ck, ik, pad], axis=-1).astype(BF16)
    padt = jnp.zeros(w_in.shape[:2] + (N_PROJ_T - ROW_IW - IDX_HEADS,), w_in.dtype)
    wt = jnp.swapaxes(jnp.concatenate([av, bv, cv, iw, padt], axis=-1), 1, 2).astype(BF16)
    return w, wt, gates.astype(BF16)


def kernel(x, meta_tokens, rel_bias_table, norm_ffn1, ffn1_w_gate, ffn1_w_up, ffn1_w_down, norm_mix, w_in,
           attn_sinks, diff_lambda, diff_subln, w_branch, w_out, norm_ffn2, ffn2_w_gate, ffn2_w_up,
           ffn2_w_down, norm_final):
    bsz, seq, _ = x.shape
    nb = seq // BLK + 1
    p_rows = nb * BLK
    n = bsz * p_rows
    assert seq % BLK == 0 and n % TM_FFN == 0
    topk = min(TOPK_MAX, seq // 4)

    lead = jnp.concatenate([jnp.zeros((N_PAD, D_MODEL), x.dtype), meta_tokens.astype(x.dtype)], axis=0)

    bias = _bias_tiles(rel_bias_table)
    bias_a = jnp.transpose(bias[:A_HEADS].reshape(A_KV_HEADS, A_GRP, 3, BLK, BLK),
                           (0, 2, 3, 1, 4)).reshape(A_KV_HEADS, 3, BLK, A_LANES)
    bias_b = jnp.transpose(bias[A_HEADS:A_HEADS + B_HEADS].reshape(B_GROUPS, 2, 3, BLK, BLK),
                           (0, 2, 3, 1, 4)).reshape(B_GROUPS, 3, BLK, 2 * BLK)
    bias_c = jnp.transpose(bias[A_HEADS + B_HEADS:], (1, 2, 0, 3)).reshape(3, BLK, C_LANES)
    w_proj, w_proj_t, w_gates = _pack_w_in(w_in)
    g1, gm, g2 = (t.reshape(DEPTH, 1, D_MODEL) for t in (norm_ffn1, norm_mix, norm_ffn2))
    f1 = [t.astype(BF16) for t in (ffn1_w_gate, ffn1_w_up, ffn1_w_down)]
    f2 = [t.astype(BF16) for t in (ffn2_w_gate, ffn2_w_up, ffn2_w_down)]
    wbr, wout = w_branch.astype(BF16), w_out.astype(BF16)
    sinks = jnp.broadcast_to(attn_sinks.astype(F32)[:, :, None],
                             (DEPTH, A_HEADS, BLK)).reshape(DEPTH, A_KV_HEADS, 1, A_LANES)
    subln = jnp.broadcast_to(diff_subln.astype(F32)[:, :, None], (DEPTH, B_V_DIM, BLK))
    lam_f = diff_lambda.astype(F32)

    for l in range(DEPTH):
        h = _ffn_first(x, lead, g1, *f1, nb) if l == 0 else _ffn(h, g1, *f1, l)
        proj, avt, bvt, cvt, iwt = _proj(h, gm, w_proj, w_proj_t, l)
        o_a = _mixer_a(proj, avt, bias_a, sinks[l], bsz, nb)
        lam_init = 0.8 - 0.6 * math.exp(-0.3 * l)
        o_b = _mixer_b(proj, bvt, bias_b, lam_f[l:l + 1], subln[l:l + 1], bsz, nb, lam_init)
        o_c = _mixer_c(proj, cvt, iwt, bias_c, bsz, nb, topk)
        h = _merge(h, gm, o_a, o_b, o_c, w_gates, wbr, wout, l)
        h = _ffn(h, g2, *f2, l)

    return _final_norm(h, norm_final.reshape(1, D_MODEL), bsz, nb)
```

```python
import functools
import math

import jax
import jax.numpy as jnp
import numpy as np
from jax import lax
from jax.experimental import pallas as pl
from jax.experimental.pallas import tpu as pltpu

F32 = jnp.float32
BF16 = jnp.bfloat16

D_MODEL = 2048
D_FF = 5632
DEPTH = 2
BLK = 128
N_META = 16
N_PAD = BLK - N_META
WINDOW = 128
HEAD_DIM = 64
A_HEADS, A_KV_HEADS = 8, 2
B_HEADS, B_QK_DIM, B_V_DIM = 4, 64, 128
C_HEADS = 8
IDX_HEADS, IDX_DIM = 8, 64
TOPK_MAX = 256
N_BUCKETS, MAX_DISTANCE = 32, 128
EPS = 1e-6
NEG = -1e30

COL_AQ, COL_CQ, COL_IQ = 0, 512, 1024
COL_BQ1, COL_BQ2, COL_BK1, COL_BK2 = 1536, 1792, 2048, 2304
COL_AK, COL_CKIK = 2560, 2688
N_PROJ = 3072
ROW_AV, ROW_BV, ROW_CV, ROW_IW, N_PROJ_T = 0, 128, 640, 704, 720

TM = 512
TM_FFN = 1024
TM_PROJ = 1024
TF = 512
FFN_VMEM_LIMIT = 62 * 1024 * 1024
TN = 1024
assert HEAD_DIM == B_QK_DIM == IDX_DIM == 64
QK_SCALE = 0.125
VMEM_LIMIT = 56 * 1024 * 1024


def _nt(a, b):
    return lax.dot_general(a, b, (((1,), (1,)), ((), ())), preferred_element_type=F32)


def _mm(a, b):
    return jnp.dot(a, b, preferred_element_type=F32)


def _rms(x, g):
    ms = jnp.mean(x * x, axis=-1, keepdims=True)
    return x * lax.rsqrt(ms + EPS) * g


def _ffn_accumulate(xn_ref, wg_ref, wu_ref, wd_ref, o_ref):
    for r in range(TM_FFN // TM):
        rows = slice(r * TM, (r + 1) * TM)
        xn = xn_ref[rows, :]
        a = _mm(xn, wg_ref[...])
        b = _mm(xn, wu_ref[...])
        t = (a * jax.nn.sigmoid(a)) * b
        o_ref[rows, :] += 0.5 * _mm(t.astype(BF16), wd_ref[...])


def _ffn_kernel(h_ref, g_ref, wg_ref, wu_ref, wd_ref, o_ref, xn_ref):
    @pl.when(pl.program_id(1) == 0)
    def _():
        h = h_ref[...]
        xn_ref[...] = _rms(h, g_ref[...]).astype(BF16)
        o_ref[...] = h

    _ffn_accumulate(xn_ref, wg_ref, wu_ref, wd_ref, o_ref)


def _ffn_first_kernel(*refs, nb):
    nblk = TM_FFN // BLK
    x_refs = refs[:nblk]
    lead_ref, g_ref, wg_ref, wu_ref, wd_ref, o_ref, xn_ref = refs[nblk:]

    @pl.when(pl.program_id(1) == 0)
    def _():
        first = pl.program_id(0) * nblk
        for t in range(nblk):
            rows = slice(t * BLK, (t + 1) * BLK)
            blk = jnp.where(lax.rem(first + t, nb) == 0, lead_ref[...], x_refs[t][...])
            xn_ref[rows, :] = _rms(blk, g_ref[...]).astype(BF16)
            o_ref[rows, :] = blk

    _ffn_accumulate(xn_ref, wg_ref, wu_ref, wd_ref, o_ref)


def _ffn_weight_specs(layer):
    return [
        pl.BlockSpec((None, 1, D_MODEL), lambda i, k: (layer, 0, 0)),
        pl.BlockSpec((None, D_MODEL, TF), lambda i, k: (layer, 0, k)),
        pl.BlockSpec((None, D_MODEL, TF), lambda i, k: (layer, 0, k)),
        pl.BlockSpec((None, TF, D_MODEL), lambda i, k: (layer, k, 0)),
    ]


def _ffn(h, gain, wg, wu, wd, layer):
    n = h.shape[0]
    return pl.pallas_call(
        _ffn_kernel,
        grid=(n // TM_FFN, D_FF // TF),
        in_specs=[pl.BlockSpec((TM_FFN, D_MODEL), lambda i, k: (i, 0))] + _ffn_weight_specs(layer),
        out_specs=pl.BlockSpec((TM_FFN, D_MODEL), lambda i, k: (i, 0)),
        out_shape=jax.ShapeDtypeStruct((n, D_MODEL), F32),
        scratch_shapes=[pltpu.VMEM((TM_FFN, D_MODEL), BF16)],
        input_output_aliases={0: 0},
        name="ffn",
        compiler_params=pltpu.CompilerParams(
            dimension_semantics=("parallel", "arbitrary"), vmem_limit_bytes=FFN_VMEM_LIMIT),
    )(h, gain, wg, wu, wd)


def _ffn_first(x, lead, gain, wg, wu, wd, nb):
    bsz = x.shape[0]
    n = bsz * nb * BLK
    nblk = TM_FFN // BLK

    def x_spec(t):
        return pl.BlockSpec((None, BLK, D_MODEL),
                            lambda i, k: ((i * nblk + t) // nb, jnp.maximum((i * nblk + t) % nb - 1, 0), 0))

    return pl.pallas_call(
        functools.partial(_ffn_first_kernel, nb=nb),
        grid=(n // TM_FFN, D_FF // TF),
        in_specs=[x_spec(t) for t in range(nblk)]
        + [pl.BlockSpec((BLK, D_MODEL), lambda i, k: (0, 0))] + _ffn_weight_specs(0),
        out_specs=pl.BlockSpec((TM_FFN, D_MODEL), lambda i, k: (i, 0)),
        out_shape=jax.ShapeDtypeStruct((n, D_MODEL), F32),
        scratch_shapes=[pltpu.VMEM((TM_FFN, D_MODEL), BF16)],
        name="ffn_first",
        compiler_params=pltpu.CompilerParams(
            dimension_semantics=("parallel", "arbitrary"), vmem_limit_bytes=FFN_VMEM_LIMIT),
    )(*([x] * nblk), lead, gain, wg, wu, wd)


def _proj_kernel(h_ref, g_ref, w_ref, wt_ref, o_ref, avt_ref, bvt_ref, cvt_ref, iwt_ref, xn_ref):
    j = pl.program_id(1)

    @pl.when(j == 0)
    def _():
        xn = _rms(h_ref[...], g_ref[...]).astype(BF16)
        xn_ref[...] = xn
        vt = _nt(wt_ref[...], xn)
        for t in range(TM_PROJ // BLK):
            cols = slice(t * BLK, (t + 1) * BLK)
            avt_ref[t] = vt[ROW_AV:ROW_BV, cols].astype(BF16)
            bvt_ref[t] = vt[ROW_BV:ROW_CV, cols].astype(BF16)
            cvt_ref[t] = vt[ROW_CV:ROW_IW, cols].astype(BF16)
            iwt_ref[t] = vt[ROW_IW:ROW_IW + IDX_HEADS, cols]

    o_ref[...] = _mm(xn_ref[...], w_ref[...]).astype(BF16)


def _proj(h, gain, w, wt, layer):
    n = h.shape[0]
    nt = n // BLK
    tpb = TM_PROJ // BLK
    return pl.pallas_call(
        _proj_kernel,
        grid=(n // TM_PROJ, N_PROJ // TN),
        in_specs=[
            pl.BlockSpec((TM_PROJ, D_MODEL), lambda i, j: (i, 0)),
            pl.BlockSpec((None, 1, D_MODEL), lambda i, j: (layer, 0, 0)),
            pl.BlockSpec((None, D_MODEL, TN), lambda i, j: (layer, 0, j)),
            pl.BlockSpec((None, N_PROJ_T, D_MODEL), lambda i, j: (layer, 0, 0)),
        ],
        out_specs=[
            pl.BlockSpec((TM_PROJ, TN), lambda i, j: (i, j)),
            pl.BlockSpec((tpb, 2 * HEAD_DIM, BLK), lambda i, j: (i, 0, 0)),
            pl.BlockSpec((tpb, B_HEADS * B_V_DIM, BLK), lambda i, j: (i, 0, 0)),
            pl.BlockSpec((tpb, HEAD_DIM, BLK), lambda i, j: (i, 0, 0)),
            pl.BlockSpec((tpb, IDX_HEADS, BLK), lambda i, j: (i, 0, 0)),
        ],
        out_shape=[
            jax.ShapeDtypeStruct((n, N_PROJ), BF16),
            jax.ShapeDtypeStruct((nt, 2 * HEAD_DIM, BLK), BF16),
            jax.ShapeDtypeStruct((nt, B_HEADS * B_V_DIM, BLK), BF16),
            jax.ShapeDtypeStruct((nt, HEAD_DIM, BLK), BF16),
            jax.ShapeDtypeStruct((nt, IDX_HEADS, BLK), F32),
        ],
        scratch_shapes=[pltpu.VMEM((TM_PROJ, D_MODEL), BF16)],
        name="mixer_proj",
        compiler_params=pltpu.CompilerParams(
            dimension_semantics=("parallel", "arbitrary"), vmem_limit_bytes=VMEM_LIMIT),
    )(h, gain, w, wt)


def _tile_iotas():
    kl = lax.broadcasted_iota(jnp.int32, (BLK, BLK), 0)
    ql = lax.broadcasted_iota(jnp.int32, (BLK, BLK), 1)
    return kl, ql


def _colmax(x):
    return jnp.max(x, axis=0, keepdims=True)


def _colsum(x):
    return jnp.sum(x, axis=0, keepdims=True)


def _pair_tiles(p, nb):
    ja = 2 * p
    return ((ja, ja), (ja + 1, jnp.minimum(ja + 1, nb - 1)))


A_GRP = A_HEADS // A_KV_HEADS
A_LANES = A_GRP * BLK


def _mixer_a_kernel(q_ref, k0_ref, k1_ref, k2_ref, v0_ref, v1_ref, v2_ref, bias_ref, sink_ref, o_ref, qst_ref):
    i = pl.program_id(1)
    kl = lax.broadcasted_iota(jnp.int32, (BLK, A_LANES), 0)
    ql = lax.broadcasted_iota(jnp.int32, (BLK, A_LANES), 1) & (BLK - 1)
    qpos = BLK * i - N_PAD + ql
    ok_meta = (kl >= N_PAD) & (qpos >= kl - N_PAD)
    ok_prev = (BLK * (i - 1) - N_PAD + kl >= N_META) & (ql < kl)
    ok_cur = (BLK * i - N_PAD + kl >= N_META) & (ql >= kl)
    t_meta = jnp.minimum(i, 2)
    qst_ref[...] = jnp.zeros_like(qst_ref)
    for h in range(A_HEADS):
        g, r = divmod(h, A_GRP)
        qst_ref[g, r * BLK:(r + 1) * BLK, g * HEAD_DIM:(g + 1) * HEAD_DIM] = (
            q_ref[:, h * HEAD_DIM:(h + 1) * HEAD_DIM] * QK_SCALE)
    outs = []
    for g in range(A_KV_HEADS):
        qs = qst_ref[g]
        s0 = jnp.where(ok_meta, _nt(k0_ref[...], qs) + bias_ref[g, t_meta], NEG)
        s1 = jnp.where(ok_prev, _nt(k1_ref[...], qs) + bias_ref[g, 1], NEG)
        s2 = jnp.where(ok_cur, _nt(k2_ref[...], qs) + bias_ref[g, 0], NEG)
        sink = sink_ref[g]
        m = jnp.maximum(jnp.maximum(_colmax(s0), _colmax(s1)), jnp.maximum(_colmax(s2), sink))
        p0 = jnp.exp(s0 - m)
        p1 = jnp.exp(s1 - m)
        p2 = jnp.exp(s2 - m)
        den = _colsum(p0) + _colsum(p1) + _colsum(p2) + jnp.exp(sink - m)
        vs = slice(g * HEAD_DIM, (g + 1) * HEAD_DIM)
        ot = (_mm(v0_ref[vs, :], p0.astype(BF16)) + _mm(v1_ref[vs, :], p1.astype(BF16))
              + _mm(v2_ref[vs, :], p2.astype(BF16))) / den
        outs.extend(ot[:, r * BLK:(r + 1) * BLK] for r in range(A_GRP))
    o_ref[...] = jnp.concatenate(outs, axis=0).T.astype(BF16)


def _mixer_a(proj, avt, bias, sinks, bsz, nb):
    n = proj.shape[0]
    kcol = COL_AK // BLK
    rowq = lambda b, i: (b * nb + i, 0)
    spec_k = lambda f: pl.BlockSpec((BLK, BLK), lambda b, i: (b * nb + f(i), kcol))
    spec_v = lambda f: pl.BlockSpec((None, 2 * HEAD_DIM, BLK), lambda b, i: (b * nb + f(i), 0, 0))
    first = lambda i: 0
    prev = lambda i: jnp.maximum(i - 1, 0)
    cur = lambda i: i
    return pl.pallas_call(
        _mixer_a_kernel,
        grid=(bsz, nb),
        in_specs=[
            pl.BlockSpec((BLK, A_HEADS * HEAD_DIM), rowq),
            spec_k(first), spec_k(prev), spec_k(cur),
            spec_v(first), spec_v(prev), spec_v(cur),
            pl.BlockSpec((A_KV_HEADS, 3, BLK, A_LANES), lambda b, i: (0, 0, 0, 0)),
            pl.BlockSpec((A_KV_HEADS, 1, A_LANES), lambda b, i: (0, 0, 0)),
        ],
        out_specs=pl.BlockSpec((BLK, A_HEADS * HEAD_DIM), rowq),
        out_shape=jax.ShapeDtypeStruct((n, A_HEADS * HEAD_DIM), BF16),
        scratch_shapes=[pltpu.VMEM((A_KV_HEADS, A_LANES, BLK), BF16)],
        name="mixer_a",
        compiler_params=pltpu.CompilerParams(dimension_semantics=("parallel", "parallel")),
    )(proj, proj, proj, proj, avt, avt, avt, bias, sinks)


B_GROUPS = B_HEADS // 2


def _mixer_b_kernel(q1_ref, q2_ref, k1_ref, k2_ref, vt_ref, bias_ref, lam_ref, subln_ref, o_ref,
                    qbd_ref, acc_ref, *, lam_init, nb):
    i = pl.program_id(1)
    npair = (i + 2) // 2
    w2 = 2 * BLK
    kl = lax.broadcasted_iota(jnp.int32, (BLK, w2), 0)
    qpos = BLK * i - N_PAD + (lax.broadcasted_iota(jnp.int32, (BLK, w2), 1) & (BLK - 1))

    feat = lax.broadcasted_iota(jnp.int32, (BLK, BLK), 0)
    for mp, q_ref in enumerate((q1_ref, q2_ref)):
        for g in range(B_GROUPS):
            qt = (q_ref[:, g * BLK:(g + 1) * BLK].astype(F32) * QK_SCALE).T
            qbd_ref[mp, g] = jnp.concatenate(
                [jnp.where(feat < B_QK_DIM, qt, 0.0), jnp.where(feat >= B_QK_DIM, qt, 0.0)], axis=1).astype(BF16)
    acc_ref[...] = jnp.zeros_like(acc_ref)

    def pair(p, carry):
        ms, ls = carry
        tiles = _pair_tiles(p, nb)
        oks, ts, rows = [], [], []
        for jl, jk in tiles:
            kpos = BLK * jl - N_PAD + kl
            oks.append((kpos >= 0) & (qpos >= kpos))
            ts.append(jnp.clip(i - jl, 0, 2))
            rows.append(pl.ds(pl.multiple_of(jk * BLK, BLK), BLK))
        raw = [[[_mm(k_ref[rows[tl], g * BLK:(g + 1) * BLK], qbd_ref[mp, g]) for tl in range(2)]
                for g in range(B_GROUPS)] for mp, k_ref in enumerate((k1_ref, k2_ref))]
        new_ms, new_ls = [], []
        ps = [[None] * B_GROUPS for _ in range(2)]
        alphas = [[None] * B_GROUPS for _ in range(2)]
        for mp in range(2):
            row_m, row_l = [], []
            for g in range(B_GROUPS):
                ss = [jnp.where(oks[tl], raw[mp][g][tl] + bias_ref[g, ts[tl]], NEG) for tl in range(2)]
                m_old = ms[mp][g]
                m_new = jnp.maximum(m_old, jnp.maximum(_colmax(ss[0]), _colmax(ss[1])))
                alpha = jnp.exp(m_old - m_new)
                pa = jnp.exp(ss[0] - m_new)
                pb = jnp.exp(ss[1] - m_new)
                row_m.append(m_new)
                row_l.append(alpha * ls[mp][g] + _colsum(pa) + _colsum(pb))
                ps[mp][g] = (pa.astype(BF16), pb.astype(BF16))
                alphas[mp][g] = alpha
            new_ms.append(row_m)
            new_ls.append(row_l)
        for h in range(B_HEADS):
            g, hs = h // 2, slice((h % 2) * BLK, (h % 2 + 1) * BLK)
            alpha = jnp.concatenate([alphas[0][g][:, hs], alphas[1][g][:, hs]], axis=1)
            upd = alpha * acc_ref[h]
            for tl, (jl, jk) in enumerate(tiles):
                p12 = jnp.concatenate([ps[0][g][tl][:, hs], ps[1][g][tl][:, hs]], axis=1)
                upd = upd + _mm(vt_ref[jk, h * B_V_DIM:(h + 1) * B_V_DIM, :], p12)
            acc_ref[h] = upd
        return new_ms, new_ls

    init_m = [[jnp.full((1, w2), NEG, F32) for _ in range(B_GROUPS)] for _ in range(2)]
    init_l = [[jnp.zeros((1, w2), F32) for _ in range(B_GROUPS)] for _ in range(2)]
    _, ls = lax.fori_loop(0, npair, pair, (init_m, init_l))

    lf = lam_ref[...]
    lam = (jnp.exp(jnp.sum(lf[0:1] * lf[1:2], axis=-1, keepdims=True))
           - jnp.exp(jnp.sum(lf[2:3] * lf[3:4], axis=-1, keepdims=True)) + lam_init)
    outs = []
    for h in range(B_HEADS):
        g, hs = h // 2, slice((h % 2) * BLK, (h % 2 + 1) * BLK)
        acc = acc_ref[h]
        o = acc[:, :BLK] / ls[0][g][:, hs] - lam * (acc[:, BLK:] / ls[1][g][:, hs])
        ms_ = jnp.mean(o * o, axis=0, keepdims=True)
        outs.append(o * lax.rsqrt(ms_ + EPS) * subln_ref[...] * (1.0 - lam_init))
    o_ref[...] = jnp.concatenate(outs, axis=0).T.astype(BF16)


def _mixer_b(proj, bvt, bias, lam, subln, bsz, nb, lam_init):
    n = proj.shape[0]
    p_rows = nb * BLK
    w = B_HEADS * B_QK_DIM
    rowq = lambda c: (lambda b, i: (b * nb + i, c))
    return pl.pallas_call(
        functools.partial(_mixer_b_kernel, lam_init=lam_init, nb=nb),
        grid=(bsz, nb),
        in_specs=[
            pl.BlockSpec((BLK, w), rowq(COL_BQ1 // w)),
            pl.BlockSpec((BLK, w), rowq(COL_BQ2 // w)),
            pl.BlockSpec((p_rows, w), lambda b, i: (b, COL_BK1 // w)),
            pl.BlockSpec((p_rows, w), lambda b, i: (b, COL_BK2 // w)),
            pl.BlockSpec((nb, B_HEADS * B_V_DIM, BLK), lambda b, i: (b, 0, 0)),
            pl.BlockSpec((B_GROUPS, 3, BLK, 2 * BLK), lambda b, i: (0, 0, 0, 0)),
            pl.BlockSpec((None, 4, B_QK_DIM), lambda b, i: (0, 0, 0)),
            pl.BlockSpec((None, B_V_DIM, BLK), lambda b, i: (0, 0, 0)),
        ],
        out_specs=pl.BlockSpec((BLK, B_HEADS * B_V_DIM), lambda b, i: (b * nb + i, 0)),
        out_shape=jax.ShapeDtypeStruct((n, B_HEADS * B_V_DIM), BF16),
        scratch_shapes=[
            pltpu.VMEM((2, B_GROUPS, BLK, 2 * BLK), BF16),
            pltpu.VMEM((B_HEADS, B_V_DIM, 2 * BLK), F32),
        ],
        name="mixer_b",
        compiler_params=pltpu.CompilerParams(dimension_semantics=("parallel", "arbitrary")),
    )(proj, proj, proj, proj, bvt, bias, lam, subln)


C_LANES = C_HEADS * BLK
C_CHUNK = 2 * BLK


def _ordered_bits_to_float(u):
    bits = jnp.where(u < 0, u ^ jnp.int32(-2 ** 31), ~u)
    return lax.bitcast_convert_type(bits, F32)


def _fold8(x):
    parts = [x[r:r + 8, :] for r in range(0, BLK, 8)]
    while len(parts) > 1:
        parts = [parts[t] + parts[t + 1] for t in range(0, len(parts), 2)]
    return parts[0]


def _mixer_c_kernel(cq_ref, iq_ref, kk_ref, vt_ref, iwt_ref, bias_ref, o_ref,
                    score_ref, sel_ref, qst_ref, iqst_ref, acc_ref, *, topk, nb):
    i = pl.program_id(1)
    npair = (i + 2) // 2
    kl, ql = _tile_iotas()
    qpos = BLK * i - N_PAD + ql
    kf = jnp.float32(topk)

    qst_ref[...] = jnp.zeros_like(qst_ref)
    iqst_ref[...] = jnp.zeros_like(iqst_ref)
    for g in range(C_HEADS // 2):
        cols = slice(g * BLK, (g + 1) * BLK)
        qt = (cq_ref[:, cols].astype(F32) * QK_SCALE).T.astype(BF16)
        it = iq_ref[:, cols].astype(F32).T.astype(BF16)
        for hh in range(2):
            lanes = slice((2 * g + hh) * BLK, (2 * g + hh + 1) * BLK)
            qst_ref[0:HEAD_DIM, lanes] = qt[hh * HEAD_DIM:(hh + 1) * HEAD_DIM, :]
            iqst_ref[HEAD_DIM:BLK, lanes] = it[hh * IDX_DIM:(hh + 1) * IDX_DIM, :]
    iw = iwt_ref[...] * QK_SCALE
    iw_all = jnp.concatenate([iw[h:h + 1, :] for h in range(IDX_HEADS)], axis=1)

    def admissible(jl):
        kpos = BLK * jl - N_PAD + kl
        return (kpos >= 0) & (qpos >= kpos)

    def score_pair(p, carry):
        for jl, jk in _pair_tiles(p, nb):
            rows = pl.ds(pl.multiple_of(jk * BLK, BLK), BLK)
            kk = kk_ref[rows, :]
            sc = None
            for c in range(C_LANES // C_CHUNK):
                ln = slice(c * C_CHUNK, (c + 1) * C_CHUNK)
                rel = jnp.maximum(_mm(kk, iqst_ref[:, ln]), 0.0) * iw_all[:, ln]
                for h in range(C_CHUNK // BLK):
                    part = rel[:, h * BLK:(h + 1) * BLK]
                    sc = part if sc is None else sc + part
            score_ref[jl] = jnp.where(admissible(jl), sc * IDX_HEADS ** -0.5, NEG)
        return carry

    lax.fori_loop(0, npair, score_pair, 0)

    def count(pred):
        def body(p, c):
            xa = jnp.where(pred(score_ref[2 * p]), 1.0, 0.0)
            xb = jnp.where(pred(score_ref[2 * p + 1]), 1.0, 0.0)
            return c + (_fold8(xa) + _fold8(xb))
        return _colsum(lax.fori_loop(0, npair, body, jnp.zeros((8, BLK), F32)))

    def search(it, u):
        cand = u | lax.shift_left(jnp.int32(1), 31 - it)
        thr_c = _ordered_bits_to_float(cand)
        return jnp.where(count(lambda s: s >= thr_c) >= kf, cand, u)

    u = lax.fori_loop(0, 32, search, jnp.zeros((1, BLK), jnp.int32))
    found = (u < 0) | (u >= jnp.int32(0x00800000))
    thr = jnp.where(found, _ordered_bits_to_float(u), -jnp.inf)
    need = kf - count(lambda s: s > thr)
    n_ge = count(lambda s: s >= thr)
    has_tie = jnp.max(jnp.where((n_ge > kf) & (thr > NEG), 1.0, 0.0)) > 0.5

    def select_plain():
        def body(p, carry):
            for jl in (2 * p, 2 * p + 1):
                sel_ref[jl] = jnp.where(admissible(jl) & (score_ref[jl] >= thr), 0.0, NEG)
            return carry
        lax.fori_loop(0, npair, body, 0)

    def select_ties():
        tri = (kl >= ql).astype(BF16)

        def body(jl, seen):
            sc = score_ref[jl]
            eq = sc == thr
            eqf = jnp.where(eq, 1.0, 0.0)
            rank = seen + _mm(tri, eqf.astype(BF16))
            sel = admissible(jl) & ((sc > thr) | (eq & (rank <= need)))
            sel_ref[jl] = jnp.where(sel, 0.0, NEG)
            return seen + _colsum(eqf)
        lax.fori_loop(0, 2 * npair, body, jnp.zeros((1, BLK), F32))

    lax.cond(has_tie, select_ties, select_plain)

    acc_ref[...] = jnp.zeros_like(acc_ref)

    def attend_pair(p, carry):
        m_old, l_old = carry
        tiles = _pair_tiles(p, nb)
        ss = []
        for jl, jk in tiles:
            rows = pl.ds(pl.multiple_of(jk * BLK, BLK), BLK)
            ss.append(_mm(kk_ref[rows, :], qst_ref[...]) + bias_ref[jnp.clip(i - jl, 0, 2)]
                      + jnp.tile(sel_ref[jl], (1, C_HEADS)))
        m_new = jnp.maximum(m_old, jnp.maximum(_colmax(ss[0]), _colmax(ss[1])))
        alpha = jnp.exp(m_old - m_new)
        pa = jnp.exp(ss[0] - m_new)
        pb = jnp.exp(ss[1] - m_new)
        l_new = alpha * l_old + _colsum(pa) + _colsum(pb)
        acc_ref[...] = (alpha * acc_ref[...] + _mm(vt_ref[tiles[0][1]], pa.astype(BF16))
                        + _mm(vt_ref[tiles[1][1]], pb.astype(BF16)))
        return m_new, l_new

    init = (jnp.full((1, C_LANES), NEG, F32), jnp.zeros((1, C_LANES), F32))
    _, l_fin = lax.fori_loop(0, npair, attend_pair, init)
    o = acc_ref[...] / l_fin
    o_rows = jnp.concatenate([o[:, h * BLK:(h + 1) * BLK] for h in range(C_HEADS)], axis=0)
    o_ref[...] = o_rows.T.astype(BF16)


def _mixer_c(proj, cvt, iwt, bias, bsz, nb, topk):
    n = proj.shape[0]
    p_rows = nb * BLK
    w = C_HEADS * HEAD_DIM
    return pl.pallas_call(
        functools.partial(_mixer_c_kernel, topk=topk, nb=nb),
        grid=(bsz, nb),
        in_specs=[
            pl.BlockSpec((BLK, w), lambda b, i: (b * nb + i, COL_CQ // w)),
            pl.BlockSpec((BLK, w), lambda b, i: (b * nb + i, COL_IQ // w)),
            pl.BlockSpec((p_rows, BLK), lambda b, i: (b, COL_CKIK // BLK)),
            pl.BlockSpec((nb, HEAD_DIM, BLK), lambda b, i: (b, 0, 0)),
            pl.BlockSpec((None, IDX_HEADS, BLK), lambda b, i: (b * nb + i, 0, 0)),
            pl.BlockSpec((3, BLK, C_LANES), lambda b, i: (0, 0, 0)),
        ],
        out_specs=pl.BlockSpec((BLK, w), lambda b, i: (b * nb + i, 0)),
        out_shape=jax.ShapeDtypeStruct((n, w), BF16),
        scratch_shapes=[
            pltpu.VMEM((nb + 1, BLK, BLK), F32),
            pltpu.VMEM((nb + 1, BLK, BLK), F32),
            pltpu.VMEM((BLK, C_LANES), BF16),
            pltpu.VMEM((BLK, C_LANES), BF16),
            pltpu.VMEM((HEAD_DIM, C_LANES), F32),
        ],
        name="mixer_c",
        compiler_params=pltpu.CompilerParams(dimension_semantics=("parallel", "arbitrary")),
    )(proj, proj, proj, cvt, iwt, bias)


def _merge_kernel(h_ref, g_ref, oa_ref, ob_ref, oc_ref, ga_ref, gb_ref, gc_ref, wa_ref, wb_ref, wc_ref, wo_ref,
                  o_ref, xn_ref, acc_ref, *, nc):
    c = pl.program_id(1)

    @pl.when(c == 0)
    def _():
        xn_ref[...] = _rms(h_ref[...], g_ref[...]).astype(BF16)
        acc_ref[...] = jnp.zeros_like(acc_ref)

    xn = xn_ref[...]
    y = (jax.nn.sigmoid(_mm(xn, ga_ref[...])) * _mm(oa_ref[...], wa_ref[...])
         + jax.nn.sigmoid(_mm(xn, gb_ref[...])) * _mm(ob_ref[...], wb_ref[...])
         + jax.nn.sigmoid(_mm(xn, gc_ref[...])) * _mm(oc_ref[...], wc_ref[...]))
    acc_ref[...] += _mm(y.astype(BF16), wo_ref[...])

    @pl.when(c == nc - 1)
    def _():
        o_ref[...] = h_ref[...] + acc_ref[...]


def _merge(h, gain, oa, ob, oc, wgates, wbr, wout, layer):
    n = h.shape[0]
    tc = 512
    nc = D_MODEL // tc
    bw = 512
    row = lambda i, c: (i, 0)
    wgate = lambda br: pl.BlockSpec((None, D_MODEL, tc), lambda i, c: (layer, 0, br * nc + c))
    wbranch = lambda br: pl.BlockSpec((None, None, bw, tc), lambda i, c: (layer, br, 0, c))
    return pl.pallas_call(
        functools.partial(_merge_kernel, nc=nc),
        grid=(n // TM, nc),
        in_specs=[
            pl.BlockSpec((TM, D_MODEL), row),
            pl.BlockSpec((None, 1, D_MODEL), lambda i, c: (layer, 0, 0)),
            pl.BlockSpec((TM, bw), row), pl.BlockSpec((TM, bw), row), pl.BlockSpec((TM, bw), row),
            wgate(0), wgate(1), wgate(2),
            wbranch(0), wbranch(1), wbranch(2),
            pl.BlockSpec((None, tc, D_MODEL), lambda i, c: (layer, c, 0)),
        ],
        out_specs=pl.BlockSpec((TM, D_MODEL), row),
        out_shape=jax.ShapeDtypeStruct((n, D_MODEL), F32),
        scratch_shapes=[pltpu.VMEM((TM, D_MODEL), BF16), pltpu.VMEM((TM, D_MODEL), F32)],
        input_output_aliases={0: 0},
        name="branch_merge",
        compiler_params=pltpu.CompilerParams(
            dimension_semantics=("parallel", "arbitrary"), vmem_limit_bytes=VMEM_LIMIT),
    )(h, gain, oa, ob, oc, wgates, wgates, wgates, wbr, wbr, wbr, wout)


def _final_kernel(h_ref, g_ref, o_ref):
    o_ref[...] = _rms(h_ref[...], g_ref[...])


def _final_norm(h, gain, bsz, nb):
    seq = (nb - 1) * BLK
    tr = 512 if seq % 512 == 0 else BLK
    return pl.pallas_call(
        _final_kernel,
        grid=(bsz, seq // tr),
        in_specs=[
            pl.BlockSpec((pl.Element(tr), pl.Element(D_MODEL)),
                         lambda b, r: (pl.multiple_of((b * nb + 1) * BLK + r * tr, BLK), 0)),
            pl.BlockSpec((1, D_MODEL), lambda b, r: (0, 0)),
        ],
        out_specs=pl.BlockSpec((None, tr, D_MODEL), lambda b, r: (b, r, 0)),
        out_shape=jax.ShapeDtypeStruct((bsz, seq, D_MODEL), F32),
        name="final_norm",
        compiler_params=pltpu.CompilerParams(dimension_semantics=("parallel", "parallel")),
    )(h, gain)


def _bucket_np(d):
    max_exact = N_BUCKETS // 2
    d = np.maximum(d, 0)
    df = np.maximum(d, 1).astype(np.float32)
    large = max_exact + (np.log(df / max_exact) / math.log(MAX_DISTANCE / max_exact)
                         * (N_BUCKETS - max_exact)).astype(np.int32)
    return np.where(d < max_exact, d, np.minimum(large, N_BUCKETS - 1)).astype(np.int32)


def _bias_tiles(table):
    kl = np.arange(BLK)[:, None]
    ql = np.arange(BLK)[None, :]
    idx = np.stack([_bucket_np(ql - kl), _bucket_np(BLK + ql - kl),
                    np.full((BLK, BLK), N_BUCKETS - 1, np.int32)])
    onehot = (idx.reshape(-1, 1) == np.arange(N_BUCKETS)[None, :]).astype(np.float32)
    tiles = jnp.dot(onehot, table.astype(F32), precision=lax.Precision.HIGHEST)
    return jnp.transpose(tiles.reshape(3, BLK, BLK, -1), (3, 0, 1, 2))


def _pack_w_in(w_in):
    o = np.cumsum([0, 512, 128, 128, 256, 256, 256, 256, 512, 512, 64, 64, 512, 64, 8, 3 * D_MODEL])
    w_in = w_in.astype(BF16)
    (aq, ak, av, bq1, bq2, bk1, bk2, bv, cq, ck, cv, iq, ik, iw, gates) = [
        w_in[:, :, o[t]:o[t + 1]] for t in range(15)]
    pad = jnp.zeros(w_in.shape[:2] + (N_PROJ - COL_CKIK - BLK,), BF16)
    w = jnp.concatenate([aq, cq, iq, bq1, bq2, bk1, bk2, ak, ck, ik, pad], axis=-1)
    padt = jnp.zeros(w_in.shape[:2] + (N_PROJ_T - ROW_IW - IDX_HEADS,), BF16)
    wt = jnp.swapaxes(jnp.concatenate([av, bv, cv, iw, padt], axis=-1), 1, 2)
    return w, wt, gates


def kernel(x, meta_tokens, rel_bias_table, norm_ffn1, ffn1_w_gate, ffn1_w_up, ffn1_w_down, norm_mix, w_in,
           attn_sinks, diff_lambda, diff_subln, w_branch, w_out, norm_ffn2, ffn2_w_gate, ffn2_w_up,
           ffn2_w_down, norm_final):
    bsz, seq, _ = x.shape
    nb = seq // BLK + 1
    p_rows = nb * BLK
    n = bsz * p_rows
    assert seq % BLK == 0 and n % TM_FFN == 0
    topk = min(TOPK_MAX, seq // 4)

    lead = jnp.concatenate([jnp.zeros((N_PAD, D_MODEL), x.dtype), meta_tokens.astype(x.dtype)], axis=0)

    bias = _bias_tiles(rel_bias_table)
    bias_a = jnp.transpose(bias[:A_HEADS].reshape(A_KV_HEADS, A_GRP, 3, BLK, BLK),
                           (0, 2, 3, 1, 4)).reshape(A_KV_HEADS, 3, BLK, A_LANES)
    bias_b = jnp.transpose(bias[A_HEADS:A_HEADS + B_HEADS].reshape(B_GROUPS, 2, 3, BLK, BLK),
                           (0, 2, 3, 1, 4)).reshape(B_GROUPS, 3, BLK, 2 * BLK)
    bias_c = jnp.transpose(bias[A_HEADS + B_HEADS:], (1, 2, 0, 3)).reshape(3, BLK, C_LANES)
    w_proj, w_proj_t, w_gates = _pack_w_in(w_in)
    g1, gm, g2 = (t.reshape(DEPTH, 1, D_MODEL) for t in (norm_ffn1, norm_mix, norm_ffn2))
    f1 = [t.astype(BF16) for t in (ffn1_w_gate, ffn1_w_up, ffn1_w_down)]
    f2 = [t.astype(BF16) for t in (ffn2_w_gate, ffn2_w_up, ffn2_w_down)]
    wbr, wout = w_branch.astype(BF16), w_out.astype(BF16)
    sinks = jnp.broadcast_to(attn_sinks.astype(F32)[:, :, None],
                             (DEPTH, A_HEADS, BLK)).reshape(DEPTH, A_KV_HEADS, 1, A_LANES)
    subln = jnp.broadcast_to(diff_subln.astype(F32)[:, :, None], (DEPTH, B_V_DIM, BLK))
    lam_f = diff_lambda.astype(F32)

    for l in range(DEPTH):
        h = _ffn_first(x, lead, g1, *f1, nb) if l == 0 else _ffn(h, g1, *f1, l)
        proj, avt, bvt, cvt, iwt = _proj(h, gm, w_proj, w_proj_t, l)
        o_a = _mixer_a(proj, avt, bias_a, sinks[l], bsz, nb)
        lam_init = 0.8 - 0.6 * math.exp(-0.3 * l)
        o_b = _mixer_b(proj, bvt, bias_b, lam_f[l:l + 1], subln[l:l + 1], bsz, nb, lam_init)
        o_c = _mixer_c(proj, cvt, iwt, bias_c, bsz, nb, topk)
        h = _merge(h, gm, o_a, o_b, o_c, w_gates, wbr, wout, l)
        h = _ffn(h, g2, *f2, l)

    return _final_norm(h, norm_final.reshape(1, D_MODEL), bsz, nb)
```

```python
import functools
import math

import jax
import jax.numpy as jnp
import numpy as np
from jax import lax
from jax.experimental import pallas as pl
from jax.experimental.pallas import tpu as pltpu

F32 = jnp.float32
BF16 = jnp.bfloat16

D_MODEL = 2048
D_FF = 5632
DEPTH = 2
BLK = 128
N_META = 16
N_PAD = BLK - N_META
WINDOW = 128
HEAD_DIM = 64
A_HEADS, A_KV_HEADS = 8, 2
B_HEADS, B_QK_DIM, B_V_DIM = 4, 64, 128
C_HEADS = 8
IDX_HEADS, IDX_DIM = 8, 64
TOPK_MAX = 256
N_BUCKETS, MAX_DISTANCE = 32, 128
EPS = 1e-6
NEG = -1e30

COL_AQ, COL_CQ, COL_IQ = 0, 512, 1024
COL_BQ1, COL_BQ2, COL_BK1, COL_BK2 = 1536, 1792, 2048, 2304
COL_AK, COL_CKIK = 2560, 2688
N_PROJ = 3072
ROW_AV, ROW_BV, ROW_CV, ROW_IW, N_PROJ_T = 0, 128, 640, 704, 720

V_ONES = 16
A_VROWS, B_VROWS, C_VROWS = HEAD_DIM + V_ONES, B_V_DIM + V_ONES, HEAD_DIM + V_ONES

TM = 512
TM_FFN = 1024
TM_PROJ = 1024
TF = 512
FFN_VMEM_LIMIT = 62 * 1024 * 1024
TN = 1024
assert HEAD_DIM == B_QK_DIM == IDX_DIM == 64
QK_SCALE = 0.125
VMEM_LIMIT = 56 * 1024 * 1024


def _nt(a, b):
    return lax.dot_general(a, b, (((1,), (1,)), ((), ())), preferred_element_type=F32)


def _mm(a, b):
    return jnp.dot(a, b, preferred_element_type=F32)


def _rms(x, g):
    ms = jnp.mean(x * x, axis=-1, keepdims=True)
    return x * lax.rsqrt(ms + EPS) * g


def _ffn_accumulate(xn_ref, wg_ref, wu_ref, wd_ref, o_ref):
    for r in range(TM_FFN // TM):
        rows = slice(r * TM, (r + 1) * TM)
        xn = xn_ref[rows, :]
        a = _mm(xn, wg_ref[...])
        b = _mm(xn, wu_ref[...])
        t = (a * jax.nn.sigmoid(a)) * b
        o_ref[rows, :] += 0.5 * _mm(t.astype(BF16), wd_ref[...])


def _ffn_kernel(h_ref, g_ref, wg_ref, wu_ref, wd_ref, o_ref, xn_ref):
    @pl.when(pl.program_id(1) == 0)
    def _():
        h = h_ref[...]
        xn_ref[...] = _rms(h, g_ref[...]).astype(BF16)
        o_ref[...] = h

    _ffn_accumulate(xn_ref, wg_ref, wu_ref, wd_ref, o_ref)


def _ffn_first_kernel(*refs, nb):
    nblk = TM_FFN // BLK
    x_refs = refs[:nblk]
    lead_ref, g_ref, wg_ref, wu_ref, wd_ref, o_ref, xn_ref = refs[nblk:]

    @pl.when(pl.program_id(1) == 0)
    def _():
        first = pl.program_id(0) * nblk
        for t in range(nblk):
            rows = slice(t * BLK, (t + 1) * BLK)
            o_ref[rows, :] = x_refs[t][...]

            @pl.when(lax.rem(first + t, nb) == 0)
            def _():
                o_ref[rows, :] = lead_ref[...]

        xn_ref[...] = _rms(o_ref[...], g_ref[...]).astype(BF16)

    _ffn_accumulate(xn_ref, wg_ref, wu_ref, wd_ref, o_ref)


def _ffn_weight_specs(layer):
    return [
        pl.BlockSpec((None, 1, D_MODEL), lambda i, k: (layer, 0, 0)),
        pl.BlockSpec((None, D_MODEL, TF), lambda i, k: (layer, 0, k)),
        pl.BlockSpec((None, D_MODEL, TF), lambda i, k: (layer, 0, k)),
        pl.BlockSpec((None, TF, D_MODEL), lambda i, k: (layer, k, 0)),
    ]


def _ffn(h, gain, wg, wu, wd, layer):
    n = h.shape[0]
    return pl.pallas_call(
        _ffn_kernel,
        grid=(n // TM_FFN, D_FF // TF),
        in_specs=[pl.BlockSpec((TM_FFN, D_MODEL), lambda i, k: (i, 0))] + _ffn_weight_specs(layer),
        out_specs=pl.BlockSpec((TM_FFN, D_MODEL), lambda i, k: (i, 0)),
        out_shape=jax.ShapeDtypeStruct((n, D_MODEL), F32),
        scratch_shapes=[pltpu.VMEM((TM_FFN, D_MODEL), BF16)],
        input_output_aliases={0: 0},
        name="ffn",
        compiler_params=pltpu.CompilerParams(
            dimension_semantics=("parallel", "arbitrary"), vmem_limit_bytes=FFN_VMEM_LIMIT),
    )(h, gain, wg, wu, wd)


def _ffn_first(x, lead, gain, wg, wu, wd, nb):
    bsz = x.shape[0]
    n = bsz * nb * BLK
    nblk = TM_FFN // BLK

    def x_spec(t):
        return pl.BlockSpec((None, BLK, D_MODEL),
                            lambda i, k: ((i * nblk + t) // nb, jnp.maximum((i * nblk + t) % nb - 1, 0), 0))

    return pl.pallas_call(
        functools.partial(_ffn_first_kernel, nb=nb),
        grid=(n // TM_FFN, D_FF // TF),
        in_specs=[x_spec(t) for t in range(nblk)]
        + [pl.BlockSpec((BLK, D_MODEL), lambda i, k: (0, 0))] + _ffn_weight_specs(0),
        out_specs=pl.BlockSpec((TM_FFN, D_MODEL), lambda i, k: (i, 0)),
        out_shape=jax.ShapeDtypeStruct((n, D_MODEL), F32),
        scratch_shapes=[pltpu.VMEM((TM_FFN, D_MODEL), BF16)],
        name="ffn_first",
        compiler_params=pltpu.CompilerParams(
            dimension_semantics=("parallel", "arbitrary"), vmem_limit_bytes=FFN_VMEM_LIMIT),
    )(*([x] * nblk), lead, gain, wg, wu, wd)


def _proj_kernel(h_ref, g_ref, w_ref, wt_ref, o_ref, avt_ref, bvt_ref, cvt_ref, iwt_ref, xn_ref):
    j = pl.program_id(1)

    @pl.when(j == 0)
    def _():
        xn = _rms(h_ref[...], g_ref[...]).astype(BF16)
        xn_ref[...] = xn
        vt = _nt(wt_ref[...], xn)
        for t in range(TM_PROJ // BLK):
            cols = slice(t * BLK, (t + 1) * BLK)
            ones = jnp.ones((V_ONES, BLK), BF16)
            for g in range(A_KV_HEADS):
                avt_ref[t, g * A_VROWS:g * A_VROWS + HEAD_DIM] = (
                    vt[ROW_AV + g * HEAD_DIM:ROW_AV + (g + 1) * HEAD_DIM, cols].astype(BF16))
                avt_ref[t, g * A_VROWS + HEAD_DIM:(g + 1) * A_VROWS] = ones
            for h in range(B_HEADS):
                bvt_ref[t, h * B_VROWS:h * B_VROWS + B_V_DIM] = (
                    vt[ROW_BV + h * B_V_DIM:ROW_BV + (h + 1) * B_V_DIM, cols].astype(BF16))
                bvt_ref[t, h * B_VROWS + B_V_DIM:(h + 1) * B_VROWS] = ones
            cvt_ref[t, 0:HEAD_DIM] = vt[ROW_CV:ROW_IW, cols].astype(BF16)
            cvt_ref[t, HEAD_DIM:C_VROWS] = ones
            iwt_ref[t] = vt[ROW_IW:ROW_IW + IDX_HEADS, cols]

    o_ref[...] = _mm(xn_ref[...], w_ref[...]).astype(BF16)


def _proj(h, gain, w, wt, layer):
    n = h.shape[0]
    nt = n // BLK
    tpb = TM_PROJ // BLK
    return pl.pallas_call(
        _proj_kernel,
        grid=(n // TM_PROJ, N_PROJ // TN),
        in_specs=[
            pl.BlockSpec((TM_PROJ, D_MODEL), lambda i, j: (i, 0)),
            pl.BlockSpec((None, 1, D_MODEL), lambda i, j: (layer, 0, 0)),
            pl.BlockSpec((None, D_MODEL, TN), lambda i, j: (layer, 0, j)),
            pl.BlockSpec((None, N_PROJ_T, D_MODEL), lambda i, j: (layer, 0, 0)),
        ],
        out_specs=[
            pl.BlockSpec((TM_PROJ, TN), lambda i, j: (i, j)),
            pl.BlockSpec((tpb, A_KV_HEADS * A_VROWS, BLK), lambda i, j: (i, 0, 0)),
            pl.BlockSpec((tpb, B_HEADS * B_VROWS, BLK), lambda i, j: (i, 0, 0)),
            pl.BlockSpec((tpb, C_VROWS, BLK), lambda i, j: (i, 0, 0)),
            pl.BlockSpec((tpb, IDX_HEADS, BLK), lambda i, j: (i, 0, 0)),
        ],
        out_shape=[
            jax.ShapeDtypeStruct((n, N_PROJ), BF16),
            jax.ShapeDtypeStruct((nt, A_KV_HEADS * A_VROWS, BLK), BF16),
            jax.ShapeDtypeStruct((nt, B_HEADS * B_VROWS, BLK), BF16),
            jax.ShapeDtypeStruct((nt, C_VROWS, BLK), BF16),
            jax.ShapeDtypeStruct((nt, IDX_HEADS, BLK), F32),
        ],
        scratch_shapes=[pltpu.VMEM((TM_PROJ, D_MODEL), BF16)],
        name="mixer_proj",
        compiler_params=pltpu.CompilerParams(
            dimension_semantics=("parallel", "arbitrary"), vmem_limit_bytes=VMEM_LIMIT),
    )(h, gain, w, wt)


def _tile_iotas():
    kl = lax.broadcasted_iota(jnp.int32, (BLK, BLK), 0)
    ql = lax.broadcasted_iota(jnp.int32, (BLK, BLK), 1)
    return kl, ql


def _colmax(x):
    return jnp.max(x, axis=0, keepdims=True)


def _colsum(x):
    return jnp.sum(x, axis=0, keepdims=True)


def _exp_bf16(x):
    return jnp.exp(x.astype(BF16))


def _pair_tiles(p, nb):
    ja = 2 * p
    return ((ja, ja), (ja + 1, jnp.minimum(ja + 1, nb - 1)))


A_GRP = A_HEADS // A_KV_HEADS
A_LANES = A_GRP * BLK


def _mixer_a_kernel(q_ref, k0_ref, k1_ref, k2_ref, v0_ref, v1_ref, v2_ref, bias_ref, sink_ref, o_ref, qst_ref):
    i = pl.program_id(1)
    kl = lax.broadcasted_iota(jnp.int32, (BLK, A_LANES), 0)
    ql = lax.broadcasted_iota(jnp.int32, (BLK, A_LANES), 1) & (BLK - 1)
    qpos = BLK * i - N_PAD + ql
    ok_meta = (kl >= N_PAD) & (qpos >= kl - N_PAD)
    ok_prev = (BLK * (i - 1) - N_PAD + kl >= N_META) & (ql < kl)
    ok_cur = (BLK * i - N_PAD + kl >= N_META) & (ql >= kl)
    t_meta = jnp.minimum(i, 2)
    qst_ref[...] = jnp.zeros_like(qst_ref)
    for h in range(A_HEADS):
        g, r = divmod(h, A_GRP)
        qst_ref[g, r * BLK:(r + 1) * BLK, g * HEAD_DIM:(g + 1) * HEAD_DIM] = (
            q_ref[:, h * HEAD_DIM:(h + 1) * HEAD_DIM] * QK_SCALE)
    outs = []
    for g in range(A_KV_HEADS):
        qs = qst_ref[g]
        s0 = jnp.where(ok_meta, _nt(k0_ref[...], qs) + bias_ref[g, t_meta], NEG)
        s1 = jnp.where(ok_prev, _nt(k1_ref[...], qs) + bias_ref[g, 1], NEG)
        s2 = jnp.where(ok_cur, _nt(k2_ref[...], qs) + bias_ref[g, 0], NEG)
        sink = sink_ref[g]
        m = jnp.maximum(jnp.maximum(_colmax(s0), _colmax(s1)), jnp.maximum(_colmax(s2), sink))
        vs = slice(g * A_VROWS, (g + 1) * A_VROWS)
        ot = (_mm(v0_ref[vs, :], _exp_bf16(s0 - m)) + _mm(v1_ref[vs, :], _exp_bf16(s1 - m))
              + _mm(v2_ref[vs, :], _exp_bf16(s2 - m)))
        ot = ot[:HEAD_DIM] / (ot[HEAD_DIM:HEAD_DIM + 1] + jnp.exp(sink - m))
        outs.extend(ot[:, r * BLK:(r + 1) * BLK] for r in range(A_GRP))
    o_ref[...] = jnp.concatenate(outs, axis=0).T.astype(BF16)


def _mixer_a(proj, avt, bias, sinks, bsz, nb):
    n = proj.shape[0]
    kcol = COL_AK // BLK
    rowq = lambda b, i: (b * nb + i, 0)
    spec_k = lambda f: pl.BlockSpec((BLK, BLK), lambda b, i: (b * nb + f(i), kcol))
    spec_v = lambda f: pl.BlockSpec((None, A_KV_HEADS * A_VROWS, BLK), lambda b, i: (b * nb + f(i), 0, 0))
    first = lambda i: 0
    prev = lambda i: jnp.maximum(i - 1, 0)
    cur = lambda i: i
    return pl.pallas_call(
        _mixer_a_kernel,
        grid=(bsz, nb),
        in_specs=[
            pl.BlockSpec((BLK, A_HEADS * HEAD_DIM), rowq),
            spec_k(first), spec_k(prev), spec_k(cur),
            spec_v(first), spec_v(prev), spec_v(cur),
            pl.BlockSpec((A_KV_HEADS, 3, BLK, A_LANES), lambda b, i: (0, 0, 0, 0)),
            pl.BlockSpec((A_KV_HEADS, 1, A_LANES), lambda b, i: (0, 0, 0)),
        ],
        out_specs=pl.BlockSpec((BLK, A_HEADS * HEAD_DIM), rowq),
        out_shape=jax.ShapeDtypeStruct((n, A_HEADS * HEAD_DIM), BF16),
        scratch_shapes=[pltpu.VMEM((A_KV_HEADS, A_LANES, BLK), BF16)],
        name="mixer_a",
        compiler_params=pltpu.CompilerParams(dimension_semantics=("parallel", "parallel")),
    )(proj, proj, proj, proj, avt, avt, avt, bias, sinks)


B_GROUPS = B_HEADS // 2


def _mixer_b_kernel(q1_ref, q2_ref, k1_ref, k2_ref, vt_ref, bias_ref, lam_ref, subln_ref, o_ref,
                    qbd_ref, acc_ref, *, lam_init, nb):
    i = pl.program_id(1)
    npair = (i + 2) // 2
    w2 = 2 * BLK
    kl = lax.broadcasted_iota(jnp.int32, (BLK, w2), 0)
    qpos = BLK * i - N_PAD + (lax.broadcasted_iota(jnp.int32, (BLK, w2), 1) & (BLK - 1))

    feat = lax.broadcasted_iota(jnp.int32, (BLK, BLK), 0)
    for mp, q_ref in enumerate((q1_ref, q2_ref)):
        for g in range(B_GROUPS):
            qt = (q_ref[:, g * BLK:(g + 1) * BLK].astype(F32) * QK_SCALE).T
            qbd_ref[mp, g] = jnp.concatenate(
                [jnp.where(feat < B_QK_DIM, qt, 0.0), jnp.where(feat >= B_QK_DIM, qt, 0.0)], axis=1).astype(BF16)
    acc_ref[...] = jnp.zeros_like(acc_ref)

    def pair(p, ms):
        tiles = _pair_tiles(p, nb)
        oks, ts, rows = [], [], []
        for jl, jk in tiles:
            kpos = BLK * jl - N_PAD + kl
            oks.append((kpos >= 0) & (qpos >= kpos))
            ts.append(jnp.clip(i - jl, 0, 2))
            rows.append(pl.ds(pl.multiple_of(jk * BLK, BLK), BLK))
        raw = [[[_mm(k_ref[rows[tl], g * BLK:(g + 1) * BLK], qbd_ref[mp, g]) for tl in range(2)]
                for g in range(B_GROUPS)] for mp, k_ref in enumerate((k1_ref, k2_ref))]
        new_ms = []
        ps = [[None] * B_GROUPS for _ in range(2)]
        alphas = [[None] * B_GROUPS for _ in range(2)]
        for mp in range(2):
            row_m = []
            for g in range(B_GROUPS):
                ss = [jnp.where(oks[tl], raw[mp][g][tl] + bias_ref[g, ts[tl]], NEG) for tl in range(2)]
                m_old = ms[mp][g]
                m_new = jnp.maximum(m_old, jnp.maximum(_colmax(ss[0]), _colmax(ss[1])))
                row_m.append(m_new)
                ps[mp][g] = (_exp_bf16(ss[0] - m_new), _exp_bf16(ss[1] - m_new))
                alphas[mp][g] = jnp.exp(m_old - m_new)
            new_ms.append(row_m)
        for h in range(B_HEADS):
            g, hs = h // 2, slice((h % 2) * BLK, (h % 2 + 1) * BLK)
            alpha = jnp.concatenate([alphas[0][g][:, hs], alphas[1][g][:, hs]], axis=1)
            upd = alpha * acc_ref[h]
            for tl, (jl, jk) in enumerate(tiles):
                p12 = jnp.concatenate([ps[0][g][tl][:, hs], ps[1][g][tl][:, hs]], axis=1)
                upd = upd + _mm(vt_ref[jk, h * B_VROWS:(h + 1) * B_VROWS, :], p12)
            acc_ref[h] = upd
        return new_ms

    lax.fori_loop(0, npair, pair, [[jnp.full((1, w2), NEG, F32) for _ in range(B_GROUPS)] for _ in range(2)])

    lf = lam_ref[...]
    lam = (jnp.exp(jnp.sum(lf[0:1] * lf[1:2], axis=-1, keepdims=True))
           - jnp.exp(jnp.sum(lf[2:3] * lf[3:4], axis=-1, keepdims=True)) + lam_init)
    outs = []
    for h in range(B_HEADS):
        acc = acc_ref[h]
        num, den = acc[:B_V_DIM], acc[B_V_DIM:B_V_DIM + 1]
        o = num[:, :BLK] / den[:, :BLK] - lam * (num[:, BLK:] / den[:, BLK:])
        ms_ = jnp.mean(o * o, axis=0, keepdims=True)
        outs.append(o * lax.rsqrt(ms_ + EPS) * subln_ref[...] * (1.0 - lam_init))
    o_ref[...] = jnp.concatenate(outs, axis=0).T.astype(BF16)


def _mixer_b(proj, bvt, bias, lam, subln, bsz, nb, lam_init):
    n = proj.shape[0]
    p_rows = nb * BLK
    w = B_HEADS * B_QK_DIM
    rowq = lambda c: (lambda b, i: (b * nb + i, c))
    return pl.pallas_call(
        functools.partial(_mixer_b_kernel, lam_init=lam_init, nb=nb),
        grid=(bsz, nb),
        in_specs=[
            pl.BlockSpec((BLK, w), rowq(COL_BQ1 // w)),
            pl.BlockSpec((BLK, w), rowq(COL_BQ2 // w)),
            pl.BlockSpec((p_rows, w), lambda b, i: (b, COL_BK1 // w)),
            pl.BlockSpec((p_rows, w), lambda b, i: (b, COL_BK2 // w)),
            pl.BlockSpec((nb, B_HEADS * B_VROWS, BLK), lambda b, i: (b, 0, 0)),
            pl.BlockSpec((B_GROUPS, 3, BLK, 2 * BLK), lambda b, i: (0, 0, 0, 0)),
            pl.BlockSpec((None, 4, B_QK_DIM), lambda b, i: (0, 0, 0)),
            pl.BlockSpec((None, B_V_DIM, BLK), lambda b, i: (0, 0, 0)),
        ],
        out_specs=pl.BlockSpec((BLK, B_HEADS * B_V_DIM), lambda b, i: (b * nb + i, 0)),
        out_shape=jax.ShapeDtypeStruct((n, B_HEADS * B_V_DIM), BF16),
        scratch_shapes=[
            pltpu.VMEM((2, B_GROUPS, BLK, 2 * BLK), BF16),
            pltpu.VMEM((B_HEADS, B_VROWS, 2 * BLK), F32),
        ],
        name="mixer_b",
        compiler_params=pltpu.CompilerParams(dimension_semantics=("parallel", "arbitrary")),
    )(proj, proj, proj, proj, bvt, bias, lam, subln)


C_LANES = C_HEADS * BLK
C_CHUNK = 2 * BLK


def _ordered_bits_to_float(u):
    bits = jnp.where(u < 0, u ^ jnp.int32(-2 ** 31), ~u)
    return lax.bitcast_convert_type(bits, F32)


def _fold8(x):
    parts = [x[r:r + 8, :] for r in range(0, BLK, 8)]
    while len(parts) > 1:
        parts = [parts[t] + parts[t + 1] for t in range(0, len(parts), 2)]
    return parts[0]


def _mixer_c_kernel(cq_ref, iq_ref, kk_ref, vt_ref, iwt_ref, bias_ref, o_ref,
                    score_ref, sel_ref, qst_ref, iqst_ref, acc_ref, *, topk, nb):
    i = pl.program_id(1)
    npair = (i + 2) // 2
    kl, ql = _tile_iotas()
    qpos = BLK * i - N_PAD + ql
    kf = jnp.float32(topk)

    qst_ref[...] = jnp.zeros_like(qst_ref)
    iqst_ref[...] = jnp.zeros_like(iqst_ref)
    for g in range(C_HEADS // 2):
        cols = slice(g * BLK, (g + 1) * BLK)
        qt = (cq_ref[:, cols].astype(F32) * QK_SCALE).T.astype(BF16)
        it = iq_ref[:, cols].astype(F32).T.astype(BF16)
        for hh in range(2):
            lanes = slice((2 * g + hh) * BLK, (2 * g + hh + 1) * BLK)
            qst_ref[0:HEAD_DIM, lanes] = qt[hh * HEAD_DIM:(hh + 1) * HEAD_DIM, :]
            iqst_ref[HEAD_DIM:BLK, lanes] = it[hh * IDX_DIM:(hh + 1) * IDX_DIM, :]
    iw = iwt_ref[...] * QK_SCALE
    iw_all = jnp.concatenate([iw[h:h + 1, :] for h in range(IDX_HEADS)], axis=1)

    def admissible(jl):
        kpos = BLK * jl - N_PAD + kl
        return (kpos >= 0) & (qpos >= kpos)

    def score_pair(p, carry):
        for jl, jk in _pair_tiles(p, nb):
            rows = pl.ds(pl.multiple_of(jk * BLK, BLK), BLK)
            kk = kk_ref[rows, :]
            sc = None
            for c in range(C_LANES // C_CHUNK):
                ln = slice(c * C_CHUNK, (c + 1) * C_CHUNK)
                rel = jnp.maximum(_mm(kk, iqst_ref[:, ln]), 0.0) * iw_all[:, ln]
                for h in range(C_CHUNK // BLK):
                    part = rel[:, h * BLK:(h + 1) * BLK]
                    sc = part if sc is None else sc + part
            score_ref[jl] = jnp.where(admissible(jl), sc * IDX_HEADS ** -0.5, NEG)
        return carry

    lax.fori_loop(0, npair, score_pair, 0)

    def count(pred):
        def body(p, cs):
            xa = jnp.where(pred(score_ref[2 * p]), 1.0, 0.0)
            xb = jnp.where(pred(score_ref[2 * p + 1]), 1.0, 0.0)
            return [c + (xa[r * 8:(r + 1) * 8, :] + xb[r * 8:(r + 1) * 8, :]) for r, c in enumerate(cs)]
        parts = lax.fori_loop(0, npair, body, [jnp.zeros((8, BLK), F32)] * (BLK // 8))
        while len(parts) > 1:
            parts = [parts[t] + parts[t + 1] for t in range(0, len(parts), 2)]
        return _colsum(parts[0])

    def search(it, u):
        cand = u | lax.shift_left(jnp.int32(1), 31 - it)
        thr_c = _ordered_bits_to_float(cand)
        return jnp.where(count(lambda s: s >= thr_c) >= kf, cand, u)

    u = lax.fori_loop(0, 32, search, jnp.zeros((1, BLK), jnp.int32))
    found = (u < 0) | (u >= jnp.int32(0x00800000))
    thr = jnp.where(found, _ordered_bits_to_float(u), -jnp.inf)
    need = kf - count(lambda s: s > thr)
    n_ge = count(lambda s: s >= thr)
    has_tie = jnp.max(jnp.where((n_ge > kf) & (thr > NEG), 1.0, 0.0)) > 0.5

    def select_plain():
        def body(p, carry):
            for jl in (2 * p, 2 * p + 1):
                sel_ref[jl] = jnp.where(admissible(jl) & (score_ref[jl] >= thr), 0.0, NEG)
            return carry
        lax.fori_loop(0, npair, body, 0)

    def select_ties():
        tri = (kl >= ql).astype(BF16)

        def body(jl, seen):
            sc = score_ref[jl]
            eq = sc == thr
            eqf = jnp.where(eq, 1.0, 0.0)
            rank = seen + _mm(tri, eqf.astype(BF16))
            sel = admissible(jl) & ((sc > thr) | (eq & (rank <= need)))
            sel_ref[jl] = jnp.where(sel, 0.0, NEG)
            return seen + _colsum(eqf)
        lax.fori_loop(0, 2 * npair, body, jnp.zeros((1, BLK), F32))

    lax.cond(has_tie, select_ties, select_plain)

    acc_ref[...] = jnp.zeros_like(acc_ref)

    def attend_pair(p, m_old):
        tiles = _pair_tiles(p, nb)
        ss = []
        for jl, jk in tiles:
            rows = pl.ds(pl.multiple_of(jk * BLK, BLK), BLK)
            ss.append(_mm(kk_ref[rows, :], qst_ref[...]) + bias_ref[jnp.clip(i - jl, 0, 2)]
                      + jnp.tile(sel_ref[jl], (1, C_HEADS)))
        m_new = jnp.maximum(m_old, jnp.maximum(_colmax(ss[0]), _colmax(ss[1])))
        acc_ref[...] = (jnp.exp(m_old - m_new) * acc_ref[...]
                        + _mm(vt_ref[tiles[0][1]], _exp_bf16(ss[0] - m_new))
                        + _mm(vt_ref[tiles[1][1]], _exp_bf16(ss[1] - m_new)))
        return m_new

    lax.fori_loop(0, npair, attend_pair, jnp.full((1, C_LANES), NEG, F32))
    o = acc_ref[0:HEAD_DIM, :] / acc_ref[HEAD_DIM:HEAD_DIM + 1, :]
    o_rows = jnp.concatenate([o[:, h * BLK:(h + 1) * BLK] for h in range(C_HEADS)], axis=0)
    o_ref[...] = o_rows.T.astype(BF16)


def _mixer_c(proj, cvt, iwt, bias, bsz, nb, topk):
    n = proj.shape[0]
    p_rows = nb * BLK
    w = C_HEADS * HEAD_DIM
    return pl.pallas_call(
        functools.partial(_mixer_c_kernel, topk=topk, nb=nb),
        grid=(bsz, nb),
        in_specs=[
            pl.BlockSpec((BLK, w), lambda b, i: (b * nb + i, COL_CQ // w)),
            pl.BlockSpec((BLK, w), lambda b, i: (b * nb + i, COL_IQ // w)),
            pl.BlockSpec((p_rows, BLK), lambda b, i: (b, COL_CKIK // BLK)),
            pl.BlockSpec((nb, C_VROWS, BLK), lambda b, i: (b, 0, 0)),
            pl.BlockSpec((None, IDX_HEADS, BLK), lambda b, i: (b * nb + i, 0, 0)),
            pl.BlockSpec((3, BLK, C_LANES), lambda b, i: (0, 0, 0)),
        ],
        out_specs=pl.BlockSpec((BLK, w), lambda b, i: (b * nb + i, 0)),
        out_shape=jax.ShapeDtypeStruct((n, w), BF16),
        scratch_shapes=[
            pltpu.VMEM((nb + 1, BLK, BLK), F32),
            pltpu.VMEM((nb + 1, BLK, BLK), F32),
            pltpu.VMEM((BLK, C_LANES), BF16),
            pltpu.VMEM((BLK, C_LANES), BF16),
            pltpu.VMEM((C_VROWS, C_LANES), F32),
        ],
        name="mixer_c",
        compiler_params=pltpu.CompilerParams(dimension_semantics=("parallel", "arbitrary")),
    )(proj, proj, proj, cvt, iwt, bias)


def _merge_kernel(h_ref, g_ref, oa_ref, ob_ref, oc_ref, ga_ref, gb_ref, gc_ref, wa_ref, wb_ref, wc_ref, wo_ref,
                  o_ref, xn_ref, acc_ref, *, nc):
    c = pl.program_id(1)

    @pl.when(c == 0)
    def _():
        xn_ref[...] = _rms(h_ref[...], g_ref[...]).astype(BF16)
        acc_ref[...] = jnp.zeros_like(acc_ref)

    xn = xn_ref[...]
    y = (jax.nn.sigmoid(_mm(xn, ga_ref[...])) * _mm(oa_ref[...], wa_ref[...])
         + jax.nn.sigmoid(_mm(xn, gb_ref[...])) * _mm(ob_ref[...], wb_ref[...])
         + jax.nn.sigmoid(_mm(xn, gc_ref[...])) * _mm(oc_ref[...], wc_ref[...]))
    acc_ref[...] += _mm(y.astype(BF16), wo_ref[...])

    @pl.when(c == nc - 1)
    def _():
        o_ref[...] = h_ref[...] + acc_ref[...]


def _merge(h, gain, oa, ob, oc, wgates, wbr, wout, layer):
    n = h.shape[0]
    tc = 512
    nc = D_MODEL // tc
    bw = 512
    row = lambda i, c: (i, 0)
    wgate = lambda br: pl.BlockSpec((None, D_MODEL, tc), lambda i, c: (layer, 0, br * nc + c))
    wbranch = lambda br: pl.BlockSpec((None, None, bw, tc), lambda i, c: (layer, br, 0, c))
    return pl.pallas_call(
        functools.partial(_merge_kernel, nc=nc),
        grid=(n // TM, nc),
        in_specs=[
            pl.BlockSpec((TM, D_MODEL), row),
            pl.BlockSpec((None, 1, D_MODEL), lambda i, c: (layer, 0, 0)),
            pl.BlockSpec((TM, bw), row), pl.BlockSpec((TM, bw), row), pl.BlockSpec((TM, bw), row),
            wgate(0), wgate(1), wgate(2),
            wbranch(0), wbranch(1), wbranch(2),
            pl.BlockSpec((None, tc, D_MODEL), lambda i, c: (layer, c, 0)),
        ],
        out_specs=pl.BlockSpec((TM, D_MODEL), row),
        out_shape=jax.ShapeDtypeStruct((n, D_MODEL), F32),
        scratch_shapes=[pltpu.VMEM((TM, D_MODEL), BF16), pltpu.VMEM((TM, D_MODEL), F32)],
        input_output_aliases={0: 0},
        name="branch_merge",
        compiler_params=pltpu.CompilerParams(
            dimension_semantics=("parallel", "arbitrary"), vmem_limit_bytes=VMEM_LIMIT),
    )(h, gain, oa, ob, oc, wgates, wgates, wgates, wbr, wbr, wbr, wout)


def _final_kernel(h_ref, g_ref, o_ref):
    o_ref[...] = _rms(h_ref[...], g_ref[...])


def _final_norm(h, gain, bsz, nb):
    seq = (nb - 1) * BLK
    tr = 512 if seq % 512 == 0 else BLK
    return pl.pallas_call(
        _final_kernel,
        grid=(bsz, seq // tr),
        in_specs=[
            pl.BlockSpec((pl.Element(tr), pl.Element(D_MODEL)),
                         lambda b, r: (pl.multiple_of((b * nb + 1) * BLK + r * tr, BLK), 0)),
            pl.BlockSpec((1, D_MODEL), lambda b, r: (0, 0)),
        ],
        out_specs=pl.BlockSpec((None, tr, D_MODEL), lambda b, r: (b, r, 0)),
        out_shape=jax.ShapeDtypeStruct((bsz, seq, D_MODEL), F32),
        name="final_norm",
        compiler_params=pltpu.CompilerParams(dimension_semantics=("parallel", "parallel")),
    )(h, gain)


def _bucket_np(d):
    max_exact = N_BUCKETS // 2
    d = np.maximum(d, 0)
    df = np.maximum(d, 1).astype(np.float32)
    large = max_exact + (np.log(df / max_exact) / math.log(MAX_DISTANCE / max_exact)
                         * (N_BUCKETS - max_exact)).astype(np.int32)
    return np.where(d < max_exact, d, np.minimum(large, N_BUCKETS - 1)).astype(np.int32)


def _bias_tiles(table):
    kl = np.arange(BLK)[:, None]
    ql = np.arange(BLK)[None, :]
    idx = np.stack([_bucket_np(ql - kl), _bucket_np(BLK + ql - kl),
                    np.full((BLK, BLK), N_BUCKETS - 1, np.int32)])
    onehot = (idx.reshape(-1, 1) == np.arange(N_BUCKETS)[None, :]).astype(np.float32)
    tiles = jnp.dot(onehot, table.astype(F32), precision=lax.Precision.HIGHEST)
    return jnp.transpose(tiles.reshape(3, BLK, BLK, -1), (3, 0, 1, 2))


def _pack_w_in(w_in):
    o = np.cumsum([0, 512, 128, 128, 256, 256, 256, 256, 512, 512, 64, 64, 512, 64, 8, 3 * D_MODEL])
    w_in = w_in.astype(BF16)
    (aq, ak, av, bq1, bq2, bk1, bk2, bv, cq, ck, cv, iq, ik, iw, gates) = [
        w_in[:, :, o[t]:o[t + 1]] for t in range(15)]
    pad = jnp.zeros(w_in.shape[:2] + (N_PROJ - COL_CKIK - BLK,), BF16)
    w = jnp.concatenate([aq, cq, iq, bq1, bq2, bk1, bk2, ak, ck, ik, pad], axis=-1)
    padt = jnp.zeros(w_in.shape[:2] + (N_PROJ_T - ROW_IW - IDX_HEADS,), BF16)
    wt = jnp.swapaxes(lax.optimization_barrier(jnp.concatenate([av, bv, cv, iw, padt], axis=-1)), 1, 2)
    return w, wt, gates


def kernel(x, meta_tokens, rel_bias_table, norm_ffn1, ffn1_w_gate, ffn1_w_up, ffn1_w_down, norm_mix, w_in,
           attn_sinks, diff_lambda, diff_subln, w_branch, w_out, norm_ffn2, ffn2_w_gate, ffn2_w_up,
           ffn2_w_down, norm_final):
    bsz, seq, _ = x.shape
    nb = seq // BLK + 1
    p_rows = nb * BLK
    n = bsz * p_rows
    assert seq % BLK == 0 and n % TM_FFN == 0
    topk = min(TOPK_MAX, seq // 4)

    lead = jnp.concatenate([jnp.zeros((N_PAD, D_MODEL), x.dtype), meta_tokens.astype(x.dtype)], axis=0)

    bias = _bias_tiles(rel_bias_table)
    bias_a = jnp.transpose(bias[:A_HEADS].reshape(A_KV_HEADS, A_GRP, 3, BLK, BLK),
                           (0, 2, 3, 1, 4)).reshape(A_KV_HEADS, 3, BLK, A_LANES)
    bias_b = jnp.transpose(bias[A_HEADS:A_HEADS + B_HEADS].reshape(B_GROUPS, 2, 3, BLK, BLK),
                           (0, 2, 3, 1, 4)).reshape(B_GROUPS, 3, BLK, 2 * BLK)
    bias_c = jnp.transpose(bias[A_HEADS + B_HEADS:], (1, 2, 0, 3)).reshape(3, BLK, C_LANES)
    w_proj, w_proj_t, w_gates = _pack_w_in(w_in)
    g1, gm, g2 = (t.reshape(DEPTH, 1, D_MODEL) for t in (norm_ffn1, norm_mix, norm_ffn2))
    f1 = [t.astype(BF16) for t in (ffn1_w_gate, ffn1_w_up, ffn1_w_down)]
    f2 = [t.astype(BF16) for t in (ffn2_w_gate, ffn2_w_up, ffn2_w_down)]
    wbr, wout = w_branch.astype(BF16), w_out.astype(BF16)
    sinks = jnp.broadcast_to(attn_sinks.astype(F32)[:, :, None],
                             (DEPTH, A_HEADS, BLK)).reshape(DEPTH, A_KV_HEADS, 1, A_LANES)
    subln = jnp.broadcast_to(diff_subln.astype(F32)[:, :, None], (DEPTH, B_V_DIM, BLK))
    lam_f = diff_lambda.astype(F32)

    for l in range(DEPTH):
        h = _ffn_first(x, lead, g1, *f1, nb) if l == 0 else _ffn(h, g1, *f1, l)
        proj, avt, bvt, cvt, iwt = _proj(h, gm, w_proj, w_proj_t, l)
        o_a = _mixer_a(proj, avt, bias_a, sinks[l], bsz, nb)
        lam_init = 0.8 - 0.6 * math.exp(-0.3 * l)
        o_b = _mixer_b(proj, bvt, bias_b, lam_f[l:l + 1], subln[l:l + 1], bsz, nb, lam_init)
        o_c = _mixer_c(proj, cvt, iwt, bias_c, bsz, nb, topk)
        h = _merge(h, gm, o_a, o_b, o_c, w_gates, wbr, wout, l)
        h = _ffn(h, g2, *f2, l)

    return _final_norm(h, norm_final.reshape(1, D_MODEL), bsz, nb)
```

```python
import functools
import math

import jax
import jax.numpy as jnp
import numpy as np
from jax import lax
from jax.experimental import pallas as pl
from jax.experimental.pallas import tpu as pltpu

F32 = jnp.float32
BF16 = jnp.bfloat16

D_MODEL = 2048
D_FF = 5632
DEPTH = 2
BLK = 128
N_META = 16
N_PAD = BLK - N_META
WINDOW = 128
assert WINDOW == BLK
HEAD_DIM = 64
A_HEADS, A_KV_HEADS = 8, 2
B_HEADS, B_QK_DIM, B_V_DIM = 4, 64, 128
C_HEADS = 8
IDX_HEADS, IDX_DIM = 8, 64
TOPK_MAX = 256
N_BUCKETS, MAX_DISTANCE = 32, 128
EPS = 1e-6
NEG = -1e30

COL_AQ, COL_CQ, COL_IQ = 0, 512, 1024
COL_BQ1, COL_BQ2, COL_BK1, COL_BK2 = 1536, 1792, 2048, 2304
COL_AK, COL_CKIK = 2560, 2688
N_PROJ = 3072
ROW_AV, ROW_BV, ROW_CV, ROW_IW, N_PROJ_T = 0, 128, 640, 704, 720

V_ONES = 16
A_VROWS, B_VROWS, C_VROWS = HEAD_DIM + V_ONES, B_V_DIM + V_ONES, HEAD_DIM + V_ONES

TM = 512
TM_FFN = 1024
TM_PROJ = 1024
TF = 512
FFN_VMEM_LIMIT = 62 * 1024 * 1024
TN = 1024
TC_MERGE = 512
assert HEAD_DIM == B_QK_DIM == IDX_DIM == 64
QK_SCALE = 0.125
VMEM_LIMIT = 56 * 1024 * 1024


def _nt(a, b):
    return lax.dot_general(a, b, (((1,), (1,)), ((), ())), preferred_element_type=F32)


def _mm(a, b):
    return jnp.dot(a, b, preferred_element_type=F32)


def _rms(x, g):
    ms = jnp.mean(x * x, axis=-1, keepdims=True)
    return x * lax.rsqrt(ms + EPS) * g


def _ffn_accumulate(xn_ref, wg_ref, wu_ref, wd_ref, o_ref):
    for r in range(TM_FFN // TM):
        rows = slice(r * TM, (r + 1) * TM)
        xn = xn_ref[rows, :]
        a = _mm(xn, wg_ref[...])
        b = _mm(xn, wu_ref[...])
        t = (a * jax.nn.sigmoid(a)) * b
        o_ref[rows, :] += 0.5 * _mm(t.astype(BF16), wd_ref[...])


def _ffn_kernel(h_ref, g_ref, wg_ref, wu_ref, wd_ref, o_ref, xn_ref):
    @pl.when(pl.program_id(1) == 0)
    def _():
        h = h_ref[...]
        xn_ref[...] = _rms(h, g_ref[...]).astype(BF16)
        o_ref[...] = h

    _ffn_accumulate(xn_ref, wg_ref, wu_ref, wd_ref, o_ref)


def _ffn_first_kernel(*refs, nb):
    nblk = TM_FFN // BLK
    x_refs = refs[:nblk]
    lead_ref, g_ref, wg_ref, wu_ref, wd_ref, o_ref, xn_ref = refs[nblk:]

    @pl.when(pl.program_id(1) == 0)
    def _():
        first = pl.program_id(0) * nblk
        for t in range(nblk):
            rows = slice(t * BLK, (t + 1) * BLK)
            o_ref[rows, :] = x_refs[t][...]

            @pl.when(lax.rem(first + t, nb) == 0)
            def _():
                o_ref[rows, :] = lead_ref[...]

        xn_ref[...] = _rms(o_ref[...], g_ref[...]).astype(BF16)

    _ffn_accumulate(xn_ref, wg_ref, wu_ref, wd_ref, o_ref)


def _ffn_weight_specs(layer):
    return [
        pl.BlockSpec((None, 1, D_MODEL), lambda i, k: (layer, 0, 0)),
        pl.BlockSpec((None, D_MODEL, TF), lambda i, k: (layer, 0, k)),
        pl.BlockSpec((None, D_MODEL, TF), lambda i, k: (layer, 0, k)),
        pl.BlockSpec((None, TF, D_MODEL), lambda i, k: (layer, k, 0)),
    ]


def _ffn(h, gain, wg, wu, wd, layer):
    n = h.shape[0]
    return pl.pallas_call(
        _ffn_kernel,
        grid=(n // TM_FFN, D_FF // TF),
        in_specs=[pl.BlockSpec((TM_FFN, D_MODEL), lambda i, k: (i, 0))] + _ffn_weight_specs(layer),
        out_specs=pl.BlockSpec((TM_FFN, D_MODEL), lambda i, k: (i, 0)),
        out_shape=jax.ShapeDtypeStruct((n, D_MODEL), F32),
        scratch_shapes=[pltpu.VMEM((TM_FFN, D_MODEL), BF16)],
        input_output_aliases={0: 0},
        name="ffn",
        compiler_params=pltpu.CompilerParams(
            dimension_semantics=("parallel", "arbitrary"), vmem_limit_bytes=FFN_VMEM_LIMIT),
    )(h, gain, wg, wu, wd)


def _ffn_first(x, lead, gain, wg, wu, wd, nb):
    bsz = x.shape[0]
    n = bsz * nb * BLK
    nblk = TM_FFN // BLK

    def x_spec(t):
        return pl.BlockSpec((None, BLK, D_MODEL),
                            lambda i, k: ((i * nblk + t) // nb, jnp.maximum((i * nblk + t) % nb - 1, 0), 0))

    return pl.pallas_call(
        functools.partial(_ffn_first_kernel, nb=nb),
        grid=(n // TM_FFN, D_FF // TF),
        in_specs=[x_spec(t) for t in range(nblk)]
        + [pl.BlockSpec((BLK, D_MODEL), lambda i, k: (0, 0))] + _ffn_weight_specs(0),
        out_specs=pl.BlockSpec((TM_FFN, D_MODEL), lambda i, k: (i, 0)),
        out_shape=jax.ShapeDtypeStruct((n, D_MODEL), F32),
        scratch_shapes=[pltpu.VMEM((TM_FFN, D_MODEL), BF16)],
        name="ffn_first",
        compiler_params=pltpu.CompilerParams(
            dimension_semantics=("parallel", "arbitrary"), vmem_limit_bytes=FFN_VMEM_LIMIT),
    )(*([x] * nblk), lead, gain, wg, wu, wd)


def _proj_kernel(h_ref, g_ref, w_ref, wt_ref, o_ref, avt_ref, bvt_ref, cvt_ref, iwt_ref, xn_ref):
    j = pl.program_id(1)

    @pl.when(j == 0)
    def _():
        xn = _rms(h_ref[...], g_ref[...]).astype(BF16)
        xn_ref[...] = xn
        vt = _nt(wt_ref[...], xn)
        for t in range(TM_PROJ // BLK):
            cols = slice(t * BLK, (t + 1) * BLK)
            ones = jnp.ones((V_ONES, BLK), BF16)
            for g in range(A_KV_HEADS):
                avt_ref[t, g * A_VROWS:g * A_VROWS + HEAD_DIM] = (
                    vt[ROW_AV + g * HEAD_DIM:ROW_AV + (g + 1) * HEAD_DIM, cols].astype(BF16))
                avt_ref[t, g * A_VROWS + HEAD_DIM:(g + 1) * A_VROWS] = ones
            for h in range(B_HEADS):
                bvt_ref[t, h * B_VROWS:h * B_VROWS + B_V_DIM] = (
                    vt[ROW_BV + h * B_V_DIM:ROW_BV + (h + 1) * B_V_DIM, cols].astype(BF16))
                bvt_ref[t, h * B_VROWS + B_V_DIM:(h + 1) * B_VROWS] = ones
            cvt_ref[t, 0:HEAD_DIM] = vt[ROW_CV:ROW_IW, cols].astype(BF16)
            cvt_ref[t, HEAD_DIM:C_VROWS] = ones
            iwt_ref[t] = vt[ROW_IW:ROW_IW + IDX_HEADS, cols]

    o_ref[...] = _mm(xn_ref[...], w_ref[...]).astype(BF16)


def _proj(h, gain, w, wt, layer):
    n = h.shape[0]
    nt = n // BLK
    tpb = TM_PROJ // BLK
    return pl.pallas_call(
        _proj_kernel,
        grid=(n // TM_PROJ, N_PROJ // TN),
        in_specs=[
            pl.BlockSpec((TM_PROJ, D_MODEL), lambda i, j: (i, 0)),
            pl.BlockSpec((None, 1, D_MODEL), lambda i, j: (layer, 0, 0)),
            pl.BlockSpec((None, D_MODEL, TN), lambda i, j: (layer, 0, j)),
            pl.BlockSpec((None, N_PROJ_T, D_MODEL), lambda i, j: (layer, 0, 0)),
        ],
        out_specs=[
            pl.BlockSpec((TM_PROJ, TN), lambda i, j: (i, j)),
            pl.BlockSpec((tpb, A_KV_HEADS * A_VROWS, BLK), lambda i, j: (i, 0, 0)),
            pl.BlockSpec((tpb, B_HEADS * B_VROWS, BLK), lambda i, j: (i, 0, 0)),
            pl.BlockSpec((tpb, C_VROWS, BLK), lambda i, j: (i, 0, 0)),
            pl.BlockSpec((tpb, IDX_HEADS, BLK), lambda i, j: (i, 0, 0)),
        ],
        out_shape=[
            jax.ShapeDtypeStruct((n, N_PROJ), BF16),
            jax.ShapeDtypeStruct((nt, A_KV_HEADS * A_VROWS, BLK), BF16),
            jax.ShapeDtypeStruct((nt, B_HEADS * B_VROWS, BLK), BF16),
            jax.ShapeDtypeStruct((nt, C_VROWS, BLK), BF16),
            jax.ShapeDtypeStruct((nt, IDX_HEADS, BLK), F32),
        ],
        scratch_shapes=[pltpu.VMEM((TM_PROJ, D_MODEL), BF16)],
        name="mixer_proj",
        compiler_params=pltpu.CompilerParams(
            dimension_semantics=("parallel", "arbitrary"), vmem_limit_bytes=VMEM_LIMIT),
    )(h, gain, w, wt)


def _tile_iotas():
    kl = lax.broadcasted_iota(jnp.int32, (BLK, BLK), 0)
    ql = lax.broadcasted_iota(jnp.int32, (BLK, BLK), 1)
    return kl, ql


def _colmax(x):
    return jnp.max(x, axis=0, keepdims=True)


def _colsum(x):
    return jnp.sum(x, axis=0, keepdims=True)


def _exp_bf16(x):
    return jnp.exp(x.astype(BF16))


def _pair_tiles(p, nb):
    ja = 2 * p
    return ((ja, ja), (ja + 1, jnp.minimum(ja + 1, nb - 1)))


A_GRP = A_HEADS // A_KV_HEADS
A_LANES = A_GRP * BLK


def _mixer_a_kernel(q_ref, k0_ref, k1_ref, k2_ref, v0_ref, v1_ref, v2_ref, bias_ref, sink_ref, o_ref, qst_ref):
    i = pl.program_id(1)
    kl = lax.broadcasted_iota(jnp.int32, (BLK, A_LANES), 0)
    ql = lax.broadcasted_iota(jnp.int32, (BLK, A_LANES), 1) & (BLK - 1)
    qpos = BLK * i - N_PAD + ql
    ok_meta = (kl >= N_PAD) & (qpos >= kl - N_PAD)
    ok_prev = (BLK * (i - 1) - N_PAD + kl >= N_META) & (ql < kl)
    ok_cur = (BLK * i - N_PAD + kl >= N_META) & (ql >= kl)
    t_meta = jnp.minimum(i, 2)
    qst_ref[...] = jnp.zeros_like(qst_ref)
    for h in range(A_HEADS):
        g, r = divmod(h, A_GRP)
        qst_ref[g, r * BLK:(r + 1) * BLK, g * HEAD_DIM:(g + 1) * HEAD_DIM] = (
            q_ref[:, h * HEAD_DIM:(h + 1) * HEAD_DIM] * QK_SCALE)
    outs = []
    for g in range(A_KV_HEADS):
        qs = qst_ref[g]
        s0 = jnp.where(ok_meta, _nt(k0_ref[...], qs) + bias_ref[g, t_meta], NEG)
        s1 = jnp.where(ok_prev, _nt(k1_ref[...], qs) + bias_ref[g, 1], NEG)
        s2 = jnp.where(ok_cur, _nt(k2_ref[...], qs) + bias_ref[g, 0], NEG)
        sink = sink_ref[g]
        m = jnp.maximum(jnp.maximum(_colmax(s0), _colmax(s1)), jnp.maximum(_colmax(s2), sink))
        vs = slice(g * A_VROWS, (g + 1) * A_VROWS)
        ot = (_mm(v0_ref[vs, :], _exp_bf16(s0 - m)) + _mm(v1_ref[vs, :], _exp_bf16(s1 - m))
              + _mm(v2_ref[vs, :], _exp_bf16(s2 - m)))
        ot = ot[:HEAD_DIM] / (ot[HEAD_DIM:HEAD_DIM + 1] + jnp.exp(sink - m))
        outs.extend(ot[:, r * BLK:(r + 1) * BLK] for r in range(A_GRP))
    o_ref[...] = jnp.concatenate(outs, axis=0).T.astype(BF16)


def _mixer_a(proj, avt, bias, sinks, bsz, nb):
    n = proj.shape[0]
    kcol = COL_AK // BLK
    rowq = lambda b, i: (b * nb + i, COL_AQ // (A_HEADS * HEAD_DIM))
    spec_k = lambda f: pl.BlockSpec((BLK, BLK), lambda b, i: (b * nb + f(i), kcol))
    spec_v = lambda f: pl.BlockSpec((None, A_KV_HEADS * A_VROWS, BLK), lambda b, i: (b * nb + f(i), 0, 0))
    first = lambda i: 0
    prev = lambda i: jnp.maximum(i - 1, 0)
    cur = lambda i: i
    return pl.pallas_call(
        _mixer_a_kernel,
        grid=(bsz, nb),
        in_specs=[
            pl.BlockSpec((BLK, A_HEADS * HEAD_DIM), rowq),
            spec_k(first), spec_k(prev), spec_k(cur),
            spec_v(first), spec_v(prev), spec_v(cur),
            pl.BlockSpec((A_KV_HEADS, 3, BLK, A_LANES), lambda b, i: (0, 0, 0, 0)),
            pl.BlockSpec((A_KV_HEADS, 1, A_LANES), lambda b, i: (0, 0, 0)),
        ],
        out_specs=pl.BlockSpec((BLK, A_HEADS * HEAD_DIM), lambda b, i: (b * nb + i, 0)),
        out_shape=jax.ShapeDtypeStruct((n, A_HEADS * HEAD_DIM), BF16),
        scratch_shapes=[pltpu.VMEM((A_KV_HEADS, A_LANES, BLK), BF16)],
        name="mixer_a",
        compiler_params=pltpu.CompilerParams(dimension_semantics=("parallel", "parallel")),
    )(proj, proj, proj, proj, avt, avt, avt, bias, sinks)


B_GROUPS = B_HEADS // 2


def _mixer_b_kernel(q1_ref, q2_ref, k1_ref, k2_ref, vt_ref, bias_ref, lam_ref, subln_ref, o_ref,
                    qbd_ref, acc_ref, *, lam_init, nb):
    i = pl.program_id(1)
    npair = (i + 2) // 2
    w2 = 2 * BLK
    kl = lax.broadcasted_iota(jnp.int32, (BLK, w2), 0)
    qpos = BLK * i - N_PAD + (lax.broadcasted_iota(jnp.int32, (BLK, w2), 1) & (BLK - 1))

    feat = lax.broadcasted_iota(jnp.int32, (BLK, BLK), 0)
    for mp, q_ref in enumerate((q1_ref, q2_ref)):
        for g in range(B_GROUPS):
            qt = (q_ref[:, g * BLK:(g + 1) * BLK].astype(F32) * QK_SCALE).T
            qbd_ref[mp, g] = jnp.concatenate(
                [jnp.where(feat < B_QK_DIM, qt, 0.0), jnp.where(feat >= B_QK_DIM, qt, 0.0)], axis=1).astype(BF16)
    acc_ref[...] = jnp.zeros_like(acc_ref)

    def pair(p, ms):
        tiles = _pair_tiles(p, nb)
        oks, ts, rows = [], [], []
        for jl, jk in tiles:
            kpos = BLK * jl - N_PAD + kl
            oks.append((kpos >= 0) & (qpos >= kpos))
            ts.append(jnp.clip(i - jl, 0, 2))
            rows.append(pl.ds(pl.multiple_of(jk * BLK, BLK), BLK))
        raw = [[[_mm(k_ref[rows[tl], g * BLK:(g + 1) * BLK], qbd_ref[mp, g]) for tl in range(2)]
                for g in range(B_GROUPS)] for mp, k_ref in enumerate((k1_ref, k2_ref))]
        new_ms = []
        ps = [[None] * B_GROUPS for _ in range(2)]
        alphas = [[None] * B_GROUPS for _ in range(2)]
        for mp in range(2):
            row_m = []
            for g in range(B_GROUPS):
                ss = [jnp.where(oks[tl], raw[mp][g][tl] + bias_ref[g, ts[tl]], NEG) for tl in range(2)]
                m_old = ms[mp][g]
                m_new = jnp.maximum(m_old, jnp.maximum(_colmax(ss[0]), _colmax(ss[1])))
                row_m.append(m_new)
                ps[mp][g] = (_exp_bf16(ss[0] - m_new), _exp_bf16(ss[1] - m_new))
                alphas[mp][g] = jnp.exp(m_old - m_new)
            new_ms.append(row_m)
        for h in range(B_HEADS):
            g, hs = h // 2, slice((h % 2) * BLK, (h % 2 + 1) * BLK)
            alpha = jnp.concatenate([alphas[0][g][:, hs], alphas[1][g][:, hs]], axis=1)
            upd = alpha * acc_ref[h]
            for tl, (jl, jk) in enumerate(tiles):
                p12 = jnp.concatenate([ps[0][g][tl][:, hs], ps[1][g][tl][:, hs]], axis=1)
                upd = upd + _mm(vt_ref[jk, h * B_VROWS:(h + 1) * B_VROWS, :], p12)
            acc_ref[h] = upd
        return new_ms

    lax.fori_loop(0, npair, pair, [[jnp.full((1, w2), NEG, F32) for _ in range(B_GROUPS)] for _ in range(2)])

    lf = lam_ref[...]
    lam = (jnp.exp(jnp.sum(lf[0:1] * lf[1:2], axis=-1, keepdims=True))
           - jnp.exp(jnp.sum(lf[2:3] * lf[3:4], axis=-1, keepdims=True)) + lam_init)
    outs = []
    for h in range(B_HEADS):
        acc = acc_ref[h]
        num, den = acc[:B_V_DIM], acc[B_V_DIM:B_V_DIM + 1]
        o = num[:, :BLK] / den[:, :BLK] - lam * (num[:, BLK:] / den[:, BLK:])
        ms_ = jnp.mean(o * o, axis=0, keepdims=True)
        outs.append(o * lax.rsqrt(ms_ + EPS) * subln_ref[...] * (1.0 - lam_init))
    o_ref[...] = jnp.concatenate(outs, axis=0).T.astype(BF16)


def _mixer_b(proj, bvt, bias, lam, subln, bsz, nb, lam_init):
    n = proj.shape[0]
    p_rows = nb * BLK
    w = B_HEADS * B_QK_DIM
    rowq = lambda c: (lambda b, i: (b * nb + i, c))
    return pl.pallas_call(
        functools.partial(_mixer_b_kernel, lam_init=lam_init, nb=nb),
        grid=(bsz, nb),
        in_specs=[
            pl.BlockSpec((BLK, w), rowq(COL_BQ1 // w)),
            pl.BlockSpec((BLK, w), rowq(COL_BQ2 // w)),
            pl.BlockSpec((p_rows, w), lambda b, i: (b, COL_BK1 // w)),
            pl.BlockSpec((p_rows, w), lambda b, i: (b, COL_BK2 // w)),
            pl.BlockSpec((nb, B_HEADS * B_VROWS, BLK), lambda b, i: (b, 0, 0)),
            pl.BlockSpec((B_GROUPS, 3, BLK, 2 * BLK), lambda b, i: (0, 0, 0, 0)),
            pl.BlockSpec((None, 4, B_QK_DIM), lambda b, i: (0, 0, 0)),
            pl.BlockSpec((None, B_V_DIM, BLK), lambda b, i: (0, 0, 0)),
        ],
        out_specs=pl.BlockSpec((BLK, B_HEADS * B_V_DIM), lambda b, i: (b * nb + i, 0)),
        out_shape=jax.ShapeDtypeStruct((n, B_HEADS * B_V_DIM), BF16),
        scratch_shapes=[
            pltpu.VMEM((2, B_GROUPS, BLK, 2 * BLK), BF16),
            pltpu.VMEM((B_HEADS, B_VROWS, 2 * BLK), F32),
        ],
        name="mixer_b",
        compiler_params=pltpu.CompilerParams(dimension_semantics=("parallel", "arbitrary")),
    )(proj, proj, proj, proj, bvt, bias, lam, subln)


C_LANES = C_HEADS * BLK
C_CHUNK = 2 * BLK


def _ordered_bits_to_float(u):
    bits = jnp.where(u < 0, u ^ jnp.int32(-2 ** 31), ~u)
    return lax.bitcast_convert_type(bits, F32)


def _mixer_c_kernel(cq_ref, iq_ref, kk_ref, vt_ref, iwt_ref, bias_ref, o_ref,
                    score_ref, sel_ref, qst_ref, iqst_ref, acc_ref, *, topk, nb):
    i = pl.program_id(1)
    npair = (i + 2) // 2
    kl, ql = _tile_iotas()
    qpos = BLK * i - N_PAD + ql
    kf = jnp.float32(topk)

    qst_ref[...] = jnp.zeros_like(qst_ref)
    iqst_ref[...] = jnp.zeros_like(iqst_ref)
    for g in range(C_HEADS // 2):
        cols = slice(g * BLK, (g + 1) * BLK)
        qt = (cq_ref[:, cols].astype(F32) * QK_SCALE).T.astype(BF16)
        it = iq_ref[:, cols].astype(F32).T.astype(BF16)
        for hh in range(2):
            lanes = slice((2 * g + hh) * BLK, (2 * g + hh + 1) * BLK)
            qst_ref[0:HEAD_DIM, lanes] = qt[hh * HEAD_DIM:(hh + 1) * HEAD_DIM, :]
            iqst_ref[HEAD_DIM:BLK, lanes] = it[hh * IDX_DIM:(hh + 1) * IDX_DIM, :]
    iw = iwt_ref[...] * QK_SCALE
    iw_all = jnp.concatenate([iw[h:h + 1, :] for h in range(IDX_HEADS)], axis=1)

    def admissible(jl):
        kpos = BLK * jl - N_PAD + kl
        return (kpos >= 0) & (qpos >= kpos)

    def score_pair(p, carry):
        for jl, jk in _pair_tiles(p, nb):
            rows = pl.ds(pl.multiple_of(jk * BLK, BLK), BLK)
            kk = kk_ref[rows, :]
            sc = None
            for c in range(C_LANES // C_CHUNK):
                ln = slice(c * C_CHUNK, (c + 1) * C_CHUNK)
                rel = jnp.maximum(_mm(kk, iqst_ref[:, ln]), 0.0) * iw_all[:, ln]
                for h in range(C_CHUNK // BLK):
                    part = rel[:, h * BLK:(h + 1) * BLK]
                    sc = part if sc is None else sc + part
            score_ref[jl] = jnp.where(admissible(jl), sc * IDX_HEADS ** -0.5, NEG)
        return carry

    lax.fori_loop(0, npair, score_pair, 0)

    def count(pred):
        def body(p, cs):
            xa = jnp.where(pred(score_ref[2 * p]), 1.0, 0.0)
            xb = jnp.where(pred(score_ref[2 * p + 1]), 1.0, 0.0)
            return [c + (xa[r * 8:(r + 1) * 8, :] + xb[r * 8:(r + 1) * 8, :]) for r, c in enumerate(cs)]
        parts = lax.fori_loop(0, npair, body, [jnp.zeros((8, BLK), F32)] * (BLK // 8))
        while len(parts) > 1:
            parts = [parts[t] + parts[t + 1] for t in range(0, len(parts), 2)]
        return _colsum(parts[0])

    def search(it, u):
        cand = u | lax.shift_left(jnp.int32(1), 31 - it)
        thr_c = _ordered_bits_to_float(cand)
        return jnp.where(count(lambda s: s >= thr_c) >= kf, cand, u)

    u = lax.fori_loop(0, 32, search, jnp.zeros((1, BLK), jnp.int32))
    found = (u < 0) | (u >= jnp.int32(0x00800000))
    thr = jnp.where(found, _ordered_bits_to_float(u), -jnp.inf)
    need = kf - count(lambda s: s > thr)
    n_ge = count(lambda s: s >= thr)
    has_tie = jnp.max(jnp.where((n_ge > kf) & (thr > NEG), 1.0, 0.0)) > 0.5

    def select_plain():
        def body(p, carry):
            for jl in (2 * p, 2 * p + 1):
                sel_ref[jl] = jnp.where(admissible(jl) & (score_ref[jl] >= thr), 0.0, NEG)
            return carry
        lax.fori_loop(0, npair, body, 0)

    def select_ties():
        tri = (kl >= ql).astype(BF16)

        def body(jl, seen):
            sc = score_ref[jl]
            eq = sc == thr
            eqf = jnp.where(eq, 1.0, 0.0)
            rank = seen + _mm(tri, eqf.astype(BF16))
            sel = admissible(jl) & ((sc > thr) | (eq & (rank <= need)))
            sel_ref[jl] = jnp.where(sel, 0.0, NEG)
            return seen + _colsum(eqf)
        lax.fori_loop(0, 2 * npair, body, jnp.zeros((1, BLK), F32))

    lax.cond(has_tie, select_ties, select_plain)

    acc_ref[...] = jnp.zeros_like(acc_ref)

    def attend_pair(p, m_old):
        tiles = _pair_tiles(p, nb)
        ss = []
        for jl, jk in tiles:
            rows = pl.ds(pl.multiple_of(jk * BLK, BLK), BLK)
            ss.append(_mm(kk_ref[rows, :], qst_ref[...]) + bias_ref[jnp.clip(i - jl, 0, 2)]
                      + jnp.tile(sel_ref[jl], (1, C_HEADS)))
        m_new = jnp.maximum(m_old, jnp.maximum(_colmax(ss[0]), _colmax(ss[1])))
        acc_ref[...] = (jnp.exp(m_old - m_new) * acc_ref[...]
                        + _mm(vt_ref[tiles[0][1]], _exp_bf16(ss[0] - m_new))
                        + _mm(vt_ref[tiles[1][1]], _exp_bf16(ss[1] - m_new)))
        return m_new

    lax.fori_loop(0, npair, attend_pair, jnp.full((1, C_LANES), NEG, F32))
    o = acc_ref[0:HEAD_DIM, :] / acc_ref[HEAD_DIM:HEAD_DIM + 1, :]
    o_rows = jnp.concatenate([o[:, h * BLK:(h + 1) * BLK] for h in range(C_HEADS)], axis=0)
    o_ref[...] = o_rows.T.astype(BF16)


def _mixer_c(proj, cvt, iwt, bias, bsz, nb, topk):
    n = proj.shape[0]
    p_rows = nb * BLK
    w = C_HEADS * HEAD_DIM
    return pl.pallas_call(
        functools.partial(_mixer_c_kernel, topk=topk, nb=nb),
        grid=(bsz, nb),
        in_specs=[
            pl.BlockSpec((BLK, w), lambda b, i: (b * nb + i, COL_CQ // w)),
            pl.BlockSpec((BLK, w), lambda b, i: (b * nb + i, COL_IQ // w)),
            pl.BlockSpec((p_rows, BLK), lambda b, i: (b, COL_CKIK // BLK)),
            pl.BlockSpec((nb, C_VROWS, BLK), lambda b, i: (b, 0, 0)),
            pl.BlockSpec((None, IDX_HEADS, BLK), lambda b, i: (b * nb + i, 0, 0)),
            pl.BlockSpec((3, BLK, C_LANES), lambda b, i: (0, 0, 0)),
        ],
        out_specs=pl.BlockSpec((BLK, w), lambda b, i: (b * nb + i, 0)),
        out_shape=jax.ShapeDtypeStruct((n, w), BF16),
        scratch_shapes=[
            pltpu.VMEM((nb + 1, BLK, BLK), F32),
            pltpu.VMEM((nb + 1, BLK, BLK), F32),
            pltpu.VMEM((BLK, C_LANES), BF16),
            pltpu.VMEM((BLK, C_LANES), BF16),
            pltpu.VMEM((C_VROWS, C_LANES), F32),
        ],
        name="mixer_c",
        compiler_params=pltpu.CompilerParams(dimension_semantics=("parallel", "arbitrary")),
    )(proj, proj, proj, cvt, iwt, bias)


def _merge_kernel(h_ref, g_ref, oa_ref, ob_ref, oc_ref, wg_ref, wbr_ref, wo_ref, o_ref, xn_ref, acc_ref, *, nc, tc):
    c = pl.program_id(1)

    @pl.when(c == 0)
    def _():
        xn_ref[...] = _rms(h_ref[...], g_ref[...]).astype(BF16)
        acc_ref[...] = jnp.zeros_like(acc_ref)

    xn = xn_ref[...]
    y = None
    for br, ob_ref_ in enumerate((oa_ref, ob_ref, oc_ref)):
        term = (jax.nn.sigmoid(_mm(xn, wg_ref[:, br * tc:(br + 1) * tc])) * _mm(ob_ref_[...], wbr_ref[br]))
        y = term if y is None else y + term
    acc_ref[...] += _mm(y.astype(BF16), wo_ref[...])

    @pl.when(c == nc - 1)
    def _():
        o_ref[...] = h_ref[...] + acc_ref[...]


def _merge(h, gain, oa, ob, oc, wgates, wbr, wout, layer):
    n = h.shape[0]
    tc = TC_MERGE
    nc = D_MODEL // tc
    bw = 512
    row = lambda i, c: (i, 0)
    return pl.pallas_call(
        functools.partial(_merge_kernel, nc=nc, tc=tc),
        grid=(n // TM, nc),
        in_specs=[
            pl.BlockSpec((TM, D_MODEL), row),
            pl.BlockSpec((None, 1, D_MODEL), lambda i, c: (layer, 0, 0)),
            pl.BlockSpec((TM, bw), row), pl.BlockSpec((TM, bw), row), pl.BlockSpec((TM, bw), row),
            pl.BlockSpec((None, D_MODEL, 3 * tc), lambda i, c: (layer, 0, c)),
            pl.BlockSpec((None, 3, bw, tc), lambda i, c: (layer, 0, 0, c)),
            pl.BlockSpec((None, tc, D_MODEL), lambda i, c: (layer, c, 0)),
        ],
        out_specs=pl.BlockSpec((TM, D_MODEL), row),
        out_shape=jax.ShapeDtypeStruct((n, D_MODEL), F32),
        scratch_shapes=[pltpu.VMEM((TM, D_MODEL), BF16), pltpu.VMEM((TM, D_MODEL), F32)],
        input_output_aliases={0: 0},
        name="branch_merge",
        compiler_params=pltpu.CompilerParams(
            dimension_semantics=("parallel", "arbitrary"), vmem_limit_bytes=VMEM_LIMIT),
    )(h, gain, oa, ob, oc, wgates, wbr, wout)


def _final_kernel(h_ref, g_ref, o_ref):
    o_ref[...] = _rms(h_ref[...], g_ref[...])


def _final_norm(h, gain, bsz, nb):
    seq = (nb - 1) * BLK
    tr = 512 if seq % 512 == 0 else BLK
    return pl.pallas_call(
        _final_kernel,
        grid=(bsz, seq // tr),
        in_specs=[
            pl.BlockSpec((pl.Element(tr), pl.Element(D_MODEL)),
                         lambda b, r: (pl.multiple_of((b * nb + 1) * BLK + r * tr, BLK), 0)),
            pl.BlockSpec((1, D_MODEL), lambda b, r: (0, 0)),
        ],
        out_specs=pl.BlockSpec((None, tr, D_MODEL), lambda b, r: (b, r, 0)),
        out_shape=jax.ShapeDtypeStruct((bsz, seq, D_MODEL), F32),
        name="final_norm",
        compiler_params=pltpu.CompilerParams(dimension_semantics=("parallel", "parallel")),
    )(h, gain)


def _bucket_np(d):
    max_exact = N_BUCKETS // 2
    d = np.maximum(d, 0)
    df = np.maximum(d, 1).astype(np.float32)
    large = max_exact + (np.log(df / max_exact) / math.log(MAX_DISTANCE / max_exact)
                         * (N_BUCKETS - max_exact)).astype(np.int32)
    return np.where(d < max_exact, d, np.minimum(large, N_BUCKETS - 1)).astype(np.int32)


def _bias_tiles(table):
    kl = np.arange(BLK)[:, None]
    ql = np.arange(BLK)[None, :]
    idx = np.stack([_bucket_np(ql - kl), _bucket_np(BLK + ql - kl),
                    np.full((BLK, BLK), N_BUCKETS - 1, np.int32)])
    onehot = (idx.reshape(-1, 1) == np.arange(N_BUCKETS)[None, :]).astype(np.float32)
    tiles = jnp.dot(onehot, table.astype(F32), precision=lax.Precision.HIGHEST)
    return jnp.transpose(tiles.reshape(3, BLK, BLK, -1), (3, 0, 1, 2))


def _pack_w_in(w_in):
    o = np.cumsum([0, 512, 128, 128, 256, 256, 256, 256, 512, 512, 64, 64, 512, 64, 8, 3 * D_MODEL])
    w_in = w_in.astype(BF16)
    (aq, ak, av, bq1, bq2, bk1, bk2, bv, cq, ck, cv, iq, ik, iw, gates) = [
        w_in[:, :, o[t]:o[t + 1]] for t in range(15)]
    pad = jnp.zeros(w_in.shape[:2] + (N_PROJ - COL_CKIK - BLK,), BF16)
    w = jnp.concatenate([aq, cq, iq, bq1, bq2, bk1, bk2, ak, ck, ik, pad], axis=-1)
    padt = jnp.zeros(w_in.shape[:2] + (N_PROJ_T - ROW_IW - IDX_HEADS,), BF16)
    wt = jnp.swapaxes(lax.optimization_barrier(jnp.concatenate([av, bv, cv, iw, padt], axis=-1)), 1, 2)
    gates = jnp.swapaxes(gates.reshape(DEPTH, D_MODEL, 3, D_MODEL // TC_MERGE, TC_MERGE), 2, 3)
    return w, wt, gates.reshape(DEPTH, D_MODEL, 3 * D_MODEL)


def kernel(x, meta_tokens, rel_bias_table, norm_ffn1, ffn1_w_gate, ffn1_w_up, ffn1_w_down, norm_mix, w_in,
           attn_sinks, diff_lambda, diff_subln, w_branch, w_out, norm_ffn2, ffn2_w_gate, ffn2_w_up,
           ffn2_w_down, norm_final):
    bsz, seq, _ = x.shape
    nb = seq // BLK + 1
    p_rows = nb * BLK
    n = bsz * p_rows
    assert seq % BLK == 0 and n % TM_FFN == 0
    topk = min(TOPK_MAX, seq // 4)

    lead = jnp.concatenate([jnp.zeros((N_PAD, D_MODEL), x.dtype), meta_tokens.astype(x.dtype)], axis=0)

    bias = _bias_tiles(rel_bias_table)
    bias_a = jnp.transpose(bias[:A_HEADS].reshape(A_KV_HEADS, A_GRP, 3, BLK, BLK),
                           (0, 2, 3, 1, 4)).reshape(A_KV_HEADS, 3, BLK, A_LANES)
    bias_b = jnp.transpose(bias[A_HEADS:A_HEADS + B_HEADS].reshape(B_GROUPS, 2, 3, BLK, BLK),
                           (0, 2, 3, 1, 4)).reshape(B_GROUPS, 3, BLK, 2 * BLK)
    bias_c = jnp.transpose(bias[A_HEADS + B_HEADS:], (1, 2, 0, 3)).reshape(3, BLK, C_LANES)
    w_proj, w_proj_t, w_gates = _pack_w_in(w_in)
    g1, gm, g2 = (t.reshape(DEPTH, 1, D_MODEL) for t in (norm_ffn1, norm_mix, norm_ffn2))
    f1 = [t.astype(BF16) for t in (ffn1_w_gate, ffn1_w_up, ffn1_w_down)]
    f2 = [t.astype(BF16) for t in (ffn2_w_gate, ffn2_w_up, ffn2_w_down)]
    wbr, wout = w_branch.astype(BF16), w_out.astype(BF16)
    sinks = jnp.broadcast_to(attn_sinks.astype(F32)[:, :, None],
                             (DEPTH, A_HEADS, BLK)).reshape(DEPTH, A_KV_HEADS, 1, A_LANES)
    subln = jnp.broadcast_to(diff_subln.astype(F32)[:, :, None], (DEPTH, B_V_DIM, BLK))
    lam_f = diff_lambda.astype(F32)

    for l in range(DEPTH):
        h = _ffn_first(x, lead, g1, *f1, nb) if l == 0 else _ffn(h, g1, *f1, l)
        proj, avt, bvt, cvt, iwt = _proj(h, gm, w_proj, w_proj_t, l)
        o_a = _mixer_a(proj, avt, bias_a, sinks[l], bsz, nb)
        lam_init = 0.8 - 0.6 * math.exp(-0.3 * l)
        o_b = _mixer_b(proj, bvt, bias_b, lam_f[l:l + 1], subln[l:l + 1], bsz, nb, lam_init)
        o_c = _mixer_c(proj, cvt, iwt, bias_c, bsz, nb, topk)
        h = _merge(h, gm, o_a, o_b, o_c, w_gates, wbr, wout, l)
        h = _ffn(h, g2, *f2, l)

    return _final_norm(h, norm_final.reshape(1, D_MODEL), bsz, nb)
```

```python
import functools
import math

import jax
import jax.numpy as jnp
import numpy as np
from jax import lax
from jax.experimental import pallas as pl
from jax.experimental.pallas import tpu as pltpu

F32 = jnp.float32
BF16 = jnp.bfloat16

D_MODEL = 2048
D_FF = 5632
DEPTH = 2
BLK = 128
N_META = 16
N_PAD = BLK - N_META
WINDOW = 128
assert WINDOW == BLK
HEAD_DIM = 64
A_HEADS, A_KV_HEADS = 8, 2
B_HEADS, B_QK_DIM, B_V_DIM = 4, 64, 128
C_HEADS = 8
IDX_HEADS, IDX_DIM = 8, 64
TOPK_MAX = 256
N_BUCKETS, MAX_DISTANCE = 32, 128
EPS = 1e-6
NEG = -1e30

COL_AQ, COL_CQ, COL_IQ = 0, 512, 1024
COL_BQ1, COL_BQ2, COL_BK1, COL_BK2 = 1536, 1792, 2048, 2304
COL_AK, COL_CKIK = 2560, 2688
N_PROJ = 3072
ROW_AV, ROW_BV, ROW_CV, ROW_IW, N_PROJ_T = 0, 128, 640, 704, 720

V_ONES = 16
A_VROWS, B_VROWS, C_VROWS = HEAD_DIM + V_ONES, B_V_DIM + V_ONES, HEAD_DIM + V_ONES

TM = 512
TM_FFN = 1024
TM_PROJ = 1024
TF = 512
FFN_VMEM_LIMIT = 62 * 1024 * 1024
TN = 1024
TC_MERGE = 512
assert HEAD_DIM == B_QK_DIM == IDX_DIM == 64
QK_SCALE = 0.125
VMEM_LIMIT = 56 * 1024 * 1024


def _nt(a, b):
    return lax.dot_general(a, b, (((1,), (1,)), ((), ())), preferred_element_type=F32)


def _mm(a, b):
    return jnp.dot(a, b, preferred_element_type=F32)


def _rms(x, g):
    ms = jnp.mean(x * x, axis=-1, keepdims=True)
    return x * lax.rsqrt(ms + EPS) * g


def _ffn_accumulate(xn_ref, wg_ref, wu_ref, wd_ref, o_ref):
    for r in range(TM_FFN // TM):
        rows = slice(r * TM, (r + 1) * TM)
        xn = xn_ref[rows, :]
        a = _mm(xn, wg_ref[...])
        b = _mm(xn, wu_ref[...])
        t = (a * jax.nn.sigmoid(a)) * b
        o_ref[rows, :] += 0.5 * _mm(t.astype(BF16), wd_ref[...])


def _ffn_kernel(h_ref, g_ref, wg_ref, wu_ref, wd_ref, o_ref, xn_ref):
    @pl.when(pl.program_id(1) == 0)
    def _():
        h = h_ref[...]
        xn_ref[...] = _rms(h, g_ref[...]).astype(BF16)
        o_ref[...] = h

    _ffn_accumulate(xn_ref, wg_ref, wu_ref, wd_ref, o_ref)


def _ffn_first_kernel(*refs, nb):
    nblk = TM_FFN // BLK
    x_refs = refs[:nblk]
    lead_ref, g_ref, wg_ref, wu_ref, wd_ref, o_ref, xn_ref = refs[nblk:]

    @pl.when(pl.program_id(1) == 0)
    def _():
        first = pl.program_id(0) * nblk
        for t in range(nblk):
            rows = slice(t * BLK, (t + 1) * BLK)
            o_ref[rows, :] = x_refs[t][...]

            @pl.when(lax.rem(first + t, nb) == 0)
            def _():
                o_ref[rows, :] = lead_ref[...]

        xn_ref[...] = _rms(o_ref[...], g_ref[...]).astype(BF16)

    _ffn_accumulate(xn_ref, wg_ref, wu_ref, wd_ref, o_ref)


def _ffn_weight_specs(layer):
    return [
        pl.BlockSpec((None, 1, D_MODEL), lambda i, k: (layer, 0, 0)),
        pl.BlockSpec((None, D_MODEL, TF), lambda i, k: (layer, 0, k)),
        pl.BlockSpec((None, D_MODEL, TF), lambda i, k: (layer, 0, k)),
        pl.BlockSpec((None, TF, D_MODEL), lambda i, k: (layer, k, 0)),
    ]


def _ffn(h, gain, wg, wu, wd, layer):
    n = h.shape[0]
    return pl.pallas_call(
        _ffn_kernel,
        grid=(n // TM_FFN, D_FF // TF),
        in_specs=[pl.BlockSpec((TM_FFN, D_MODEL), lambda i, k: (i, 0))] + _ffn_weight_specs(layer),
        out_specs=pl.BlockSpec((TM_FFN, D_MODEL), lambda i, k: (i, 0)),
        out_shape=jax.ShapeDtypeStruct((n, D_MODEL), F32),
        scratch_shapes=[pltpu.VMEM((TM_FFN, D_MODEL), BF16)],
        input_output_aliases={0: 0},
        name="ffn",
        compiler_params=pltpu.CompilerParams(
            dimension_semantics=("parallel", "arbitrary"), vmem_limit_bytes=FFN_VMEM_LIMIT),
    )(h, gain, wg, wu, wd)


def _ffn_first(x, lead, gain, wg, wu, wd, nb):
    bsz = x.shape[0]
    n = bsz * nb * BLK
    nblk = TM_FFN // BLK

    def x_spec(t):
        return pl.BlockSpec((None, BLK, D_MODEL),
                            lambda i, k: ((i * nblk + t) // nb, jnp.maximum((i * nblk + t) % nb - 1, 0), 0))

    return pl.pallas_call(
        functools.partial(_ffn_first_kernel, nb=nb),
        grid=(n // TM_FFN, D_FF // TF),
        in_specs=[x_spec(t) for t in range(nblk)]
        + [pl.BlockSpec((BLK, D_MODEL), lambda i, k: (0, 0))] + _ffn_weight_specs(0),
        out_specs=pl.BlockSpec((TM_FFN, D_MODEL), lambda i, k: (i, 0)),
        out_shape=jax.ShapeDtypeStruct((n, D_MODEL), F32),
        scratch_shapes=[pltpu.VMEM((TM_FFN, D_MODEL), BF16)],
        name="ffn_first",
        compiler_params=pltpu.CompilerParams(
            dimension_semantics=("parallel", "arbitrary"), vmem_limit_bytes=FFN_VMEM_LIMIT),
    )(*([x] * nblk), lead, gain, wg, wu, wd)


def _proj_kernel(h_ref, g_ref, w_ref, wt_ref, o_ref, avt_ref, bvt_ref, cvt_ref, iwt_ref, xn_ref):
    j = pl.program_id(1)

    @pl.when(j == 0)
    def _():
        xn = _rms(h_ref[...], g_ref[...]).astype(BF16)
        xn_ref[...] = xn
        vt = _nt(wt_ref[...], xn)
        for t in range(TM_PROJ // BLK):
            cols = slice(t * BLK, (t + 1) * BLK)
            ones = jnp.ones((V_ONES, BLK), BF16)
            for g in range(A_KV_HEADS):
                avt_ref[t, g * A_VROWS:g * A_VROWS + HEAD_DIM] = (
                    vt[ROW_AV + g * HEAD_DIM:ROW_AV + (g + 1) * HEAD_DIM, cols].astype(BF16))
                avt_ref[t, g * A_VROWS + HEAD_DIM:(g + 1) * A_VROWS] = ones
            for h in range(B_HEADS):
                bvt_ref[t, h * B_VROWS:h * B_VROWS + B_V_DIM] = (
                    vt[ROW_BV + h * B_V_DIM:ROW_BV + (h + 1) * B_V_DIM, cols].astype(BF16))
                bvt_ref[t, h * B_VROWS + B_V_DIM:(h + 1) * B_VROWS] = ones
            cvt_ref[t, 0:HEAD_DIM] = vt[ROW_CV:ROW_IW, cols].astype(BF16)
            cvt_ref[t, HEAD_DIM:C_VROWS] = ones
            iwt_ref[t] = vt[ROW_IW:ROW_IW + IDX_HEADS, cols]

    o_ref[...] = _mm(xn_ref[...], w_ref[...]).astype(BF16)


def _proj(h, gain, w, wt, layer):
    n = h.shape[0]
    nt = n // BLK
    tpb = TM_PROJ // BLK
    return pl.pallas_call(
        _proj_kernel,
        grid=(n // TM_PROJ, N_PROJ // TN),
        in_specs=[
            pl.BlockSpec((TM_PROJ, D_MODEL), lambda i, j: (i, 0)),
            pl.BlockSpec((None, 1, D_MODEL), lambda i, j: (layer, 0, 0)),
            pl.BlockSpec((None, D_MODEL, TN), lambda i, j: (layer, 0, j)),
            pl.BlockSpec((None, N_PROJ_T, D_MODEL), lambda i, j: (layer, 0, 0)),
        ],
        out_specs=[
            pl.BlockSpec((TM_PROJ, TN), lambda i, j: (i, j)),
            pl.BlockSpec((tpb, A_KV_HEADS * A_VROWS, BLK), lambda i, j: (i, 0, 0)),
            pl.BlockSpec((tpb, B_HEADS * B_VROWS, BLK), lambda i, j: (i, 0, 0)),
            pl.BlockSpec((tpb, C_VROWS, BLK), lambda i, j: (i, 0, 0)),
            pl.BlockSpec((tpb, IDX_HEADS, BLK), lambda i, j: (i, 0, 0)),
        ],
        out_shape=[
            jax.ShapeDtypeStruct((n, N_PROJ), BF16),
            jax.ShapeDtypeStruct((nt, A_KV_HEADS * A_VROWS, BLK), BF16),
            jax.ShapeDtypeStruct((nt, B_HEADS * B_VROWS, BLK), BF16),
            jax.ShapeDtypeStruct((nt, C_VROWS, BLK), BF16),
            jax.ShapeDtypeStruct((nt, IDX_HEADS, BLK), F32),
        ],
        scratch_shapes=[pltpu.VMEM((TM_PROJ, D_MODEL), BF16)],
        name="mixer_proj",
        compiler_params=pltpu.CompilerParams(
            dimension_semantics=("parallel", "arbitrary"), vmem_limit_bytes=VMEM_LIMIT),
    )(h, gain, w, wt)


def _tile_iotas():
    kl = lax.broadcasted_iota(jnp.int32, (BLK, BLK), 0)
    ql = lax.broadcasted_iota(jnp.int32, (BLK, BLK), 1)
    return kl, ql


def _colmax(x):
    return jnp.max(x, axis=0, keepdims=True)


def _colsum(x):
    return jnp.sum(x, axis=0, keepdims=True)


def _exp_bf16(x):
    return jnp.exp(x.astype(BF16))


def _pair_tiles(p, nb):
    ja = 2 * p
    return ((ja, ja), (ja + 1, jnp.minimum(ja + 1, nb - 1)))


A_GRP = A_HEADS // A_KV_HEADS
A_LANES = A_GRP * BLK


def _mixer_a_kernel(q_ref, k0_ref, k1_ref, k2_ref, v0_ref, v1_ref, v2_ref, bias_ref, sink_ref, o_ref, qst_ref):
    i = pl.program_id(1)
    kl = lax.broadcasted_iota(jnp.int32, (BLK, A_LANES), 0)
    ql = lax.broadcasted_iota(jnp.int32, (BLK, A_LANES), 1) & (BLK - 1)
    qpos = BLK * i - N_PAD + ql
    ok_meta = (kl >= N_PAD) & (qpos >= kl - N_PAD)
    ok_prev = (BLK * (i - 1) - N_PAD + kl >= N_META) & (ql < kl)
    ok_cur = (BLK * i - N_PAD + kl >= N_META) & (ql >= kl)
    t_meta = jnp.minimum(i, 2)
    qst_ref[...] = jnp.zeros_like(qst_ref)
    for h in range(A_HEADS):
        g, r = divmod(h, A_GRP)
        qst_ref[g, r * BLK:(r + 1) * BLK, g * HEAD_DIM:(g + 1) * HEAD_DIM] = (
            q_ref[:, h * HEAD_DIM:(h + 1) * HEAD_DIM] * QK_SCALE)
    outs = []
    for g in range(A_KV_HEADS):
        qs = qst_ref[g]
        s0 = jnp.where(ok_meta, _nt(k0_ref[...], qs) + bias_ref[g, t_meta], NEG)
        s1 = jnp.where(ok_prev, _nt(k1_ref[...], qs) + bias_ref[g, 1], NEG)
        s2 = jnp.where(ok_cur, _nt(k2_ref[...], qs) + bias_ref[g, 0], NEG)
        sink = sink_ref[g]
        m = jnp.maximum(jnp.maximum(_colmax(s0), _colmax(s1)), jnp.maximum(_colmax(s2), sink))
        vs = slice(g * A_VROWS, (g + 1) * A_VROWS)
        ot = (_mm(v0_ref[vs, :], _exp_bf16(s0 - m)) + _mm(v1_ref[vs, :], _exp_bf16(s1 - m))
              + _mm(v2_ref[vs, :], _exp_bf16(s2 - m)))
        ot = ot[:HEAD_DIM] / (ot[HEAD_DIM:HEAD_DIM + 1] + jnp.exp(sink - m))
        outs.extend(ot[:, r * BLK:(r + 1) * BLK] for r in range(A_GRP))
    o_ref[...] = jnp.concatenate(outs, axis=0).T.astype(BF16)


def _mixer_a(proj, avt, bias, sinks, bsz, nb):
    n = proj.shape[0]
    kcol = COL_AK // BLK
    rowq = lambda b, i: (b * nb + i, COL_AQ // (A_HEADS * HEAD_DIM))
    spec_k = lambda f: pl.BlockSpec((BLK, BLK), lambda b, i: (b * nb + f(i), kcol))
    spec_v = lambda f: pl.BlockSpec((None, A_KV_HEADS * A_VROWS, BLK), lambda b, i: (b * nb + f(i), 0, 0))
    first = lambda i: 0
    prev = lambda i: jnp.maximum(i - 1, 0)
    cur = lambda i: i
    return pl.pallas_call(
        _mixer_a_kernel,
        grid=(bsz, nb),
        in_specs=[
            pl.BlockSpec((BLK, A_HEADS * HEAD_DIM), rowq),
            spec_k(first), spec_k(prev), spec_k(cur),
            spec_v(first), spec_v(prev), spec_v(cur),
            pl.BlockSpec((A_KV_HEADS, 3, BLK, A_LANES), lambda b, i: (0, 0, 0, 0)),
            pl.BlockSpec((A_KV_HEADS, 1, A_LANES), lambda b, i: (0, 0, 0)),
        ],
        out_specs=pl.BlockSpec((BLK, A_HEADS * HEAD_DIM), lambda b, i: (b * nb + i, 0)),
        out_shape=jax.ShapeDtypeStruct((n, A_HEADS * HEAD_DIM), BF16),
        scratch_shapes=[pltpu.VMEM((A_KV_HEADS, A_LANES, BLK), BF16)],
        name="mixer_a",
        compiler_params=pltpu.CompilerParams(dimension_semantics=("parallel", "parallel")),
    )(proj, proj, proj, proj, avt, avt, avt, bias, sinks)


B_GROUPS = B_HEADS // 2


def _mixer_b_kernel(q1_ref, q2_ref, k1_ref, k2_ref, vt_ref, bias_ref, lam_ref, subln_ref, o_ref,
                    qbd_ref, acc_ref, *, lam_init, nb):
    i = pl.program_id(1)
    npair = (i + 2) // 2
    w2 = 2 * BLK
    kl = lax.broadcasted_iota(jnp.int32, (BLK, w2), 0)
    qpos = BLK * i - N_PAD + (lax.broadcasted_iota(jnp.int32, (BLK, w2), 1) & (BLK - 1))

    feat = lax.broadcasted_iota(jnp.int32, (BLK, BLK), 0)
    for mp, q_ref in enumerate((q1_ref, q2_ref)):
        for g in range(B_GROUPS):
            qt = (q_ref[:, g * BLK:(g + 1) * BLK].astype(F32) * QK_SCALE).T
            qbd_ref[mp, g] = jnp.concatenate(
                [jnp.where(feat < B_QK_DIM, qt, 0.0), jnp.where(feat >= B_QK_DIM, qt, 0.0)], axis=1).astype(BF16)
    acc_ref[...] = jnp.zeros_like(acc_ref)

    def pair(p, ms):
        tiles = _pair_tiles(p, nb)
        oks, ts, rows = [], [], []
        for jl, jk in tiles:
            kpos = BLK * jl - N_PAD + kl
            oks.append((kpos >= 0) & (qpos >= kpos))
            ts.append(jnp.clip(i - jl, 0, 2))
            rows.append(pl.ds(pl.multiple_of(jk * BLK, BLK), BLK))
        raw = [[[_mm(k_ref[rows[tl], g * BLK:(g + 1) * BLK], qbd_ref[mp, g]) for tl in range(2)]
                for g in range(B_GROUPS)] for mp, k_ref in enumerate((k1_ref, k2_ref))]
        new_ms = []
        ps = [[None] * B_GROUPS for _ in range(2)]
        alphas = [[None] * B_GROUPS for _ in range(2)]
        for mp in range(2):
            row_m = []
            for g in range(B_GROUPS):
                ss = [jnp.where(oks[tl], raw[mp][g][tl] + bias_ref[g, ts[tl]], NEG) for tl in range(2)]
                m_old = ms[mp][g]
                m_new = jnp.maximum(m_old, jnp.maximum(_colmax(ss[0]), _colmax(ss[1])))
                row_m.append(m_new)
                ps[mp][g] = (_exp_bf16(ss[0] - m_new), _exp_bf16(ss[1] - m_new))
                alphas[mp][g] = jnp.exp(m_old - m_new)
            new_ms.append(row_m)
        for h in range(B_HEADS):
            g, hs = h // 2, slice((h % 2) * BLK, (h % 2 + 1) * BLK)
            alpha = jnp.concatenate([alphas[0][g][:, hs], alphas[1][g][:, hs]], axis=1)
            upd = alpha * acc_ref[h]
            for tl, (jl, jk) in enumerate(tiles):
                p12 = jnp.concatenate([ps[0][g][tl][:, hs], ps[1][g][tl][:, hs]], axis=1)
                upd = upd + _mm(vt_ref[jk, h * B_VROWS:(h + 1) * B_VROWS, :], p12)
            acc_ref[h] = upd
        return new_ms

    lax.fori_loop(0, npair, pair, [[jnp.full((1, w2), NEG, F32) for _ in range(B_GROUPS)] for _ in range(2)])

    lf = lam_ref[...]
    lam = (jnp.exp(jnp.sum(lf[0:1] * lf[1:2], axis=-1, keepdims=True))
           - jnp.exp(jnp.sum(lf[2:3] * lf[3:4], axis=-1, keepdims=True)) + lam_init)
    outs = []
    for h in range(B_HEADS):
        acc = acc_ref[h]
        num, den = acc[:B_V_DIM], acc[B_V_DIM:B_V_DIM + 1]
        o = num[:, :BLK] / den[:, :BLK] - lam * (num[:, BLK:] / den[:, BLK:])
        ms_ = jnp.mean(o * o, axis=0, keepdims=True)
        outs.append(o * lax.rsqrt(ms_ + EPS) * subln_ref[...] * (1.0 - lam_init))
    o_ref[...] = jnp.concatenate(outs, axis=0).T.astype(BF16)


def _mixer_b(proj, bvt, bias, lam, subln, bsz, nb, lam_init):
    n = proj.shape[0]
    p_rows = nb * BLK
    w = B_HEADS * B_QK_DIM
    rowq = lambda c: (lambda b, i: (b * nb + i, c))
    return pl.pallas_call(
        functools.partial(_mixer_b_kernel, lam_init=lam_init, nb=nb),
        grid=(bsz, nb),
        in_specs=[
            pl.BlockSpec((BLK, w), rowq(COL_BQ1 // w)),
            pl.BlockSpec((BLK, w), rowq(COL_BQ2 // w)),
            pl.BlockSpec((p_rows, w), lambda b, i: (b, COL_BK1 // w)),
            pl.BlockSpec((p_rows, w), lambda b, i: (b, COL_BK2 // w)),
            pl.BlockSpec((nb, B_HEADS * B_VROWS, BLK), lambda b, i: (b, 0, 0)),
            pl.BlockSpec((B_GROUPS, 3, BLK, 2 * BLK), lambda b, i: (0, 0, 0, 0)),
            pl.BlockSpec((None, 4, B_QK_DIM), lambda b, i: (0, 0, 0)),
            pl.BlockSpec((None, B_V_DIM, BLK), lambda b, i: (0, 0, 0)),
        ],
        out_specs=pl.BlockSpec((BLK, B_HEADS * B_V_DIM), lambda b, i: (b * nb + i, 0)),
        out_shape=jax.ShapeDtypeStruct((n, B_HEADS * B_V_DIM), BF16),
        scratch_shapes=[
            pltpu.VMEM((2, B_GROUPS, BLK, 2 * BLK), BF16),
            pltpu.VMEM((B_HEADS, B_VROWS, 2 * BLK), F32),
        ],
        name="mixer_b",
        compiler_params=pltpu.CompilerParams(dimension_semantics=("parallel", "arbitrary")),
    )(proj, proj, proj, proj, bvt, bias, lam, subln)


C_LANES = C_HEADS * BLK
C_CHUNK = 2 * BLK


def _ordered_bits_to_float(u):
    bits = jnp.where(u < 0, u ^ jnp.int32(-2 ** 31), ~u)
    return lax.bitcast_convert_type(bits, F32)


def _mixer_c_kernel(cq_ref, iq_ref, kk_ref, vt_ref, iwt_ref, bias_ref, o_ref,
                    score_ref, sel_ref, qst_ref, iqst_ref, acc_ref, *, topk, nb):
    i = pl.program_id(1)
    npair = (i + 2) // 2
    kl, ql = _tile_iotas()
    qpos = BLK * i - N_PAD + ql
    kf = jnp.float32(topk)

    qst_ref[...] = jnp.zeros_like(qst_ref)
    iqst_ref[...] = jnp.zeros_like(iqst_ref)
    for g in range(C_HEADS // 2):
        cols = slice(g * BLK, (g + 1) * BLK)
        qt = (cq_ref[:, cols].astype(F32) * QK_SCALE).T.astype(BF16)
        it = iq_ref[:, cols].astype(F32).T.astype(BF16)
        for hh in range(2):
            lanes = slice((2 * g + hh) * BLK, (2 * g + hh + 1) * BLK)
            qst_ref[0:HEAD_DIM, lanes] = qt[hh * HEAD_DIM:(hh + 1) * HEAD_DIM, :]
            iqst_ref[HEAD_DIM:BLK, lanes] = it[hh * IDX_DIM:(hh + 1) * IDX_DIM, :]
    iw = iwt_ref[...] * QK_SCALE
    iw_all = jnp.concatenate([iw[h:h + 1, :] for h in range(IDX_HEADS)], axis=1)

    def admissible(jl):
        kpos = BLK * jl - N_PAD + kl
        return (kpos >= 0) & (qpos >= kpos)

    def score_pair(p, carry):
        for jl, jk in _pair_tiles(p, nb):
            rows = pl.ds(pl.multiple_of(jk * BLK, BLK), BLK)
            kk = kk_ref[rows, :]
            sc = None
            for c in range(C_LANES // C_CHUNK):
                ln = slice(c * C_CHUNK, (c + 1) * C_CHUNK)
                rel = jnp.maximum(_mm(kk, iqst_ref[:, ln]), 0.0) * iw_all[:, ln]
                for h in range(C_CHUNK // BLK):
                    part = rel[:, h * BLK:(h + 1) * BLK]
                    sc = part if sc is None else sc + part
            score_ref[jl] = jnp.where(admissible(jl), sc * IDX_HEADS ** -0.5, NEG)
        return carry

    lax.fori_loop(0, npair, score_pair, 0)

    def count(pred):
        def body(p, cs):
            xa = jnp.where(pred(score_ref[2 * p]), 1.0, 0.0)
            xb = jnp.where(pred(score_ref[2 * p + 1]), 1.0, 0.0)
            return [c + (xa[r * 8:(r + 1) * 8, :] + xb[r * 8:(r + 1) * 8, :]) for r, c in enumerate(cs)]
        parts = lax.fori_loop(0, npair, body, [jnp.zeros((8, BLK), F32)] * (BLK // 8))
        while len(parts) > 1:
            parts = [parts[t] + parts[t + 1] for t in range(0, len(parts), 2)]
        return _colsum(parts[0])

    def search(it, u):
        cand = u | lax.shift_left(jnp.int32(1), 31 - it)
        thr_c = _ordered_bits_to_float(cand)
        return jnp.where(count(lambda s: s >= thr_c) >= kf, cand, u)

    u = lax.fori_loop(0, 32, search, jnp.zeros((1, BLK), jnp.int32))
    found = (u < 0) | (u >= jnp.int32(0x00800000))
    thr = jnp.where(found, _ordered_bits_to_float(u), -jnp.inf)
    need = kf - count(lambda s: s > thr)
    n_ge = count(lambda s: s >= thr)
    has_tie = jnp.max(jnp.where((n_ge > kf) & (thr > NEG), 1.0, 0.0)) > 0.5

    def select_plain():
        def body(p, carry):
            for jl in (2 * p, 2 * p + 1):
                sel_ref[jl] = jnp.where(admissible(jl) & (score_ref[jl] >= thr), 0.0, NEG)
            return carry
        lax.fori_loop(0, npair, body, 0)

    def select_ties():
        tri = (kl >= ql).astype(BF16)

        def body(jl, seen):
            sc = score_ref[jl]
            eq = sc == thr
            eqf = jnp.where(eq, 1.0, 0.0)
            rank = seen + _mm(tri, eqf.astype(BF16))
            sel = admissible(jl) & ((sc > thr) | (eq & (rank <= need)))
            sel_ref[jl] = jnp.where(sel, 0.0, NEG)
            return seen + _colsum(eqf)
        lax.fori_loop(0, 2 * npair, body, jnp.zeros((1, BLK), F32))

    lax.cond(has_tie, select_ties, select_plain)

    acc_ref[...] = jnp.zeros_like(acc_ref)

    def attend_pair(p, m_old):
        tiles = _pair_tiles(p, nb)
        ss = []
        for jl, jk in tiles:
            rows = pl.ds(pl.multiple_of(jk * BLK, BLK), BLK)
            ss.append(_mm(kk_ref[rows, :], qst_ref[...]) + bias_ref[jnp.clip(i - jl, 0, 2)]
                      + jnp.tile(sel_ref[jl], (1, C_HEADS)))
        m_new = jnp.maximum(m_old, jnp.maximum(_colmax(ss[0]), _colmax(ss[1])))
        acc_ref[...] = (jnp.exp(m_old - m_new) * acc_ref[...]
                        + _mm(vt_ref[tiles[0][1]], _exp_bf16(ss[0] - m_new))
                        + _mm(vt_ref[tiles[1][1]], _exp_bf16(ss[1] - m_new)))
        return m_new

    lax.fori_loop(0, npair, attend_pair, jnp.full((1, C_LANES), NEG, F32))
    o = acc_ref[0:HEAD_DIM, :] / acc_ref[HEAD_DIM:HEAD_DIM + 1, :]
    o_rows = jnp.concatenate([o[:, h * BLK:(h + 1) * BLK] for h in range(C_HEADS)], axis=0)
    o_ref[...] = o_rows.T.astype(BF16)


def _mixer_c(proj, cvt, iwt, bias, bsz, nb, topk):
    n = proj.shape[0]
    p_rows = nb * BLK
    w = C_HEADS * HEAD_DIM
    return pl.pallas_call(
        functools.partial(_mixer_c_kernel, topk=topk, nb=nb),
        grid=(bsz, nb),
        in_specs=[
            pl.BlockSpec((BLK, w), lambda b, i: (b * nb + i, COL_CQ // w)),
            pl.BlockSpec((BLK, w), lambda b, i: (b * nb + i, COL_IQ // w)),
            pl.BlockSpec((p_rows, BLK), lambda b, i: (b, COL_CKIK // BLK)),
            pl.BlockSpec((nb, C_VROWS, BLK), lambda b, i: (b, 0, 0)),
            pl.BlockSpec((None, IDX_HEADS, BLK), lambda b, i: (b * nb + i, 0, 0)),
            pl.BlockSpec((3, BLK, C_LANES), lambda b, i: (0, 0, 0)),
        ],
        out_specs=pl.BlockSpec((BLK, w), lambda b, i: (b * nb + i, 0)),
        out_shape=jax.ShapeDtypeStruct((n, w), BF16),
        scratch_shapes=[
            pltpu.VMEM((nb + 1, BLK, BLK), F32),
            pltpu.VMEM((nb + 1, BLK, BLK), F32),
            pltpu.VMEM((BLK, C_LANES), BF16),
            pltpu.VMEM((BLK, C_LANES), BF16),
            pltpu.VMEM((C_VROWS, C_LANES), F32),
        ],
        name="mixer_c",
        compiler_params=pltpu.CompilerParams(dimension_semantics=("parallel", "arbitrary")),
    )(proj, proj, proj, cvt, iwt, bias)


def _merge_kernel(h_ref, g_ref, oa_ref, ob_ref, oc_ref, ga_ref, gb_ref, gc_ref, wbr_ref, wo_ref,
                  o_ref, xn_ref, acc_ref, *, nc):
    c = pl.program_id(1)

    @pl.when(c == 0)
    def _():
        xn_ref[...] = _rms(h_ref[...], g_ref[...]).astype(BF16)
        acc_ref[...] = jnp.zeros_like(acc_ref)

    xn = xn_ref[...]
    y = (jax.nn.sigmoid(_mm(xn, ga_ref[...])) * _mm(oa_ref[...], wbr_ref[0])
         + jax.nn.sigmoid(_mm(xn, gb_ref[...])) * _mm(ob_ref[...], wbr_ref[1])
         + jax.nn.sigmoid(_mm(xn, gc_ref[...])) * _mm(oc_ref[...], wbr_ref[2]))
    acc_ref[...] += _mm(y.astype(BF16), wo_ref[...])

    @pl.when(c == nc - 1)
    def _():
        o_ref[...] = h_ref[...] + acc_ref[...]


def _merge(h, gain, oa, ob, oc, wgates, wbr, wout, layer):
    n = h.shape[0]
    tc = TC_MERGE
    nc = D_MODEL // tc
    bw = 512
    row = lambda i, c: (i, 0)
    wgate = lambda br: pl.BlockSpec((None, D_MODEL, tc), lambda i, c: (layer, 0, br * nc + c))
    return pl.pallas_call(
        functools.partial(_merge_kernel, nc=nc),
        grid=(n // TM, nc),
        in_specs=[
            pl.BlockSpec((TM, D_MODEL), row),
            pl.BlockSpec((None, 1, D_MODEL), lambda i, c: (layer, 0, 0)),
            pl.BlockSpec((TM, bw), row), pl.BlockSpec((TM, bw), row), pl.BlockSpec((TM, bw), row),
            wgate(0), wgate(1), wgate(2),
            pl.BlockSpec((None, 3, bw, tc), lambda i, c: (layer, 0, 0, c)),
            pl.BlockSpec((None, tc, D_MODEL), lambda i, c: (layer, c, 0)),
        ],
        out_specs=pl.BlockSpec((TM, D_MODEL), row),
        out_shape=jax.ShapeDtypeStruct((n, D_MODEL), F32),
        scratch_shapes=[pltpu.VMEM((TM, D_MODEL), BF16), pltpu.VMEM((TM, D_MODEL), F32)],
        input_output_aliases={0: 0},
        name="branch_merge",
        compiler_params=pltpu.CompilerParams(
            dimension_semantics=("parallel", "arbitrary"), vmem_limit_bytes=VMEM_LIMIT),
    )(h, gain, oa, ob, oc, wgates, wgates, wgates, wbr, wout)


def _final_kernel(h_ref, g_ref, o_ref):
    o_ref[...] = _rms(h_ref[...], g_ref[...])


def _final_norm(h, gain, bsz, nb):
    seq = (nb - 1) * BLK
    tr = 512 if seq % 512 == 0 else BLK
    return pl.pallas_call(
        _final_kernel,
        grid=(bsz, seq // tr),
        in_specs=[
            pl.BlockSpec((pl.Element(tr), pl.Element(D_MODEL)),
                         lambda b, r: (pl.multiple_of((b * nb + 1) * BLK + r * tr, BLK), 0)),
            pl.BlockSpec((1, D_MODEL), lambda b, r: (0, 0)),
        ],
        out_specs=pl.BlockSpec((None, tr, D_MODEL), lambda b, r: (b, r, 0)),
        out_shape=jax.ShapeDtypeStruct((bsz, seq, D_MODEL), F32),
        name="final_norm",
        compiler_params=pltpu.CompilerParams(dimension_semantics=("parallel", "parallel")),
    )(h, gain)


def _bucket_np(d):
    max_exact = N_BUCKETS // 2
    d = np.maximum(d, 0)
    df = np.maximum(d, 1).astype(np.float32)
    large = max_exact + (np.log(df / max_exact) / math.log(MAX_DISTANCE / max_exact)
                         * (N_BUCKETS - max_exact)).astype(np.int32)
    return np.where(d < max_exact, d, np.minimum(large, N_BUCKETS - 1)).astype(np.int32)


def _bias_tiles(table):
    kl = np.arange(BLK)[:, None]
    ql = np.arange(BLK)[None, :]
    idx = np.stack([_bucket_np(ql - kl), _bucket_np(BLK + ql - kl),
                    np.full((BLK, BLK), N_BUCKETS - 1, np.int32)])
    onehot = (idx.reshape(-1, 1) == np.arange(N_BUCKETS)[None, :]).astype(np.float32)
    tiles = jnp.dot(onehot, table.astype(F32), precision=lax.Precision.HIGHEST)
    return jnp.transpose(tiles.reshape(3, BLK, BLK, -1), (3, 0, 1, 2))


def _pack_w_in(w_in):
    o = np.cumsum([0, 512, 128, 128, 256, 256, 256, 256, 512, 512, 64, 64, 512, 64, 8, 3 * D_MODEL])
    w_in = w_in.astype(BF16)
    (aq, ak, av, bq1, bq2, bk1, bk2, bv, cq, ck, cv, iq, ik, iw, gates) = [
        w_in[:, :, o[t]:o[t + 1]] for t in range(15)]
    pad = jnp.zeros(w_in.shape[:2] + (N_PROJ - COL_CKIK - BLK,), BF16)
    w = jnp.concatenate([aq, cq, iq, bq1, bq2, bk1, bk2, ak, ck, ik, pad], axis=-1)
    padt = jnp.zeros(w_in.shape[:2] + (N_PROJ_T - ROW_IW - IDX_HEADS,), BF16)
    wt = jnp.swapaxes(lax.optimization_barrier(jnp.concatenate([av, bv, cv, iw, padt], axis=-1)), 1, 2)
    return w, wt, gates


def kernel(x, meta_tokens, rel_bias_table, norm_ffn1, ffn1_w_gate, ffn1_w_up, ffn1_w_down, norm_mix, w_in,
           attn_sinks, diff_lambda, diff_subln, w_branch, w_out, norm_ffn2, ffn2_w_gate, ffn2_w_up,
           ffn2_w_down, norm_final):
    bsz, seq, _ = x.shape
    nb = seq // BLK + 1
    p_rows = nb * BLK
    n = bsz * p_rows
    assert seq % BLK == 0 and n % TM_FFN == 0
    topk = min(TOPK_MAX, seq // 4)

    lead = jnp.concatenate([jnp.zeros((N_PAD, D_MODEL), x.dtype), meta_tokens.astype(x.dtype)], axis=0)

    bias = _bias_tiles(rel_bias_table)
    bias_a = jnp.transpose(bias[:A_HEADS].reshape(A_KV_HEADS, A_GRP, 3, BLK, BLK),
                           (0, 2, 3, 1, 4)).reshape(A_KV_HEADS, 3, BLK, A_LANES)
    bias_b = jnp.transpose(bias[A_HEADS:A_HEADS + B_HEADS].reshape(B_GROUPS, 2, 3, BLK, BLK),
                           (0, 2, 3, 1, 4)).reshape(B_GROUPS, 3, BLK, 2 * BLK)
    bias_c = jnp.transpose(bias[A_HEADS + B_HEADS:], (1, 2, 0, 3)).reshape(3, BLK, C_LANES)
    w_proj, w_proj_t, w_gates = _pack_w_in(w_in)
    g1, gm, g2 = (t.reshape(DEPTH, 1, D_MODEL) for t in (norm_ffn1, norm_mix, norm_ffn2))
    f1 = [t.astype(BF16) for t in (ffn1_w_gate, ffn1_w_up, ffn1_w_down)]
    f2 = [t.astype(BF16) for t in (ffn2_w_gate, ffn2_w_up, ffn2_w_down)]
    wbr, wout = w_branch.astype(BF16), w_out.astype(BF16)
    sinks = jnp.broadcast_to(attn_sinks.astype(F32)[:, :, None],
                             (DEPTH, A_HEADS, BLK)).reshape(DEPTH, A_KV_HEADS, 1, A_LANES)
    subln = jnp.broadcast_to(diff_subln.astype(F32)[:, :, None], (DEPTH, B_V_DIM, BLK))
    lam_f = diff_lambda.astype(F32)

    for l in range(DEPTH):
        h = _ffn_first(x, lead, g1, *f1, nb) if l == 0 else _ffn(h, g1, *f1, l)
        proj, avt, bvt, cvt, iwt = _proj(h, gm, w_proj, w_proj_t, l)
        o_a = _mixer_a(proj, avt, bias_a, sinks[l], bsz, nb)
        lam_init = 0.8 - 0.6 * math.exp(-0.3 * l)
        o_b = _mixer_b(proj, bvt, bias_b, lam_f[l:l + 1], subln[l:l + 1], bsz, nb, lam_init)
        o_c = _mixer_c(proj, cvt, iwt, bias_c, bsz, nb, topk)
        h = _merge(h, gm, o_a, o_b, o_c, w_gates, wbr, wout, l)
        h = _ffn(h, g2, *f2, l)

    return _final_norm(h, norm_final.reshape(1, D_MODEL), bsz, nb)
```

```python
import functools
import math

import jax
import jax.numpy as jnp
import numpy as np
from jax import lax
from jax.experimental import pallas as pl
from jax.experimental.pallas import tpu as pltpu

F32 = jnp.float32
BF16 = jnp.bfloat16

D_MODEL = 2048
D_FF = 5632
DEPTH = 2
BLK = 128
N_META = 16
N_PAD = BLK - N_META
WINDOW = 128
assert WINDOW == BLK
HEAD_DIM = 64
A_HEADS, A_KV_HEADS = 8, 2
B_HEADS, B_QK_DIM, B_V_DIM = 4, 64, 128
C_HEADS = 8
IDX_HEADS, IDX_DIM = 8, 64
TOPK_MAX = 256
N_BUCKETS, MAX_DISTANCE = 32, 128
EPS = 1e-6
NEG = -1e30

COL_AQ, COL_CQ, COL_IQ = 0, 512, 1024
COL_BQ1, COL_BQ2, COL_BK1, COL_BK2 = 1536, 1792, 2048, 2304
COL_AK, COL_CKIK = 2560, 2688
N_PROJ = 3072
ROW_AV, ROW_BV, ROW_CV, ROW_IW, N_PROJ_T = 0, 128, 640, 704, 720

V_ONES = 16
A_VROWS, B_VROWS, C_VROWS = HEAD_DIM + V_ONES, B_V_DIM + V_ONES, HEAD_DIM + V_ONES

TM = 512
TM_FFN = 1024
TM_PROJ = 1024
TF = 512
FFN_VMEM_LIMIT = 62 * 1024 * 1024
TN = 1024
TC_MERGE = 512
assert HEAD_DIM == B_QK_DIM == IDX_DIM == 64
QK_SCALE = 0.125
VMEM_LIMIT = 56 * 1024 * 1024


def _nt(a, b):
    return lax.dot_general(a, b, (((1,), (1,)), ((), ())), preferred_element_type=F32)


def _mm(a, b):
    return jnp.dot(a, b, preferred_element_type=F32)


def _rms(x, g):
    ms = jnp.mean(x * x, axis=-1, keepdims=True)
    return x * lax.rsqrt(ms + EPS) * g


def _ffn_accumulate(xn_ref, wg_ref, wu_ref, wd_ref, o_ref):
    for r in range(TM_FFN // TM):
        rows = slice(r * TM, (r + 1) * TM)
        xn = xn_ref[rows, :]
        a = _mm(xn, wg_ref[...])
        b = _mm(xn, wu_ref[...])
        t = (a * jax.nn.sigmoid(a)) * b
        o_ref[rows, :] += 0.5 * _mm(t.astype(BF16), wd_ref[...])


def _ffn_kernel(h_ref, g_ref, wg_ref, wu_ref, wd_ref, o_ref, xn_ref):
    @pl.when(pl.program_id(1) == 0)
    def _():
        h = h_ref[...]
        xn_ref[...] = _rms(h, g_ref[...]).astype(BF16)
        o_ref[...] = h

    _ffn_accumulate(xn_ref, wg_ref, wu_ref, wd_ref, o_ref)


def _ffn_first_kernel(*refs, nb):
    nblk = TM_FFN // BLK
    x_refs = refs[:nblk]
    lead_ref, g_ref, wg_ref, wu_ref, wd_ref, o_ref, xn_ref = refs[nblk:]

    @pl.when(pl.program_id(1) == 0)
    def _():
        first = pl.program_id(0) * nblk
        for t in range(nblk):
            rows = slice(t * BLK, (t + 1) * BLK)
            o_ref[rows, :] = x_refs[t][...]

            @pl.when(lax.rem(first + t, nb) == 0)
            def _():
                o_ref[rows, :] = lead_ref[...]

        xn_ref[...] = _rms(o_ref[...], g_ref[...]).astype(BF16)

    _ffn_accumulate(xn_ref, wg_ref, wu_ref, wd_ref, o_ref)


def _ffn_weight_specs(layer):
    return [
        pl.BlockSpec((None, 1, D_MODEL), lambda i, k: (layer, 0, 0)),
        pl.BlockSpec((None, D_MODEL, TF), lambda i, k: (layer, 0, k)),
        pl.BlockSpec((None, D_MODEL, TF), lambda i, k: (layer, 0, k)),
        pl.BlockSpec((None, TF, D_MODEL), lambda i, k: (layer, k, 0)),
    ]


def _ffn(h, gain, wg, wu, wd, layer):
    n = h.shape[0]
    return pl.pallas_call(
        _ffn_kernel,
        grid=(n // TM_FFN, D_FF // TF),
        in_specs=[pl.BlockSpec((TM_FFN, D_MODEL), lambda i, k: (i, 0))] + _ffn_weight_specs(layer),
        out_specs=pl.BlockSpec((TM_FFN, D_MODEL), lambda i, k: (i, 0)),
        out_shape=jax.ShapeDtypeStruct((n, D_MODEL), F32),
        scratch_shapes=[pltpu.VMEM((TM_FFN, D_MODEL), BF16)],
        input_output_aliases={0: 0},
        name="ffn",
        compiler_params=pltpu.CompilerParams(
            dimension_semantics=("parallel", "arbitrary"), vmem_limit_bytes=FFN_VMEM_LIMIT),
    )(h, gain, wg, wu, wd)


def _ffn_first(x, lead, gain, wg, wu, wd, nb):
    bsz = x.shape[0]
    n = bsz * nb * BLK
    nblk = TM_FFN // BLK

    def x_spec(t):
        return pl.BlockSpec((None, BLK, D_MODEL),
                            lambda i, k: ((i * nblk + t) // nb, jnp.maximum((i * nblk + t) % nb - 1, 0), 0))

    return pl.pallas_call(
        functools.partial(_ffn_first_kernel, nb=nb),
        grid=(n // TM_FFN, D_FF // TF),
        in_specs=[x_spec(t) for t in range(nblk)]
        + [pl.BlockSpec((BLK, D_MODEL), lambda i, k: (0, 0))] + _ffn_weight_specs(0),
        out_specs=pl.BlockSpec((TM_FFN, D_MODEL), lambda i, k: (i, 0)),
        out_shape=jax.ShapeDtypeStruct((n, D_MODEL), F32),
        scratch_shapes=[pltpu.VMEM((TM_FFN, D_MODEL), BF16)],
        name="ffn_first",
        compiler_params=pltpu.CompilerParams(
            dimension_semantics=("parallel", "arbitrary"), vmem_limit_bytes=FFN_VMEM_LIMIT),
    )(*([x] * nblk), lead, gain, wg, wu, wd)


def _proj_kernel(h_ref, g_ref, w_ref, wt_ref, o_ref, avt_ref, bvt_ref, cvt_ref, iwt_ref, xn_ref):
    j = pl.program_id(1)

    @pl.when(j == 0)
    def _():
        xn = _rms(h_ref[...], g_ref[...]).astype(BF16)
        xn_ref[...] = xn
        vt = _nt(wt_ref[...], xn)
        for t in range(TM_PROJ // BLK):
            cols = slice(t * BLK, (t + 1) * BLK)
            ones = jnp.ones((V_ONES, BLK), BF16)
            for g in range(A_KV_HEADS):
                avt_ref[t, g * A_VROWS:g * A_VROWS + HEAD_DIM] = (
                    vt[ROW_AV + g * HEAD_DIM:ROW_AV + (g + 1) * HEAD_DIM, cols].astype(BF16))
                avt_ref[t, g * A_VROWS + HEAD_DIM:(g + 1) * A_VROWS] = ones
            for h in range(B_HEADS):
                bvt_ref[t, h * B_VROWS:h * B_VROWS + B_V_DIM] = (
                    vt[ROW_BV + h * B_V_DIM:ROW_BV + (h + 1) * B_V_DIM, cols].astype(BF16))
                bvt_ref[t, h * B_VROWS + B_V_DIM:(h + 1) * B_VROWS] = ones
            cvt_ref[t, 0:HEAD_DIM] = vt[ROW_CV:ROW_IW, cols].astype(BF16)
            cvt_ref[t, HEAD_DIM:C_VROWS] = ones
            iwt_ref[t] = vt[ROW_IW:ROW_IW + IDX_HEADS, cols]

    o_ref[...] = _mm(xn_ref[...], w_ref[...]).astype(BF16)


def _proj(h, gain, w, wt, layer):
    n = h.shape[0]
    nt = n // BLK
    tpb = TM_PROJ // BLK
    return pl.pallas_call(
        _proj_kernel,
        grid=(n // TM_PROJ, N_PROJ // TN),
        in_specs=[
            pl.BlockSpec((TM_PROJ, D_MODEL), lambda i, j: (i, 0)),
            pl.BlockSpec((None, 1, D_MODEL), lambda i, j: (layer, 0, 0)),
            pl.BlockSpec((None, D_MODEL, TN), lambda i, j: (layer, 0, j)),
            pl.BlockSpec((None, N_PROJ_T, D_MODEL), lambda i, j: (layer, 0, 0)),
        ],
        out_specs=[
            pl.BlockSpec((TM_PROJ, TN), lambda i, j: (i, j)),
            pl.BlockSpec((tpb, A_KV_HEADS * A_VROWS, BLK), lambda i, j: (i, 0, 0)),
            pl.BlockSpec((tpb, B_HEADS * B_VROWS, BLK), lambda i, j: (i, 0, 0)),
            pl.BlockSpec((tpb, C_VROWS, BLK), lambda i, j: (i, 0, 0)),
            pl.BlockSpec((tpb, IDX_HEADS, BLK), lambda i, j: (i, 0, 0)),
        ],
        out_shape=[
            jax.ShapeDtypeStruct((n, N_PROJ), BF16),
            jax.ShapeDtypeStruct((nt, A_KV_HEADS * A_VROWS, BLK), BF16),
            jax.ShapeDtypeStruct((nt, B_HEADS * B_VROWS, BLK), BF16),
            jax.ShapeDtypeStruct((nt, C_VROWS, BLK), BF16),
            jax.ShapeDtypeStruct((nt, IDX_HEADS, BLK), F32),
        ],
        scratch_shapes=[pltpu.VMEM((TM_PROJ, D_MODEL), BF16)],
        name="mixer_proj",
        compiler_params=pltpu.CompilerParams(
            dimension_semantics=("parallel", "arbitrary"), vmem_limit_bytes=VMEM_LIMIT),
    )(h, gain, w, wt)


def _tile_iotas():
    kl = lax.broadcasted_iota(jnp.int32, (BLK, BLK), 0)
    ql = lax.broadcasted_iota(jnp.int32, (BLK, BLK), 1)
    return kl, ql


def _colmax(x):
    return jnp.max(x, axis=0, keepdims=True)


def _colsum(x):
    return jnp.sum(x, axis=0, keepdims=True)


def _exp_bf16(x):
    return jnp.exp(x.astype(BF16))


def _pair_tiles(p, nb):
    ja = 2 * p
    return ((ja, ja), (ja + 1, jnp.minimum(ja + 1, nb - 1)))


A_GRP = A_HEADS // A_KV_HEADS
A_LANES = A_GRP * BLK


def _mixer_a_kernel(q_ref, k0_ref, k1_ref, k2_ref, v0_ref, v1_ref, v2_ref, bias_ref, sink_ref, o_ref, qst_ref):
    i = pl.program_id(1)
    kl = lax.broadcasted_iota(jnp.int32, (BLK, A_LANES), 0)
    ql = lax.broadcasted_iota(jnp.int32, (BLK, A_LANES), 1) & (BLK - 1)
    qpos = BLK * i - N_PAD + ql
    ok_meta = (kl >= N_PAD) & (qpos >= kl - N_PAD)
    ok_prev = (BLK * (i - 1) - N_PAD + kl >= N_META) & (ql < kl)
    ok_cur = (BLK * i - N_PAD + kl >= N_META) & (ql >= kl)
    t_meta = jnp.minimum(i, 2)
    qst_ref[...] = jnp.zeros_like(qst_ref)
    for h in range(A_HEADS):
        g, r = divmod(h, A_GRP)
        qst_ref[g, r * BLK:(r + 1) * BLK, g * HEAD_DIM:(g + 1) * HEAD_DIM] = (
            q_ref[:, h * HEAD_DIM:(h + 1) * HEAD_DIM] * QK_SCALE)
    outs = []
    for g in range(A_KV_HEADS):
        qs = qst_ref[g]
        s0 = jnp.where(ok_meta, _nt(k0_ref[...], qs) + bias_ref[g, t_meta], NEG)
        s1 = jnp.where(ok_prev, _nt(k1_ref[...], qs) + bias_ref[g, 1], NEG)
        s2 = jnp.where(ok_cur, _nt(k2_ref[...], qs) + bias_ref[g, 0], NEG)
        sink = sink_ref[g]
        m = jnp.maximum(jnp.maximum(_colmax(s0), _colmax(s1)), jnp.maximum(_colmax(s2), sink))
        vs = slice(g * A_VROWS, (g + 1) * A_VROWS)
        ot = (_mm(v0_ref[vs, :], _exp_bf16(s0 - m)) + _mm(v1_ref[vs, :], _exp_bf16(s1 - m))
              + _mm(v2_ref[vs, :], _exp_bf16(s2 - m)))
        ot = ot[:HEAD_DIM] / (ot[HEAD_DIM:HEAD_DIM + 1] + jnp.exp(sink - m))
        outs.extend(ot[:, r * BLK:(r + 1) * BLK] for r in range(A_GRP))
    o_ref[...] = jnp.concatenate(outs, axis=0).T.astype(BF16)


def _mixer_a(proj, avt, bias, sinks, bsz, nb):
    n = proj.shape[0]
    kcol = COL_AK // BLK
    rowq = lambda b, i: (b * nb + i, COL_AQ // (A_HEADS * HEAD_DIM))
    spec_k = lambda f: pl.BlockSpec((BLK, BLK), lambda b, i: (b * nb + f(i), kcol))
    spec_v = lambda f: pl.BlockSpec((None, A_KV_HEADS * A_VROWS, BLK), lambda b, i: (b * nb + f(i), 0, 0))
    first = lambda i: 0
    prev = lambda i: jnp.maximum(i - 1, 0)
    cur = lambda i: i
    return pl.pallas_call(
        _mixer_a_kernel,
        grid=(bsz, nb),
        in_specs=[
            pl.BlockSpec((BLK, A_HEADS * HEAD_DIM), rowq),
            spec_k(first), spec_k(prev), spec_k(cur),
            spec_v(first), spec_v(prev), spec_v(cur),
            pl.BlockSpec((A_KV_HEADS, 3, BLK, A_LANES), lambda b, i: (0, 0, 0, 0)),
            pl.BlockSpec((A_KV_HEADS, 1, A_LANES), lambda b, i: (0, 0, 0)),
        ],
        out_specs=pl.BlockSpec((BLK, A_HEADS * HEAD_DIM), lambda b, i: (b * nb + i, 0)),
        out_shape=jax.ShapeDtypeStruct((n, A_HEADS * HEAD_DIM), BF16),
        scratch_shapes=[pltpu.VMEM((A_KV_HEADS, A_LANES, BLK), BF16)],
        name="mixer_a",
        compiler_params=pltpu.CompilerParams(dimension_semantics=("parallel", "parallel")),
    )(proj, proj, proj, proj, avt, avt, avt, bias, sinks)


B_GROUPS = B_HEADS // 2


def _mixer_b_kernel(q1_ref, q2_ref, k1_ref, k2_ref, vt_ref, bias_ref, lam_ref, subln_ref, o_ref,
                    qbd_ref, acc_ref, *, lam_init, nb):
    i = pl.program_id(1)
    npair = (i + 2) // 2
    w2 = 2 * BLK
    kl = lax.broadcasted_iota(jnp.int32, (BLK, w2), 0)
    qpos = BLK * i - N_PAD + (lax.broadcasted_iota(jnp.int32, (BLK, w2), 1) & (BLK - 1))

    feat = lax.broadcasted_iota(jnp.int32, (BLK, BLK), 0)
    for mp, q_ref in enumerate((q1_ref, q2_ref)):
        for g in range(B_GROUPS):
            qt = (q_ref[:, g * BLK:(g + 1) * BLK].astype(F32) * QK_SCALE).T
            qbd_ref[mp, g] = jnp.concatenate(
                [jnp.where(feat < B_QK_DIM, qt, 0.0), jnp.where(feat >= B_QK_DIM, qt, 0.0)], axis=1).astype(BF16)
    acc_ref[...] = jnp.zeros_like(acc_ref)

    def pair(p, ms):
        tiles = _pair_tiles(p, nb)
        oks, ts, rows = [], [], []
        for jl, jk in tiles:
            kpos = BLK * jl - N_PAD + kl
            oks.append((kpos >= 0) & (qpos >= kpos))
            ts.append(jnp.clip(i - jl, 0, 2))
            rows.append(pl.ds(pl.multiple_of(jk * BLK, BLK), BLK))
        ok2 = jnp.concatenate(oks, axis=0)
        raw = [[_mm(jnp.concatenate([k_ref[rows[tl], g * BLK:(g + 1) * BLK] for tl in range(2)], axis=0),
                    qbd_ref[mp, g]) for g in range(B_GROUPS)] for mp, k_ref in enumerate((k1_ref, k2_ref))]
        bias2 = [jnp.concatenate([bias_ref[g, ts[tl]] for tl in range(2)], axis=0) for g in range(B_GROUPS)]
        new_ms = []
        ps = [[None] * B_GROUPS for _ in range(2)]
        alphas = [[None] * B_GROUPS for _ in range(2)]
        for mp in range(2):
            row_m = []
            for g in range(B_GROUPS):
                s = jnp.where(ok2, raw[mp][g] + bias2[g], NEG)
                m_old = ms[mp][g]
                m_new = jnp.maximum(m_old, _colmax(s))
                row_m.append(m_new)
                ps[mp][g] = _exp_bf16(s - m_new)
                alphas[mp][g] = jnp.exp(m_old - m_new)
            new_ms.append(row_m)
        for h in range(B_HEADS):
            g, hs = h // 2, slice((h % 2) * BLK, (h % 2 + 1) * BLK)
            alpha = jnp.concatenate([alphas[0][g][:, hs], alphas[1][g][:, hs]], axis=1)
            p12 = jnp.concatenate([ps[0][g][:, hs], ps[1][g][:, hs]], axis=1)
            vt2 = jnp.concatenate([vt_ref[jk, h * B_VROWS:(h + 1) * B_VROWS, :] for _, jk in tiles], axis=1)
            acc_ref[h] = alpha * acc_ref[h] + _mm(vt2, p12)
        return new_ms

    lax.fori_loop(0, npair, pair, [[jnp.full((1, w2), NEG, F32) for _ in range(B_GROUPS)] for _ in range(2)])

    lf = lam_ref[...]
    lam = (jnp.exp(jnp.sum(lf[0:1] * lf[1:2], axis=-1, keepdims=True))
           - jnp.exp(jnp.sum(lf[2:3] * lf[3:4], axis=-1, keepdims=True)) + lam_init)
    outs = []
    for h in range(B_HEADS):
        acc = acc_ref[h]
        num, den = acc[:B_V_DIM], acc[B_V_DIM:B_V_DIM + 1]
        o = num[:, :BLK] / den[:, :BLK] - lam * (num[:, BLK:] / den[:, BLK:])
        ms_ = jnp.mean(o * o, axis=0, keepdims=True)
        outs.append(o * lax.rsqrt(ms_ + EPS) * subln_ref[...] * (1.0 - lam_init))
    o_ref[...] = jnp.concatenate(outs, axis=0).T.astype(BF16)


def _mixer_b(proj, bvt, bias, lam, subln, bsz, nb, lam_init):
    n = proj.shape[0]
    p_rows = nb * BLK
    w = B_HEADS * B_QK_DIM
    rowq = lambda c: (lambda b, i: (b * nb + i, c))
    return pl.pallas_call(
        functools.partial(_mixer_b_kernel, lam_init=lam_init, nb=nb),
        grid=(bsz, nb),
        in_specs=[
            pl.BlockSpec((BLK, w), rowq(COL_BQ1 // w)),
            pl.BlockSpec((BLK, w), rowq(COL_BQ2 // w)),
            pl.BlockSpec((p_rows, w), lambda b, i: (b, COL_BK1 // w)),
            pl.BlockSpec((p_rows, w), lambda b, i: (b, COL_BK2 // w)),
            pl.BlockSpec((nb, B_HEADS * B_VROWS, BLK), lambda b, i: (b, 0, 0)),
            pl.BlockSpec((B_GROUPS, 3, BLK, 2 * BLK), lambda b, i: (0, 0, 0, 0)),
            pl.BlockSpec((None, 4, B_QK_DIM), lambda b, i: (0, 0, 0)),
            pl.BlockSpec((None, B_V_DIM, BLK), lambda b, i: (0, 0, 0)),
        ],
        out_specs=pl.BlockSpec((BLK, B_HEADS * B_V_DIM), lambda b, i: (b * nb + i, 0)),
        out_shape=jax.ShapeDtypeStruct((n, B_HEADS * B_V_DIM), BF16),
        scratch_shapes=[
            pltpu.VMEM((2, B_GROUPS, BLK, 2 * BLK), BF16),
            pltpu.VMEM((B_HEADS, B_VROWS, 2 * BLK), F32),
        ],
        name="mixer_b",
        compiler_params=pltpu.CompilerParams(dimension_semantics=("parallel", "arbitrary")),
    )(proj, proj, proj, proj, bvt, bias, lam, subln)


C_LANES = C_HEADS * BLK
C_CHUNK = 2 * BLK


def _ordered_bits_to_float(u):
    bits = jnp.where(u < 0, u ^ jnp.int32(-2 ** 31), ~u)
    return lax.bitcast_convert_type(bits, F32)


def _mixer_c_kernel(cq_ref, iq_ref, kk_ref, vt_ref, iwt_ref, bias_ref, o_ref,
                    score_ref, sel_ref, qst_ref, iqst_ref, acc_ref, *, topk, nb):
    i = pl.program_id(1)
    npair = (i + 2) // 2
    kl, ql = _tile_iotas()
    qpos = BLK * i - N_PAD + ql
    kf = jnp.float32(topk)

    qst_ref[...] = jnp.zeros_like(qst_ref)
    iqst_ref[...] = jnp.zeros_like(iqst_ref)
    for g in range(C_HEADS // 2):
        cols = slice(g * BLK, (g + 1) * BLK)
        qt = (cq_ref[:, cols].astype(F32) * QK_SCALE).T.astype(BF16)
        it = iq_ref[:, cols].astype(F32).T.astype(BF16)
        for hh in range(2):
            lanes = slice((2 * g + hh) * BLK, (2 * g + hh + 1) * BLK)
            qst_ref[0:HEAD_DIM, lanes] = qt[hh * HEAD_DIM:(hh + 1) * HEAD_DIM, :]
            iqst_ref[HEAD_DIM:BLK, lanes] = it[hh * IDX_DIM:(hh + 1) * IDX_DIM, :]
    iw = iwt_ref[...] * QK_SCALE
    iw_all = jnp.concatenate([iw[h:h + 1, :] for h in range(IDX_HEADS)], axis=1)

    def admissible(jl):
        kpos = BLK * jl - N_PAD + kl
        return (kpos >= 0) & (qpos >= kpos)

    def score_pair(p, carry):
        tiles = _pair_tiles(p, nb)
        kk2 = jnp.concatenate([kk_ref[pl.ds(pl.multiple_of(jk * BLK, BLK), BLK), :] for _, jk in tiles], axis=0)
        sc = None
        for c in range(C_LANES // C_CHUNK):
            ln = slice(c * C_CHUNK, (c + 1) * C_CHUNK)
            rel = jnp.maximum(_mm(kk2, iqst_ref[:, ln]), 0.0) * iw_all[:, ln]
            for h in range(C_CHUNK // BLK):
                part = rel[:, h * BLK:(h + 1) * BLK]
                sc = part if sc is None else sc + part
        for t, (jl, _) in enumerate(tiles):
            score_ref[jl] = jnp.where(admissible(jl), sc[t * BLK:(t + 1) * BLK] * IDX_HEADS ** -0.5, NEG)
        return carry

    lax.fori_loop(0, npair, score_pair, 0)

    def count(pred):
        def body(p, cs):
            xa = jnp.where(pred(score_ref[2 * p]), 1.0, 0.0)
            xb = jnp.where(pred(score_ref[2 * p + 1]), 1.0, 0.0)
            return [c + (xa[r * 8:(r + 1) * 8, :] + xb[r * 8:(r + 1) * 8, :]) for r, c in enumerate(cs)]
        parts = lax.fori_loop(0, npair, body, [jnp.zeros((8, BLK), F32)] * (BLK // 8))
        while len(parts) > 1:
            parts = [parts[t] + parts[t + 1] for t in range(0, len(parts), 2)]
        return _colsum(parts[0])

    def search(it, u):
        cand = u | lax.shift_left(jnp.int32(1), 31 - it)
        thr_c = _ordered_bits_to_float(cand)
        return jnp.where(count(lambda s: s >= thr_c) >= kf, cand, u)

    u = lax.fori_loop(0, 32, search, jnp.zeros((1, BLK), jnp.int32))
    found = (u < 0) | (u >= jnp.int32(0x00800000))
    thr = jnp.where(found, _ordered_bits_to_float(u), -jnp.inf)
    need = kf - count(lambda s: s > thr)
    n_ge = count(lambda s: s >= thr)
    has_tie = jnp.max(jnp.where((n_ge > kf) & (thr > NEG), 1.0, 0.0)) > 0.5

    def select_plain():
        def body(p, carry):
            for jl in (2 * p, 2 * p + 1):
                sel_ref[jl] = jnp.where(admissible(jl) & (score_ref[jl] >= thr), 0.0, NEG)
            return carry
        lax.fori_loop(0, npair, body, 0)

    def select_ties():
        tri = (kl >= ql).astype(BF16)

        def body(jl, seen):
            sc = score_ref[jl]
            eq = sc == thr
            eqf = jnp.where(eq, 1.0, 0.0)
            rank = seen + _mm(tri, eqf.astype(BF16))
            sel = admissible(jl) & ((sc > thr) | (eq & (rank <= need)))
            sel_ref[jl] = jnp.where(sel, 0.0, NEG)
            return seen + _colsum(eqf)
        lax.fori_loop(0, 2 * npair, body, jnp.zeros((1, BLK), F32))

    lax.cond(has_tie, select_ties, select_plain)

    acc_ref[...] = jnp.zeros_like(acc_ref)

    def attend_pair(p, m_old):
        tiles = _pair_tiles(p, nb)
        kk2 = jnp.concatenate([kk_ref[pl.ds(pl.multiple_of(jk * BLK, BLK), BLK), :] for _, jk in tiles], axis=0)
        vt2 = jnp.concatenate([vt_ref[jk] for _, jk in tiles], axis=1)
        extra = jnp.concatenate([bias_ref[jnp.clip(i - jl, 0, 2)] + jnp.tile(sel_ref[jl], (1, C_HEADS))
                                 for jl, _ in tiles], axis=0)
        s = _mm(kk2, qst_ref[...]) + extra
        m_new = jnp.maximum(m_old, _colmax(s))
        acc_ref[...] = jnp.exp(m_old - m_new) * acc_ref[...] + _mm(vt2, _exp_bf16(s - m_new))
        return m_new

    lax.fori_loop(0, npair, attend_pair, jnp.full((1, C_LANES), NEG, F32))
    o = acc_ref[0:HEAD_DIM, :] / acc_ref[HEAD_DIM:HEAD_DIM + 1, :]
    o_rows = jnp.concatenate([o[:, h * BLK:(h + 1) * BLK] for h in range(C_HEADS)], axis=0)
    o_ref[...] = o_rows.T.astype(BF16)


def _mixer_c(proj, cvt, iwt, bias, bsz, nb, topk):
    n = proj.shape[0]
    p_rows = nb * BLK
    w = C_HEADS * HEAD_DIM
    return pl.pallas_call(
        functools.partial(_mixer_c_kernel, topk=topk, nb=nb),
        grid=(bsz, nb),
        in_specs=[
            pl.BlockSpec((BLK, w), lambda b, i: (b * nb + i, COL_CQ // w)),
            pl.BlockSpec((BLK, w), lambda b, i: (b * nb + i, COL_IQ // w)),
            pl.BlockSpec((p_rows, BLK), lambda b, i: (b, COL_CKIK // BLK)),
            pl.BlockSpec((nb, C_VROWS, BLK), lambda b, i: (b, 0, 0)),
            pl.BlockSpec((None, IDX_HEADS, BLK), lambda b, i: (b * nb + i, 0, 0)),
            pl.BlockSpec((3, BLK, C_LANES), lambda b, i: (0, 0, 0)),
        ],
        out_specs=pl.BlockSpec((BLK, w), lambda b, i: (b * nb + i, 0)),
        out_shape=jax.ShapeDtypeStruct((n, w), BF16),
        scratch_shapes=[
            pltpu.VMEM((nb + 1, BLK, BLK), F32),
            pltpu.VMEM((nb + 1, BLK, BLK), F32),
            pltpu.VMEM((BLK, C_LANES), BF16),
            pltpu.VMEM((BLK, C_LANES), BF16),
            pltpu.VMEM((C_VROWS, C_LANES), F32),
        ],
        name="mixer_c",
        compiler_params=pltpu.CompilerParams(dimension_semantics=("parallel", "arbitrary")),
    )(proj, proj, proj, cvt, iwt, bias)


def _merge_kernel(h_ref, g_ref, oa_ref, ob_ref, oc_ref, ga_ref, gb_ref, gc_ref, wbr_ref, wo_ref,
                  o_ref, xn_ref, acc_ref, *, nc):
    c = pl.program_id(1)

    @pl.when(c == 0)
    def _():
        xn_ref[...] = _rms(h_ref[...], g_ref[...]).astype(BF16)
        acc_ref[...] = jnp.zeros_like(acc_ref)

    xn = xn_ref[...]
    y = (jax.nn.sigmoid(_mm(xn, ga_ref[...])) * _mm(oa_ref[...], wbr_ref[0])
         + jax.nn.sigmoid(_mm(xn, gb_ref[...])) * _mm(ob_ref[...], wbr_ref[1])
         + jax.nn.sigmoid(_mm(xn, gc_ref[...])) * _mm(oc_ref[...], wbr_ref[2]))
    acc_ref[...] += _mm(y.astype(BF16), wo_ref[...])

    @pl.when(c == nc - 1)
    def _():
        o_ref[...] = h_ref[...] + acc_ref[...]


def _merge(h, gain, oa, ob, oc, wgates, wbr, wout, layer):
    n = h.shape[0]
    tc = TC_MERGE
    nc = D_MODEL // tc
    bw = 512
    row = lambda i, c: (i, 0)
    wgate = lambda br: pl.BlockSpec((None, D_MODEL, tc), lambda i, c: (layer, 0, br * nc + c))
    return pl.pallas_call(
        functools.partial(_merge_kernel, nc=nc),
        grid=(n // TM, nc),
        in_specs=[
            pl.BlockSpec((TM, D_MODEL), row),
            pl.BlockSpec((None, 1, D_MODEL), lambda i, c: (layer, 0, 0)),
            pl.BlockSpec((TM, bw), row), pl.BlockSpec((TM, bw), row), pl.BlockSpec((TM, bw), row),
            wgate(0), wgate(1), wgate(2),
            pl.BlockSpec((None, 3, bw, tc), lambda i, c: (layer, 0, 0, c)),
            pl.BlockSpec((None, tc, D_MODEL), lambda i, c: (layer, c, 0)),
        ],
        out_specs=pl.BlockSpec((TM, D_MODEL), row),
        out_shape=jax.ShapeDtypeStruct((n, D_MODEL), F32),
        scratch_shapes=[pltpu.VMEM((TM, D_MODEL), BF16), pltpu.VMEM((TM, D_MODEL), F32)],
        input_output_aliases={0: 0},
        name="branch_merge",
        compiler_params=pltpu.CompilerParams(
            dimension_semantics=("parallel", "arbitrary"), vmem_limit_bytes=VMEM_LIMIT),
    )(h, gain, oa, ob, oc, wgates, wgates, wgates, wbr, wout)


def _final_kernel(h_ref, g_ref, o_ref):
    o_ref[...] = _rms(h_ref[...], g_ref[...])


def _final_norm(h, gain, bsz, nb):
    seq = (nb - 1) * BLK
    tr = 512 if seq % 512 == 0 else BLK
    return pl.pallas_call(
        _final_kernel,
        grid=(bsz, seq // tr),
        in_specs=[
            pl.BlockSpec((pl.Element(tr), pl.Element(D_MODEL)),
                         lambda b, r: (pl.multiple_of((b * nb + 1) * BLK + r * tr, BLK), 0)),
            pl.BlockSpec((1, D_MODEL), lambda b, r: (0, 0)),
        ],
        out_specs=pl.BlockSpec((None, tr, D_MODEL), lambda b, r: (b, r, 0)),
        out_shape=jax.ShapeDtypeStruct((bsz, seq, D_MODEL), F32),
        name="final_norm",
        compiler_params=pltpu.CompilerParams(dimension_semantics=("parallel", "parallel")),
    )(h, gain)


def _bucket_np(d):
    max_exact = N_BUCKETS // 2
    d = np.maximum(d, 0)
    df = np.maximum(d, 1).astype(np.float32)
    large = max_exact + (np.log(df / max_exact) / math.log(MAX_DISTANCE / max_exact)
                         * (N_BUCKETS - max_exact)).astype(np.int32)
    return np.where(d < max_exact, d, np.minimum(large, N_BUCKETS - 1)).astype(np.int32)


def _bias_tiles(table):
    kl = np.arange(BLK)[:, None]
    ql = np.arange(BLK)[None, :]
    idx = np.stack([_bucket_np(ql - kl), _bucket_np(BLK + ql - kl),
                    np.full((BLK, BLK), N_BUCKETS - 1, np.int32)])
    onehot = (idx.reshape(-1, 1) == np.arange(N_BUCKETS)[None, :]).astype(np.float32)
    tiles = jnp.dot(onehot, table.astype(F32), precision=lax.Precision.HIGHEST)
    return jnp.transpose(tiles.reshape(3, BLK, BLK, -1), (3, 0, 1, 2))


def _pack_w_in(w_in):
    o = np.cumsum([0, 512, 128, 128, 256, 256, 256, 256, 512, 512, 64, 64, 512, 64, 8, 3 * D_MODEL])
    w_in = w_in.astype(BF16)
    (aq, ak, av, bq1, bq2, bk1, bk2, bv, cq, ck, cv, iq, ik, iw, gates) = [
        w_in[:, :, o[t]:o[t + 1]] for t in range(15)]
    pad = jnp.zeros(w_in.shape[:2] + (N_PROJ - COL_CKIK - BLK,), BF16)
    w = jnp.concatenate([aq, cq, iq, bq1, bq2, bk1, bk2, ak, ck, ik, pad], axis=-1)
    padt = jnp.zeros(w_in.shape[:2] + (N_PROJ_T - ROW_IW - IDX_HEADS,), BF16)
    wt = jnp.swapaxes(lax.optimization_barrier(jnp.concatenate([av, bv, cv, iw, padt], axis=-1)), 1, 2)
    return w, wt, gates


def kernel(x, meta_tokens, rel_bias_table, norm_ffn1, ffn1_w_gate, ffn1_w_up, ffn1_w_down, norm_mix, w_in,
           attn_sinks, diff_lambda, diff_subln, w_branch, w_out, norm_ffn2, ffn2_w_gate, ffn2_w_up,
           ffn2_w_down, norm_final):
    bsz, seq, _ = x.shape
    nb = seq // BLK + 1
    p_rows = nb * BLK
    n = bsz * p_rows
    assert seq % BLK == 0 and n % TM_FFN == 0
    topk = min(TOPK_MAX, seq // 4)

    lead = jnp.concatenate([jnp.zeros((N_PAD, D_MODEL), x.dtype), meta_tokens.astype(x.dtype)], axis=0)

    bias = _bias_tiles(rel_bias_table)
    bias_a = jnp.transpose(bias[:A_HEADS].reshape(A_KV_HEADS, A_GRP, 3, BLK, BLK),
                           (0, 2, 3, 1, 4)).reshape(A_KV_HEADS, 3, BLK, A_LANES)
    bias_b = jnp.transpose(bias[A_HEADS:A_HEADS + B_HEADS].reshape(B_GROUPS, 2, 3, BLK, BLK),
                           (0, 2, 3, 1, 4)).reshape(B_GROUPS, 3, BLK, 2 * BLK)
    bias_c = jnp.transpose(bias[A_HEADS + B_HEADS:], (1, 2, 0, 3)).reshape(3, BLK, C_LANES)
    w_proj, w_proj_t, w_gates = _pack_w_in(w_in)
    g1, gm, g2 = (t.reshape(DEPTH, 1, D_MODEL) for t in (norm_ffn1, norm_mix, norm_ffn2))
    f1 = [t.astype(BF16) for t in (ffn1_w_gate, ffn1_w_up, ffn1_w_down)]
    f2 = [t.astype(BF16) for t in (ffn2_w_gate, ffn2_w_up, ffn2_w_down)]
    wbr, wout = w_branch.astype(BF16), w_out.astype(BF16)
    sinks = jnp.broadcast_to(attn_sinks.astype(F32)[:, :, None],
                             (DEPTH, A_HEADS, BLK)).reshape(DEPTH, A_KV_HEADS, 1, A_LANES)
    subln = jnp.broadcast_to(diff_subln.astype(F32)[:, :, None], (DEPTH, B_V_DIM, BLK))
    lam_f = diff_lambda.astype(F32)

    for l in range(DEPTH):
        h = _ffn_first(x, lead, g1, *f1, nb) if l == 0 else _ffn(h, g1, *f1, l)
        proj, avt, bvt, cvt, iwt = _proj(h, gm, w_proj, w_proj_t, l)
        o_a = _mixer_a(proj, avt, bias_a, sinks[l], bsz, nb)
        lam_init = 0.8 - 0.6 * math.exp(-0.3 * l)
        o_b = _mixer_b(proj, bvt, bias_b, lam_f[l:l + 1], subln[l:l + 1], bsz, nb, lam_init)
        o_c = _mixer_c(proj, cvt, iwt, bias_c, bsz, nb, topk)
        h = _merge(h, gm, o_a, o_b, o_c, w_gates, wbr, wout, l)
        h = _ffn(h, g2, *f2, l)

    return _final_norm(h, norm_final.reshape(1, D_MODEL), bsz, nb)
```

```python
import functools
import math

import jax
import jax.numpy as jnp
import numpy as np
from jax import lax
from jax.experimental import pallas as pl
from jax.experimental.pallas import tpu as pltpu

F32 = jnp.float32
BF16 = jnp.bfloat16

D_MODEL = 2048
D_FF = 5632
DEPTH = 2
BLK = 128
N_META = 16
N_PAD = BLK - N_META
WINDOW = 128
assert WINDOW == BLK
HEAD_DIM = 64
A_HEADS, A_KV_HEADS = 8, 2
B_HEADS, B_QK_DIM, B_V_DIM = 4, 64, 128
C_HEADS = 8
IDX_HEADS, IDX_DIM = 8, 64
TOPK_MAX = 256
N_BUCKETS, MAX_DISTANCE = 32, 128
EPS = 1e-6
NEG = -1e30

COL_AQ, COL_CQ, COL_IQ = 0, 512, 1024
COL_BQ1, COL_BQ2, COL_BK1, COL_BK2 = 1536, 1792, 2048, 2304
COL_AK, COL_CKIK = 2560, 2688
N_PROJ = 3072
ROW_AV, ROW_BV, ROW_CV, ROW_IW, N_PROJ_T = 0, 128, 640, 704, 720

V_ONES = 16
A_VROWS, B_VROWS, C_VROWS = HEAD_DIM + V_ONES, B_V_DIM + V_ONES, HEAD_DIM + V_ONES

TM = 512
TM_FFN = 1024
TM_PROJ = 1024
TF = 512
FFN_VMEM_LIMIT = 62 * 1024 * 1024
TN = 1024
TC_MERGE = 512
assert HEAD_DIM == B_QK_DIM == IDX_DIM == 64
QK_SCALE = 0.125
VMEM_LIMIT = 56 * 1024 * 1024


def _nt(a, b):
    return lax.dot_general(a, b, (((1,), (1,)), ((), ())), preferred_element_type=F32)


def _mm(a, b):
    return jnp.dot(a, b, preferred_element_type=F32)


def _rms(x, g):
    ms = jnp.mean(x * x, axis=-1, keepdims=True)
    return x * lax.rsqrt(ms + EPS) * g


def _ffn_accumulate(xn_ref, wg_ref, wu_ref, wd_ref, o_ref):
    for r in range(TM_FFN // TM):
        rows = slice(r * TM, (r + 1) * TM)
        xn = xn_ref[rows, :]
        a = _mm(xn, wg_ref[...])
        b = _mm(xn, wu_ref[...])
        t = (a * jax.nn.sigmoid(a)) * b
        o_ref[rows, :] += 0.5 * _mm(t.astype(BF16), wd_ref[...])


def _ffn_kernel(h_ref, g_ref, wg_ref, wu_ref, wd_ref, o_ref, xn_ref):
    @pl.when(pl.program_id(1) == 0)
    def _():
        h = h_ref[...]
        xn_ref[...] = _rms(h, g_ref[...]).astype(BF16)
        o_ref[...] = h

    _ffn_accumulate(xn_ref, wg_ref, wu_ref, wd_ref, o_ref)


def _ffn_first_kernel(*refs, nb):
    nblk = TM_FFN // BLK
    x_refs = refs[:nblk]
    lead_ref, g_ref, wg_ref, wu_ref, wd_ref, o_ref, xn_ref = refs[nblk:]

    @pl.when(pl.program_id(1) == 0)
    def _():
        first = pl.program_id(0) * nblk
        for t in range(nblk):
            rows = slice(t * BLK, (t + 1) * BLK)
            o_ref[rows, :] = x_refs[t][...]

            @pl.when(lax.rem(first + t, nb) == 0)
            def _():
                o_ref[rows, :] = lead_ref[...]

        xn_ref[...] = _rms(o_ref[...], g_ref[...]).astype(BF16)

    _ffn_accumulate(xn_ref, wg_ref, wu_ref, wd_ref, o_ref)


def _ffn_weight_specs(layer):
    return [
        pl.BlockSpec((None, 1, D_MODEL), lambda i, k: (layer, 0, 0)),
        pl.BlockSpec((None, D_MODEL, TF), lambda i, k: (layer, 0, k)),
        pl.BlockSpec((None, D_MODEL, TF), lambda i, k: (layer, 0, k)),
        pl.BlockSpec((None, TF, D_MODEL), lambda i, k: (layer, k, 0)),
    ]


def _ffn(h, gain, wg, wu, wd, layer):
    n = h.shape[0]
    return pl.pallas_call(
        _ffn_kernel,
        grid=(n // TM_FFN, D_FF // TF),
        in_specs=[pl.BlockSpec((TM_FFN, D_MODEL), lambda i, k: (i, 0))] + _ffn_weight_specs(layer),
        out_specs=pl.BlockSpec((TM_FFN, D_MODEL), lambda i, k: (i, 0)),
        out_shape=jax.ShapeDtypeStruct((n, D_MODEL), F32),
        scratch_shapes=[pltpu.VMEM((TM_FFN, D_MODEL), BF16)],
        input_output_aliases={0: 0},
        name="ffn",
        compiler_params=pltpu.CompilerParams(
            dimension_semantics=("parallel", "arbitrary"), vmem_limit_bytes=FFN_VMEM_LIMIT),
    )(h, gain, wg, wu, wd)


def _ffn_first(x, lead, gain, wg, wu, wd, nb):
    bsz = x.shape[0]
    n = bsz * nb * BLK
    nblk = TM_FFN // BLK

    def x_spec(t):
        return pl.BlockSpec((None, BLK, D_MODEL),
                            lambda i, k: ((i * nblk + t) // nb, jnp.maximum((i * nblk + t) % nb - 1, 0), 0))

    return pl.pallas_call(
        functools.partial(_ffn_first_kernel, nb=nb),
        grid=(n // TM_FFN, D_FF // TF),
        in_specs=[x_spec(t) for t in range(nblk)]
        + [pl.BlockSpec((BLK, D_MODEL), lambda i, k: (0, 0))] + _ffn_weight_specs(0),
        out_specs=pl.BlockSpec((TM_FFN, D_MODEL), lambda i, k: (i, 0)),
        out_shape=jax.ShapeDtypeStruct((n, D_MODEL), F32),
        scratch_shapes=[pltpu.VMEM((TM_FFN, D_MODEL), BF16)],
        name="ffn_first",
        compiler_params=pltpu.CompilerParams(
            dimension_semantics=("parallel", "arbitrary"), vmem_limit_bytes=FFN_VMEM_LIMIT),
    )(*([x] * nblk), lead, gain, wg, wu, wd)


def _proj_kernel(h_ref, g_ref, w_ref, wt_ref, o_ref, avt_ref, bvt_ref, cvt_ref, iwt_ref, xn_ref):
    j = pl.program_id(1)

    @pl.when(j == 0)
    def _():
        xn = _rms(h_ref[...], g_ref[...]).astype(BF16)
        xn_ref[...] = xn
        vt = _nt(wt_ref[...], xn)
        for t in range(TM_PROJ // BLK):
            cols = slice(t * BLK, (t + 1) * BLK)
            ones = jnp.ones((V_ONES, BLK), BF16)
            for g in range(A_KV_HEADS):
                avt_ref[t, g * A_VROWS:g * A_VROWS + HEAD_DIM] = (
                    vt[ROW_AV + g * HEAD_DIM:ROW_AV + (g + 1) * HEAD_DIM, cols].astype(BF16))
                avt_ref[t, g * A_VROWS + HEAD_DIM:(g + 1) * A_VROWS] = ones
            for h in range(B_HEADS):
                bvt_ref[t, h * B_VROWS:h * B_VROWS + B_V_DIM] = (
                    vt[ROW_BV + h * B_V_DIM:ROW_BV + (h + 1) * B_V_DIM, cols].astype(BF16))
                bvt_ref[t, h * B_VROWS + B_V_DIM:(h + 1) * B_VROWS] = ones
            cvt_ref[t, 0:HEAD_DIM] = vt[ROW_CV:ROW_IW, cols].astype(BF16)
            cvt_ref[t, HEAD_DIM:C_VROWS] = ones
            iwt_ref[t] = vt[ROW_IW:ROW_IW + IDX_HEADS, cols]

    o_ref[...] = _mm(xn_ref[...], w_ref[...]).astype(BF16)


def _proj(h, gain, w, wt, layer):
    n = h.shape[0]
    nt = n // BLK
    tpb = TM_PROJ // BLK
    return pl.pallas_call(
        _proj_kernel,
        grid=(n // TM_PROJ, N_PROJ // TN),
        in_specs=[
            pl.BlockSpec((TM_PROJ, D_MODEL), lambda i, j: (i, 0)),
            pl.BlockSpec((None, 1, D_MODEL), lambda i, j: (layer, 0, 0)),
            pl.BlockSpec((None, D_MODEL, TN), lambda i, j: (layer, 0, j)),
            pl.BlockSpec((None, N_PROJ_T, D_MODEL), lambda i, j: (layer, 0, 0)),
        ],
        out_specs=[
            pl.BlockSpec((TM_PROJ, TN), lambda i, j: (i, j)),
            pl.BlockSpec((tpb, A_KV_HEADS * A_VROWS, BLK), lambda i, j: (i, 0, 0)),
            pl.BlockSpec((tpb, B_HEADS * B_VROWS, BLK), lambda i, j: (i, 0, 0)),
            pl.BlockSpec((tpb, C_VROWS, BLK), lambda i, j: (i, 0, 0)),
            pl.BlockSpec((tpb, IDX_HEADS, BLK), lambda i, j: (i, 0, 0)),
        ],
        out_shape=[
            jax.ShapeDtypeStruct((n, N_PROJ), BF16),
            jax.ShapeDtypeStruct((nt, A_KV_HEADS * A_VROWS, BLK), BF16),
            jax.ShapeDtypeStruct((nt, B_HEADS * B_VROWS, BLK), BF16),
            jax.ShapeDtypeStruct((nt, C_VROWS, BLK), BF16),
            jax.ShapeDtypeStruct((nt, IDX_HEADS, BLK), F32),
        ],
        scratch_shapes=[pltpu.VMEM((TM_PROJ, D_MODEL), BF16)],
        name="mixer_proj",
        compiler_params=pltpu.CompilerParams(
            dimension_semantics=("parallel", "arbitrary"), vmem_limit_bytes=VMEM_LIMIT),
    )(h, gain, w, wt)


def _tile_iotas():
    kl = lax.broadcasted_iota(jnp.int32, (BLK, BLK), 0)
    ql = lax.broadcasted_iota(jnp.int32, (BLK, BLK), 1)
    return kl, ql


def _colmax(x):
    return jnp.max(x, axis=0, keepdims=True)


def _colsum(x):
    return jnp.sum(x, axis=0, keepdims=True)


def _exp_bf16(x):
    return jnp.exp(x.astype(BF16))


def _pair_tiles(p, nb):
    ja = 2 * p
    return ((ja, ja), (ja + 1, jnp.minimum(ja + 1, nb - 1)))


A_GRP = A_HEADS // A_KV_HEADS
A_LANES = A_GRP * BLK


def _mixer_a_kernel(q_ref, k0_ref, k1_ref, k2_ref, v0_ref, v1_ref, v2_ref, bias_ref, sink_ref, o_ref, qst_ref):
    i = pl.program_id(1)
    kl = lax.broadcasted_iota(jnp.int32, (BLK, A_LANES), 0)
    ql = lax.broadcasted_iota(jnp.int32, (BLK, A_LANES), 1) & (BLK - 1)
    qpos = BLK * i - N_PAD + ql
    ok_meta = (kl >= N_PAD) & (qpos >= kl - N_PAD)
    ok_prev = (BLK * (i - 1) - N_PAD + kl >= N_META) & (ql < kl)
    ok_cur = (BLK * i - N_PAD + kl >= N_META) & (ql >= kl)
    t_meta = jnp.minimum(i, 2)
    qst_ref[...] = jnp.zeros_like(qst_ref)
    for h in range(A_HEADS):
        g, r = divmod(h, A_GRP)
        qst_ref[g, r * BLK:(r + 1) * BLK, g * HEAD_DIM:(g + 1) * HEAD_DIM] = (
            q_ref[:, h * HEAD_DIM:(h + 1) * HEAD_DIM] * QK_SCALE)
    outs = []
    for g in range(A_KV_HEADS):
        qs = qst_ref[g]
        s0 = jnp.where(ok_meta, _nt(k0_ref[...], qs) + bias_ref[g, t_meta], NEG)
        s1 = jnp.where(ok_prev, _nt(k1_ref[...], qs) + bias_ref[g, 1], NEG)
        s2 = jnp.where(ok_cur, _nt(k2_ref[...], qs) + bias_ref[g, 0], NEG)
        sink = sink_ref[g]
        m = jnp.maximum(jnp.maximum(_colmax(s0), _colmax(s1)), jnp.maximum(_colmax(s2), sink))
        vs = slice(g * A_VROWS, (g + 1) * A_VROWS)
        ot = (_mm(v0_ref[vs, :], _exp_bf16(s0 - m)) + _mm(v1_ref[vs, :], _exp_bf16(s1 - m))
              + _mm(v2_ref[vs, :], _exp_bf16(s2 - m)))
        ot = ot[:HEAD_DIM] / (ot[HEAD_DIM:HEAD_DIM + 1] + jnp.exp(sink - m))
        outs.extend(ot[:, r * BLK:(r + 1) * BLK] for r in range(A_GRP))
    o_ref[...] = jnp.concatenate(outs, axis=0).T.astype(BF16)


def _mixer_a(proj, avt, bias, sinks, bsz, nb):
    n = proj.shape[0]
    kcol = COL_AK // BLK
    rowq = lambda b, i: (b * nb + i, COL_AQ // (A_HEADS * HEAD_DIM))
    spec_k = lambda f: pl.BlockSpec((BLK, BLK), lambda b, i: (b * nb + f(i), kcol))
    spec_v = lambda f: pl.BlockSpec((None, A_KV_HEADS * A_VROWS, BLK), lambda b, i: (b * nb + f(i), 0, 0))
    first = lambda i: 0
    prev = lambda i: jnp.maximum(i - 1, 0)
    cur = lambda i: i
    return pl.pallas_call(
        _mixer_a_kernel,
        grid=(bsz, nb),
        in_specs=[
            pl.BlockSpec((BLK, A_HEADS * HEAD_DIM), rowq),
            spec_k(first), spec_k(prev), spec_k(cur),
            spec_v(first), spec_v(prev), spec_v(cur),
            pl.BlockSpec((A_KV_HEADS, 3, BLK, A_LANES), lambda b, i: (0, 0, 0, 0)),
            pl.BlockSpec((A_KV_HEADS, 1, A_LANES), lambda b, i: (0, 0, 0)),
        ],
        out_specs=pl.BlockSpec((BLK, A_HEADS * HEAD_DIM), lambda b, i: (b * nb + i, 0)),
        out_shape=jax.ShapeDtypeStruct((n, A_HEADS * HEAD_DIM), BF16),
        scratch_shapes=[pltpu.VMEM((A_KV_HEADS, A_LANES, BLK), BF16)],
        name="mixer_a",
        compiler_params=pltpu.CompilerParams(dimension_semantics=("parallel", "parallel")),
    )(proj, proj, proj, proj, avt, avt, avt, bias, sinks)


B_GROUPS = B_HEADS // 2


def _mixer_b_kernel(q1_ref, q2_ref, k1_ref, k2_ref, vt_ref, bias_ref, lam_ref, subln_ref, o_ref,
                    qbd_ref, acc_ref, *, lam_init, nb):
    i = pl.program_id(1)
    npair = (i + 2) // 2
    w2 = 2 * BLK
    kl = lax.broadcasted_iota(jnp.int32, (BLK, w2), 0)
    qpos = BLK * i - N_PAD + (lax.broadcasted_iota(jnp.int32, (BLK, w2), 1) & (BLK - 1))

    feat = lax.broadcasted_iota(jnp.int32, (BLK, BLK), 0)
    for mp, q_ref in enumerate((q1_ref, q2_ref)):
        for g in range(B_GROUPS):
            qt = (q_ref[:, g * BLK:(g + 1) * BLK].astype(F32) * QK_SCALE).T
            qbd_ref[mp, g] = jnp.concatenate(
                [jnp.where(feat < B_QK_DIM, qt, 0.0), jnp.where(feat >= B_QK_DIM, qt, 0.0)], axis=1).astype(BF16)
    acc_ref[...] = jnp.zeros_like(acc_ref)

    def step(jls, ms):
        tiles = [(jl, jnp.minimum(jl, nb - 1)) for jl in jls]
        oks, ts, rows = [], [], []
        for jl, jk in tiles:
            kpos = BLK * jl - N_PAD + kl
            oks.append((kpos >= 0) & (qpos >= kpos))
            ts.append(jnp.clip(i - jl, 0, 2))
            rows.append(pl.ds(pl.multiple_of(jk * BLK, BLK), BLK))
        ok2 = jnp.concatenate(oks, axis=0)
        raw = [[_mm(jnp.concatenate([k_ref[r, g * BLK:(g + 1) * BLK] for r in rows], axis=0), qbd_ref[mp, g])
                for g in range(B_GROUPS)] for mp, k_ref in enumerate((k1_ref, k2_ref))]
        bias2 = [jnp.concatenate([bias_ref[g, t] for t in ts], axis=0) for g in range(B_GROUPS)]
        new_ms = []
        ps = [[None] * B_GROUPS for _ in range(2)]
        alphas = [[None] * B_GROUPS for _ in range(2)]
        for mp in range(2):
            row_m = []
            for g in range(B_GROUPS):
                s = jnp.where(ok2, raw[mp][g] + bias2[g], NEG)
                m_old = ms[mp][g]
                m_new = jnp.maximum(m_old, _colmax(s))
                row_m.append(m_new)
                ps[mp][g] = _exp_bf16(s - m_new)
                alphas[mp][g] = jnp.exp(m_old - m_new)
            new_ms.append(row_m)
        for h in range(B_HEADS):
            g, hs = h // 2, slice((h % 2) * BLK, (h % 2 + 1) * BLK)
            alpha = jnp.concatenate([alphas[0][g][:, hs], alphas[1][g][:, hs]], axis=1)
            p12 = jnp.concatenate([ps[0][g][:, hs], ps[1][g][:, hs]], axis=1)
            vt2 = jnp.concatenate([vt_ref[jk, h * B_VROWS:(h + 1) * B_VROWS, :] for _, jk in tiles], axis=1)
            acc_ref[h] = alpha * acc_ref[h] + _mm(vt2, p12)
        return new_ms

    nquad = (i + 1) // 4
    ms_run = lax.fori_loop(0, nquad, lambda q, ms: step([4 * q + t for t in range(4)], ms),
                           [[jnp.full((1, w2), NEG, F32) for _ in range(B_GROUPS)] for _ in range(2)])
    lax.fori_loop(2 * nquad, npair, lambda p, ms: step([2 * p, 2 * p + 1], ms), ms_run)

    lf = lam_ref[...]
    lam = (jnp.exp(jnp.sum(lf[0:1] * lf[1:2], axis=-1, keepdims=True))
           - jnp.exp(jnp.sum(lf[2:3] * lf[3:4], axis=-1, keepdims=True)) + lam_init)
    outs = []
    for h in range(B_HEADS):
        acc = acc_ref[h]
        num, den = acc[:B_V_DIM], acc[B_V_DIM:B_V_DIM + 1]
        o = num[:, :BLK] / den[:, :BLK] - lam * (num[:, BLK:] / den[:, BLK:])
        ms_ = jnp.mean(o * o, axis=0, keepdims=True)
        outs.append(o * lax.rsqrt(ms_ + EPS) * subln_ref[...] * (1.0 - lam_init))
    o_ref[...] = jnp.concatenate(outs, axis=0).T.astype(BF16)


def _mixer_b(proj, bvt, bias, lam, subln, bsz, nb, lam_init):
    n = proj.shape[0]
    p_rows = nb * BLK
    w = B_HEADS * B_QK_DIM
    rowq = lambda c: (lambda b, i: (b * nb + i, c))
    return pl.pallas_call(
        functools.partial(_mixer_b_kernel, lam_init=lam_init, nb=nb),
        grid=(bsz, nb),
        in_specs=[
            pl.BlockSpec((BLK, w), rowq(COL_BQ1 // w)),
            pl.BlockSpec((BLK, w), rowq(COL_BQ2 // w)),
            pl.BlockSpec((p_rows, w), lambda b, i: (b, COL_BK1 // w)),
            pl.BlockSpec((p_rows, w), lambda b, i: (b, COL_BK2 // w)),
            pl.BlockSpec((nb, B_HEADS * B_VROWS, BLK), lambda b, i: (b, 0, 0)),
            pl.BlockSpec((B_GROUPS, 3, BLK, 2 * BLK), lambda b, i: (0, 0, 0, 0)),
            pl.BlockSpec((None, 4, B_QK_DIM), lambda b, i: (0, 0, 0)),
            pl.BlockSpec((None, B_V_DIM, BLK), lambda b, i: (0, 0, 0)),
        ],
        out_specs=pl.BlockSpec((BLK, B_HEADS * B_V_DIM), lambda b, i: (b * nb + i, 0)),
        out_shape=jax.ShapeDtypeStruct((n, B_HEADS * B_V_DIM), BF16),
        scratch_shapes=[
            pltpu.VMEM((2, B_GROUPS, BLK, 2 * BLK), BF16),
            pltpu.VMEM((B_HEADS, B_VROWS, 2 * BLK), F32),
        ],
        name="mixer_b",
        compiler_params=pltpu.CompilerParams(dimension_semantics=("parallel", "arbitrary")),
    )(proj, proj, proj, proj, bvt, bias, lam, subln)


C_LANES = C_HEADS * BLK
C_CHUNK = 2 * BLK


def _ordered_bits_to_float(u):
    bits = jnp.where(u < 0, u ^ jnp.int32(-2 ** 31), ~u)
    return lax.bitcast_convert_type(bits, F32)


def _mixer_c_kernel(cq_ref, iq_ref, kk_ref, vt_ref, iwt_ref, bias_ref, o_ref,
                    score_ref, sel_ref, qst_ref, iqst_ref, acc_ref, *, topk, nb):
    i = pl.program_id(1)
    npair = (i + 2) // 2
    kl, ql = _tile_iotas()
    qpos = BLK * i - N_PAD + ql
    kf = jnp.float32(topk)

    qst_ref[...] = jnp.zeros_like(qst_ref)
    iqst_ref[...] = jnp.zeros_like(iqst_ref)
    for g in range(C_HEADS // 2):
        cols = slice(g * BLK, (g + 1) * BLK)
        qt = (cq_ref[:, cols].astype(F32) * QK_SCALE).T.astype(BF16)
        it = iq_ref[:, cols].astype(F32).T.astype(BF16)
        for hh in range(2):
            lanes = slice((2 * g + hh) * BLK, (2 * g + hh + 1) * BLK)
            qst_ref[0:HEAD_DIM, lanes] = qt[hh * HEAD_DIM:(hh + 1) * HEAD_DIM, :]
            iqst_ref[HEAD_DIM:BLK, lanes] = it[hh * IDX_DIM:(hh + 1) * IDX_DIM, :]
    iw = iwt_ref[...] * QK_SCALE
    iw_all = jnp.concatenate([iw[h:h + 1, :] for h in range(IDX_HEADS)], axis=1)

    def admissible(jl):
        kpos = BLK * jl - N_PAD + kl
        return (kpos >= 0) & (qpos >= kpos)

    def score_tiles(jls):
        kk = jnp.concatenate([kk_ref[pl.ds(pl.multiple_of(jnp.minimum(jl, nb - 1) * BLK, BLK), BLK), :]
                              for jl in jls], axis=0)
        sc = None
        for c in range(C_LANES // C_CHUNK):
            ln = slice(c * C_CHUNK, (c + 1) * C_CHUNK)
            rel = jnp.maximum(_mm(kk, iqst_ref[:, ln]), 0.0) * iw_all[:, ln]
            for h in range(C_CHUNK // BLK):
                part = rel[:, h * BLK:(h + 1) * BLK]
                sc = part if sc is None else sc + part
        for t, jl in enumerate(jls):
            score_ref[jl] = jnp.where(admissible(jl), sc[t * BLK:(t + 1) * BLK] * IDX_HEADS ** -0.5, NEG)
        return 0

    nquad = (i + 1) // 4
    lax.fori_loop(0, nquad, lambda q, c: score_tiles([4 * q + t for t in range(4)]), 0)
    lax.fori_loop(2 * nquad, npair, lambda p, c: score_tiles([2 * p, 2 * p + 1]), 0)

    def count(pred):
        def body(p, cs):
            xa = jnp.where(pred(score_ref[2 * p]), 1.0, 0.0)
            xb = jnp.where(pred(score_ref[2 * p + 1]), 1.0, 0.0)
            return [c + (xa[r * 8:(r + 1) * 8, :] + xb[r * 8:(r + 1) * 8, :]) for r, c in enumerate(cs)]
        parts = lax.fori_loop(0, npair, body, [jnp.zeros((8, BLK), F32)] * (BLK // 8))
        while len(parts) > 1:
            parts = [parts[t] + parts[t + 1] for t in range(0, len(parts), 2)]
        return _colsum(parts[0])

    def search(it, u):
        cand = u | lax.shift_left(jnp.int32(1), 31 - it)
        thr_c = _ordered_bits_to_float(cand)
        return jnp.where(count(lambda s: s >= thr_c) >= kf, cand, u)

    u = lax.fori_loop(0, 32, search, jnp.zeros((1, BLK), jnp.int32))
    found = (u < 0) | (u >= jnp.int32(0x00800000))
    thr = jnp.where(found, _ordered_bits_to_float(u), -jnp.inf)
    need = kf - count(lambda s: s > thr)
    n_ge = count(lambda s: s >= thr)
    has_tie = jnp.max(jnp.where((n_ge > kf) & (thr > NEG), 1.0, 0.0)) > 0.5

    def select_plain():
        def body(p, carry):
            for jl in (2 * p, 2 * p + 1):
                sel_ref[jl] = jnp.where(admissible(jl) & (score_ref[jl] >= thr), 0.0, NEG)
            return carry
        lax.fori_loop(0, npair, body, 0)

    def select_ties():
        tri = (kl >= ql).astype(BF16)

        def body(jl, seen):
            sc = score_ref[jl]
            eq = sc == thr
            eqf = jnp.where(eq, 1.0, 0.0)
            rank = seen + _mm(tri, eqf.astype(BF16))
            sel = admissible(jl) & ((sc > thr) | (eq & (rank <= need)))
            sel_ref[jl] = jnp.where(sel, 0.0, NEG)
            return seen + _colsum(eqf)
        lax.fori_loop(0, 2 * npair, body, jnp.zeros((1, BLK), F32))

    lax.cond(has_tie, select_ties, select_plain)

    acc_ref[...] = jnp.zeros_like(acc_ref)

    def attend_tiles(jls, m_old):
        jks = [jnp.minimum(jl, nb - 1) for jl in jls]
        kk = jnp.concatenate([kk_ref[pl.ds(pl.multiple_of(jk * BLK, BLK), BLK), :] for jk in jks], axis=0)
        vt = jnp.concatenate([vt_ref[jk] for jk in jks], axis=1)
        extra = jnp.concatenate([bias_ref[jnp.clip(i - jl, 0, 2)] + jnp.tile(sel_ref[jl], (1, C_HEADS))
                                 for jl in jls], axis=0)
        s = _mm(kk, qst_ref[...]) + extra
        m_new = jnp.maximum(m_old, _colmax(s))
        acc_ref[...] = jnp.exp(m_old - m_new) * acc_ref[...] + _mm(vt, _exp_bf16(s - m_new))
        return m_new

    nquad = (i + 1) // 4
    m_run = lax.fori_loop(0, nquad, lambda q, m: attend_tiles([4 * q + t for t in range(4)], m),
                          jnp.full((1, C_LANES), NEG, F32))
    lax.fori_loop(2 * nquad, npair, lambda p, m: attend_tiles([2 * p, 2 * p + 1], m), m_run)
    o = acc_ref[0:HEAD_DIM, :] / acc_ref[HEAD_DIM:HEAD_DIM + 1, :]
    o_rows = jnp.concatenate([o[:, h * BLK:(h + 1) * BLK] for h in range(C_HEADS)], axis=0)
    o_ref[...] = o_rows.T.astype(BF16)


def _mixer_c(proj, cvt, iwt, bias, bsz, nb, topk):
    n = proj.shape[0]
    p_rows = nb * BLK
    w = C_HEADS * HEAD_DIM
    return pl.pallas_call(
        functools.partial(_mixer_c_kernel, topk=topk, nb=nb),
        grid=(bsz, nb),
        in_specs=[
            pl.BlockSpec((BLK, w), lambda b, i: (b * nb + i, COL_CQ // w)),
            pl.BlockSpec((BLK, w), lambda b, i: (b * nb + i, COL_IQ // w)),
            pl.BlockSpec((p_rows, BLK), lambda b, i: (b, COL_CKIK // BLK)),
            pl.BlockSpec((nb, C_VROWS, BLK), lambda b, i: (b, 0, 0)),
            pl.BlockSpec((None, IDX_HEADS, BLK), lambda b, i: (b * nb + i, 0, 0)),
            pl.BlockSpec((3, BLK, C_LANES), lambda b, i: (0, 0, 0)),
        ],
        out_specs=pl.BlockSpec((BLK, w), lambda b, i: (b * nb + i, 0)),
        out_shape=jax.ShapeDtypeStruct((n, w), BF16),
        scratch_shapes=[
            pltpu.VMEM((nb + 1, BLK, BLK), F32),
            pltpu.VMEM((nb + 1, BLK, BLK), F32),
            pltpu.VMEM((BLK, C_LANES), BF16),
            pltpu.VMEM((BLK, C_LANES), BF16),
            pltpu.VMEM((C_VROWS, C_LANES), F32),
        ],
        name="mixer_c",
        compiler_params=pltpu.CompilerParams(dimension_semantics=("parallel", "arbitrary")),
    )(proj, proj, proj, cvt, iwt, bias)


def _merge_kernel(h_ref, g_ref, oa_ref, ob_ref, oc_ref, ga_ref, gb_ref, gc_ref, wbr_ref, wo_ref,
                  o_ref, xn_ref, acc_ref, *, nc):
    c = pl.program_id(1)

    @pl.when(c == 0)
    def _():
        xn_ref[...] = _rms(h_ref[...], g_ref[...]).astype(BF16)
        acc_ref[...] = jnp.zeros_like(acc_ref)

    xn = xn_ref[...]
    y = (jax.nn.sigmoid(_mm(xn, ga_ref[...])) * _mm(oa_ref[...], wbr_ref[0])
         + jax.nn.sigmoid(_mm(xn, gb_ref[...])) * _mm(ob_ref[...], wbr_ref[1])
         + jax.nn.sigmoid(_mm(xn, gc_ref[...])) * _mm(oc_ref[...], wbr_ref[2]))
    acc_ref[...] += _mm(y.astype(BF16), wo_ref[...])

    @pl.when(c == nc - 1)
    def _():
        o_ref[...] = h_ref[...] + acc_ref[...]


def _merge(h, gain, oa, ob, oc, wgates, wbr, wout, layer):
    n = h.shape[0]
    tc = TC_MERGE
    nc = D_MODEL // tc
    bw = 512
    row = lambda i, c: (i, 0)
    wgate = lambda br: pl.BlockSpec((None, D_MODEL, tc), lambda i, c: (layer, 0, br * nc + c))
    return pl.pallas_call(
        functools.partial(_merge_kernel, nc=nc),
        grid=(n // TM, nc),
        in_specs=[
            pl.BlockSpec((TM, D_MODEL), row),
            pl.BlockSpec((None, 1, D_MODEL), lambda i, c: (layer, 0, 0)),
            pl.BlockSpec((TM, bw), row), pl.BlockSpec((TM, bw), row), pl.BlockSpec((TM, bw), row),
            wgate(0), wgate(1), wgate(2),
            pl.BlockSpec((None, 3, bw, tc), lambda i, c: (layer, 0, 0, c)),
            pl.BlockSpec((None, tc, D_MODEL), lambda i, c: (layer, c, 0)),
        ],
        out_specs=pl.BlockSpec((TM, D_MODEL), row),
        out_shape=jax.ShapeDtypeStruct((n, D_MODEL), F32),
        scratch_shapes=[pltpu.VMEM((TM, D_MODEL), BF16), pltpu.VMEM((TM, D_MODEL), F32)],
        input_output_aliases={0: 0},
        name="branch_merge",
        compiler_params=pltpu.CompilerParams(
            dimension_semantics=("parallel", "arbitrary"), vmem_limit_bytes=VMEM_LIMIT),
    )(h, gain, oa, ob, oc, wgates, wgates, wgates, wbr, wout)


def _final_kernel(h_ref, g_ref, o_ref):
    o_ref[...] = _rms(h_ref[...], g_ref[...])


def _final_norm(h, gain, bsz, nb):
    seq = (nb - 1) * BLK
    tr = 512 if seq % 512 == 0 else BLK
    return pl.pallas_call(
        _final_kernel,
        grid=(bsz, seq // tr),
        in_specs=[
            pl.BlockSpec((pl.Element(tr), pl.Element(D_MODEL)),
                         lambda b, r: (pl.multiple_of((b * nb + 1) * BLK + r * tr, BLK), 0)),
            pl.BlockSpec((1, D_MODEL), lambda b, r: (0, 0)),
        ],
        out_specs=pl.BlockSpec((None, tr, D_MODEL), lambda b, r: (b, r, 0)),
        out_shape=jax.ShapeDtypeStruct((bsz, seq, D_MODEL), F32),
        name="final_norm",
        compiler_params=pltpu.CompilerParams(dimension_semantics=("parallel", "parallel")),
    )(h, gain)


def _bucket_np(d):
    max_exact = N_BUCKETS // 2
    d = np.maximum(d, 0)
    df = np.maximum(d, 1).astype(np.float32)
    large = max_exact + (np.log(df / max_exact) / math.log(MAX_DISTANCE / max_exact)
                         * (N_BUCKETS - max_exact)).astype(np.int32)
    return np.where(d < max_exact, d, np.minimum(large, N_BUCKETS - 1)).astype(np.int32)


def _bias_tiles(table):
    kl = np.arange(BLK)[:, None]
    ql = np.arange(BLK)[None, :]
    idx = np.stack([_bucket_np(ql - kl), _bucket_np(BLK + ql - kl),
                    np.full((BLK, BLK), N_BUCKETS - 1, np.int32)])
    onehot = (idx.reshape(-1, 1) == np.arange(N_BUCKETS)[None, :]).astype(np.float32)
    tiles = jnp.dot(onehot, table.astype(F32), precision=lax.Precision.HIGHEST)
    return jnp.transpose(tiles.reshape(3, BLK, BLK, -1), (3, 0, 1, 2))


def _pack_w_in(w_in):
    o = np.cumsum([0, 512, 128, 128, 256, 256, 256, 256, 512, 512, 64, 64, 512, 64, 8, 3 * D_MODEL])
    w_in = w_in.astype(BF16)
    (aq, ak, av, bq1, bq2, bk1, bk2, bv, cq, ck, cv, iq, ik, iw, gates) = [
        w_in[:, :, o[t]:o[t + 1]] for t in range(15)]
    pad = jnp.zeros(w_in.shape[:2] + (N_PROJ - COL_CKIK - BLK,), BF16)
    w = jnp.concatenate([aq, cq, iq, bq1, bq2, bk1, bk2, ak, ck, ik, pad], axis=-1)
    padt = jnp.zeros(w_in.shape[:2] + (N_PROJ_T - ROW_IW - IDX_HEADS,), BF16)
    wt = jnp.swapaxes(lax.optimization_barrier(jnp.concatenate([av, bv, cv, iw, padt], axis=-1)), 1, 2)
    return w, wt, gates


def kernel(x, meta_tokens, rel_bias_table, norm_ffn1, ffn1_w_gate, ffn1_w_up, ffn1_w_down, norm_mix, w_in,
           attn_sinks, diff_lambda, diff_subln, w_branch, w_out, norm_ffn2, ffn2_w_gate, ffn2_w_up,
           ffn2_w_down, norm_final):
    bsz, seq, _ = x.shape
    nb = seq // BLK + 1
    p_rows = nb * BLK
    n = bsz * p_rows
    assert seq % BLK == 0 and n % TM_FFN == 0
    topk = min(TOPK_MAX, seq // 4)

    lead = jnp.concatenate([jnp.zeros((N_PAD, D_MODEL), x.dtype), meta_tokens.astype(x.dtype)], axis=0)

    bias = _bias_tiles(rel_bias_table)
    bias_a = jnp.transpose(bias[:A_HEADS].reshape(A_KV_HEADS, A_GRP, 3, BLK, BLK),
                           (0, 2, 3, 1, 4)).reshape(A_KV_HEADS, 3, BLK, A_LANES)
    bias_b = jnp.transpose(bias[A_HEADS:A_HEADS + B_HEADS].reshape(B_GROUPS, 2, 3, BLK, BLK),
                           (0, 2, 3, 1, 4)).reshape(B_GROUPS, 3, BLK, 2 * BLK)
    bias_c = jnp.transpose(bias[A_HEADS + B_HEADS:], (1, 2, 0, 3)).reshape(3, BLK, C_LANES)
    w_proj, w_proj_t, w_gates = _pack_w_in(w_in)
    g1, gm, g2 = (t.reshape(DEPTH, 1, D_MODEL) for t in (norm_ffn1, norm_mix, norm_ffn2))
    f1 = [t.astype(BF16) for t in (ffn1_w_gate, ffn1_w_up, ffn1_w_down)]
    f2 = [t.astype(BF16) for t in (ffn2_w_gate, ffn2_w_up, ffn2_w_down)]
    wbr, wout = w_branch.astype(BF16), w_out.astype(BF16)
    sinks = jnp.broadcast_to(attn_sinks.astype(F32)[:, :, None],
                             (DEPTH, A_HEADS, BLK)).reshape(DEPTH, A_KV_HEADS, 1, A_LANES)
    subln = jnp.broadcast_to(diff_subln.astype(F32)[:, :, None], (DEPTH, B_V_DIM, BLK))
    lam_f = diff_lambda.astype(F32)

    for l in range(DEPTH):
        h = _ffn_first(x, lead, g1, *f1, nb) if l == 0 else _ffn(h, g1, *f1, l)
        proj, avt, bvt, cvt, iwt = _proj(h, gm, w_proj, w_proj_t, l)
        o_a = _mixer_a(proj, avt, bias_a, sinks[l], bsz, nb)
        lam_init = 0.8 - 0.6 * math.exp(-0.3 * l)
        o_b = _mixer_b(proj, bvt, bias_b, lam_f[l:l + 1], subln[l:l + 1], bsz, nb, lam_init)
        o_c = _mixer_c(proj, cvt, iwt, bias_c, bsz, nb, topk)
        h = _merge(h, gm, o_a, o_b, o_c, w_gates, wbr, wout, l)
        h = _ffn(h, g2, *f2, l)

    return _final_norm(h, norm_final.reshape(1, D_MODEL), bsz, nb)
```

```python
import functools
import math

import jax
import jax.numpy as jnp
import numpy as np
from jax import lax
from jax.experimental import pallas as pl
from jax.experimental.pallas import tpu as pltpu

F32 = jnp.float32
BF16 = jnp.bfloat16

D_MODEL = 2048
D_FF = 5632
DEPTH = 2
BLK = 128
N_META = 16
N_PAD = BLK - N_META
WINDOW = 128
assert WINDOW == BLK
HEAD_DIM = 64
A_HEADS, A_KV_HEADS = 8, 2
B_HEADS, B_QK_DIM, B_V_DIM = 4, 64, 128
C_HEADS = 8
IDX_HEADS, IDX_DIM = 8, 64
TOPK_MAX = 256
N_BUCKETS, MAX_DISTANCE = 32, 128
EPS = 1e-6
NEG = -1e30

COL_AQ, COL_CQ, COL_IQ = 0, 512, 1024
COL_BQ1, COL_BQ2, COL_BK1, COL_BK2 = 1536, 1792, 2048, 2304
COL_AK, COL_CKIK = 2560, 2688
N_PROJ = 3072
ROW_AV, ROW_BV, ROW_CV, ROW_IW, N_PROJ_T = 0, 128, 640, 704, 720

V_ONES = 16
A_VROWS, B_VROWS, C_VROWS = HEAD_DIM + V_ONES, B_V_DIM + V_ONES, HEAD_DIM + V_ONES

TM = 512
TM_FFN = 1024
TM_PROJ = 1024
TF = 512
FFN_VMEM_LIMIT = 62 * 1024 * 1024
TN = 1024
TC_MERGE = 512
assert HEAD_DIM == B_QK_DIM == IDX_DIM == 64
QK_SCALE = 0.125
VMEM_LIMIT = 56 * 1024 * 1024


def _nt(a, b):
    return lax.dot_general(a, b, (((1,), (1,)), ((), ())), preferred_element_type=F32)


def _mm(a, b):
    return jnp.dot(a, b, preferred_element_type=F32)


def _rms(x, g):
    ms = jnp.mean(x * x, axis=-1, keepdims=True)
    return x * lax.rsqrt(ms + EPS) * g


def _ffn_accumulate(xn_ref, wg_ref, wu_ref, wd_ref, o_ref):
    for r in range(TM_FFN // TM):
        rows = slice(r * TM, (r + 1) * TM)
        xn = xn_ref[rows, :]
        a = _mm(xn, wg_ref[...])
        b = _mm(xn, wu_ref[...])
        t = (a * jax.nn.sigmoid(a)) * b
        o_ref[rows, :] += 0.5 * _mm(t.astype(BF16), wd_ref[...])


def _ffn_kernel(h_ref, g_ref, wg_ref, wu_ref, wd_ref, o_ref, xn_ref):
    @pl.when(pl.program_id(1) == 0)
    def _():
        h = h_ref[...]
        xn_ref[...] = _rms(h, g_ref[...]).astype(BF16)
        o_ref[...] = h

    _ffn_accumulate(xn_ref, wg_ref, wu_ref, wd_ref, o_ref)


def _ffn_first_kernel(*refs, nb):
    nblk = TM_FFN // BLK
    x_refs = refs[:nblk]
    lead_ref, g_ref, wg_ref, wu_ref, wd_ref, o_ref, xn_ref = refs[nblk:]

    @pl.when(pl.program_id(1) == 0)
    def _():
        first = pl.program_id(0) * nblk
        for t in range(nblk):
            rows = slice(t * BLK, (t + 1) * BLK)
            o_ref[rows, :] = x_refs[t][...]

            @pl.when(lax.rem(first + t, nb) == 0)
            def _():
                o_ref[rows, :] = lead_ref[...]

        xn_ref[...] = _rms(o_ref[...], g_ref[...]).astype(BF16)

    _ffn_accumulate(xn_ref, wg_ref, wu_ref, wd_ref, o_ref)


def _ffn_weight_specs(layer):
    return [
        pl.BlockSpec((None, 1, D_MODEL), lambda i, k: (layer, 0, 0)),
        pl.BlockSpec((None, D_MODEL, TF), lambda i, k: (layer, 0, k)),
        pl.BlockSpec((None, D_MODEL, TF), lambda i, k: (layer, 0, k)),
        pl.BlockSpec((None, TF, D_MODEL), lambda i, k: (layer, k, 0)),
    ]


def _ffn(h, gain, wg, wu, wd, layer):
    n = h.shape[0]
    return pl.pallas_call(
        _ffn_kernel,
        grid=(n // TM_FFN, D_FF // TF),
        in_specs=[pl.BlockSpec((TM_FFN, D_MODEL), lambda i, k: (i, 0))] + _ffn_weight_specs(layer),
        out_specs=pl.BlockSpec((TM_FFN, D_MODEL), lambda i, k: (i, 0)),
        out_shape=jax.ShapeDtypeStruct((n, D_MODEL), F32),
        scratch_shapes=[pltpu.VMEM((TM_FFN, D_MODEL), BF16)],
        input_output_aliases={0: 0},
        name="ffn",
        compiler_params=pltpu.CompilerParams(
            dimension_semantics=("parallel", "arbitrary"), vmem_limit_bytes=FFN_VMEM_LIMIT),
    )(h, gain, wg, wu, wd)


def _ffn_first(x, lead, gain, wg, wu, wd, nb):
    bsz = x.shape[0]
    n = bsz * nb * BLK
    nblk = TM_FFN // BLK

    def x_spec(t):
        return pl.BlockSpec((None, BLK, D_MODEL),
                            lambda i, k: ((i * nblk + t) // nb, jnp.maximum((i * nblk + t) % nb - 1, 0), 0))

    return pl.pallas_call(
        functools.partial(_ffn_first_kernel, nb=nb),
        grid=(n // TM_FFN, D_FF // TF),
        in_specs=[x_spec(t) for t in range(nblk)]
        + [pl.BlockSpec((BLK, D_MODEL), lambda i, k: (0, 0))] + _ffn_weight_specs(0),
        out_specs=pl.BlockSpec((TM_FFN, D_MODEL), lambda i, k: (i, 0)),
        out_shape=jax.ShapeDtypeStruct((n, D_MODEL), F32),
        scratch_shapes=[pltpu.VMEM((TM_FFN, D_MODEL), BF16)],
        name="ffn_first",
        compiler_params=pltpu.CompilerParams(
            dimension_semantics=("parallel", "arbitrary"), vmem_limit_bytes=FFN_VMEM_LIMIT),
    )(*([x] * nblk), lead, gain, wg, wu, wd)


def _proj_kernel(h_ref, g_ref, w_ref, wt_ref, o_ref, avt_ref, bvt_ref, cvt_ref, iwt_ref, xn_ref):
    j = pl.program_id(1)

    @pl.when(j == 0)
    def _():
        xn = _rms(h_ref[...], g_ref[...]).astype(BF16)
        xn_ref[...] = xn
        vt = _nt(wt_ref[...], xn)
        for t in range(TM_PROJ // BLK):
            cols = slice(t * BLK, (t + 1) * BLK)
            ones = jnp.ones((V_ONES, BLK), BF16)
            for g in range(A_KV_HEADS):
                avt_ref[t, g * A_VROWS:g * A_VROWS + HEAD_DIM] = (
                    vt[ROW_AV + g * HEAD_DIM:ROW_AV + (g + 1) * HEAD_DIM, cols].astype(BF16))
                avt_ref[t, g * A_VROWS + HEAD_DIM:(g + 1) * A_VROWS] = ones
            for h in range(B_HEADS):
                bvt_ref[t, h * B_VROWS:h * B_VROWS + B_V_DIM] = (
                    vt[ROW_BV + h * B_V_DIM:ROW_BV + (h + 1) * B_V_DIM, cols].astype(BF16))
                bvt_ref[t, h * B_VROWS + B_V_DIM:(h + 1) * B_VROWS] = ones
            cvt_ref[t, 0:HEAD_DIM] = vt[ROW_CV:ROW_IW, cols].astype(BF16)
            cvt_ref[t, HEAD_DIM:C_VROWS] = ones
            iwt_ref[t] = vt[ROW_IW:ROW_IW + IDX_HEADS, cols]

    o_ref[...] = _mm(xn_ref[...], w_ref[...]).astype(BF16)


def _proj(h, gain, w, wt, layer):
    n = h.shape[0]
    nt = n // BLK
    tpb = TM_PROJ // BLK
    return pl.pallas_call(
        _proj_kernel,
        grid=(n // TM_PROJ, N_PROJ // TN),
        in_specs=[
            pl.BlockSpec((TM_PROJ, D_MODEL), lambda i, j: (i, 0)),
            pl.BlockSpec((None, 1, D_MODEL), lambda i, j: (layer, 0, 0)),
            pl.BlockSpec((None, D_MODEL, TN), lambda i, j: (layer, 0, j)),
            pl.BlockSpec((None, N_PROJ_T, D_MODEL), lambda i, j: (layer, 0, 0)),
        ],
        out_specs=[
            pl.BlockSpec((TM_PROJ, TN), lambda i, j: (i, j)),
            pl.BlockSpec((tpb, A_KV_HEADS * A_VROWS, BLK), lambda i, j: (i, 0, 0)),
            pl.BlockSpec((tpb, B_HEADS * B_VROWS, BLK), lambda i, j: (i, 0, 0)),
            pl.BlockSpec((tpb, C_VROWS, BLK), lambda i, j: (i, 0, 0)),
            pl.BlockSpec((tpb, IDX_HEADS, BLK), lambda i, j: (i, 0, 0)),
        ],
        out_shape=[
            jax.ShapeDtypeStruct((n, N_PROJ), BF16),
            jax.ShapeDtypeStruct((nt, A_KV_HEADS * A_VROWS, BLK), BF16),
            jax.ShapeDtypeStruct((nt, B_HEADS * B_VROWS, BLK), BF16),
            jax.ShapeDtypeStruct((nt, C_VROWS, BLK), BF16),
            jax.ShapeDtypeStruct((nt, IDX_HEADS, BLK), F32),
        ],
        scratch_shapes=[pltpu.VMEM((TM_PROJ, D_MODEL), BF16)],
        name="mixer_proj",
        compiler_params=pltpu.CompilerParams(
            dimension_semantics=("parallel", "arbitrary"), vmem_limit_bytes=VMEM_LIMIT),
    )(h, gain, w, wt)


def _tile_iotas():
    kl = lax.broadcasted_iota(jnp.int32, (BLK, BLK), 0)
    ql = lax.broadcasted_iota(jnp.int32, (BLK, BLK), 1)
    return kl, ql


def _colmax(x):
    return jnp.max(x, axis=0, keepdims=True)


def _colsum(x):
    return jnp.sum(x, axis=0, keepdims=True)


def _exp_bf16(x):
    return jnp.exp(x.astype(BF16))


def _tile_cascade(nt, step, carry):
    start = 0
    for width in (8, 4, 2):
        stop = (nt + 1) // 2 if width == 2 else nt // width
        carry = lax.fori_loop(start, stop, lambda s, c, w=width: step([w * s + t for t in range(w)], c), carry)
        start = 2 * stop
    return carry


A_GRP = A_HEADS // A_KV_HEADS
A_LANES = A_GRP * BLK


def _mixer_a_kernel(q_ref, k0_ref, k1_ref, k2_ref, v0_ref, v1_ref, v2_ref, bias_ref, sink_ref, o_ref, qst_ref):
    i = pl.program_id(1)
    kl = lax.broadcasted_iota(jnp.int32, (BLK, A_LANES), 0)
    ql = lax.broadcasted_iota(jnp.int32, (BLK, A_LANES), 1) & (BLK - 1)
    qpos = BLK * i - N_PAD + ql
    ok_meta = (kl >= N_PAD) & (qpos >= kl - N_PAD)
    ok_prev = (BLK * (i - 1) - N_PAD + kl >= N_META) & (ql < kl)
    ok_cur = (BLK * i - N_PAD + kl >= N_META) & (ql >= kl)
    t_meta = jnp.minimum(i, 2)
    qst_ref[...] = jnp.zeros_like(qst_ref)
    for h in range(A_HEADS):
        g, r = divmod(h, A_GRP)
        qst_ref[g, r * BLK:(r + 1) * BLK, g * HEAD_DIM:(g + 1) * HEAD_DIM] = (
            q_ref[:, h * HEAD_DIM:(h + 1) * HEAD_DIM] * QK_SCALE)
    outs = []
    for g in range(A_KV_HEADS):
        qs = qst_ref[g]
        s0 = jnp.where(ok_meta, _nt(k0_ref[...], qs) + bias_ref[g, t_meta], NEG)
        s1 = jnp.where(ok_prev, _nt(k1_ref[...], qs) + bias_ref[g, 1], NEG)
        s2 = jnp.where(ok_cur, _nt(k2_ref[...], qs) + bias_ref[g, 0], NEG)
        sink = sink_ref[g]
        m = jnp.maximum(jnp.maximum(_colmax(s0), _colmax(s1)), jnp.maximum(_colmax(s2), sink))
        vs = slice(g * A_VROWS, (g + 1) * A_VROWS)
        ot = (_mm(v0_ref[vs, :], _exp_bf16(s0 - m)) + _mm(v1_ref[vs, :], _exp_bf16(s1 - m))
              + _mm(v2_ref[vs, :], _exp_bf16(s2 - m)))
        ot = ot[:HEAD_DIM] / (ot[HEAD_DIM:HEAD_DIM + 1] + jnp.exp(sink - m))
        outs.extend(ot[:, r * BLK:(r + 1) * BLK] for r in range(A_GRP))
    o_ref[...] = jnp.concatenate(outs, axis=0).T.astype(BF16)


def _mixer_a(proj, avt, bias, sinks, bsz, nb):
    n = proj.shape[0]
    kcol = COL_AK // BLK
    rowq = lambda b, i: (b * nb + i, COL_AQ // (A_HEADS * HEAD_DIM))
    spec_k = lambda f: pl.BlockSpec((BLK, BLK), lambda b, i: (b * nb + f(i), kcol))
    spec_v = lambda f: pl.BlockSpec((None, A_KV_HEADS * A_VROWS, BLK), lambda b, i: (b * nb + f(i), 0, 0))
    first = lambda i: 0
    prev = lambda i: jnp.maximum(i - 1, 0)
    cur = lambda i: i
    return pl.pallas_call(
        _mixer_a_kernel,
        grid=(bsz, nb),
        in_specs=[
            pl.BlockSpec((BLK, A_HEADS * HEAD_DIM), rowq),
            spec_k(first), spec_k(prev), spec_k(cur),
            spec_v(first), spec_v(prev), spec_v(cur),
            pl.BlockSpec((A_KV_HEADS, 3, BLK, A_LANES), lambda b, i: (0, 0, 0, 0)),
            pl.BlockSpec((A_KV_HEADS, 1, A_LANES), lambda b, i: (0, 0, 0)),
        ],
        out_specs=pl.BlockSpec((BLK, A_HEADS * HEAD_DIM), lambda b, i: (b * nb + i, 0)),
        out_shape=jax.ShapeDtypeStruct((n, A_HEADS * HEAD_DIM), BF16),
        scratch_shapes=[pltpu.VMEM((A_KV_HEADS, A_LANES, BLK), BF16)],
        name="mixer_a",
        compiler_params=pltpu.CompilerParams(dimension_semantics=("parallel", "parallel")),
    )(proj, proj, proj, proj, avt, avt, avt, bias, sinks)


B_GROUPS = B_HEADS // 2


def _mixer_b_kernel(q1_ref, q2_ref, k1_ref, k2_ref, vt_ref, bias_ref, lam_ref, subln_ref, o_ref,
                    qbd_ref, acc_ref, *, lam_init, nb):
    i = pl.program_id(1)
    w2 = 2 * BLK
    kl = lax.broadcasted_iota(jnp.int32, (BLK, w2), 0)
    qpos = BLK * i - N_PAD + (lax.broadcasted_iota(jnp.int32, (BLK, w2), 1) & (BLK - 1))

    feat = lax.broadcasted_iota(jnp.int32, (BLK, BLK), 0)
    for mp, q_ref in enumerate((q1_ref, q2_ref)):
        for g in range(B_GROUPS):
            qt = (q_ref[:, g * BLK:(g + 1) * BLK].astype(F32) * QK_SCALE).T
            qbd_ref[mp, g] = jnp.concatenate(
                [jnp.where(feat < B_QK_DIM, qt, 0.0), jnp.where(feat >= B_QK_DIM, qt, 0.0)], axis=1).astype(BF16)
    acc_ref[...] = jnp.zeros_like(acc_ref)

    def step(jls, ms):
        tiles = [(jl, jnp.minimum(jl, nb - 1)) for jl in jls]
        oks, ts, rows = [], [], []
        for jl, jk in tiles:
            kpos = BLK * jl - N_PAD + kl
            oks.append((kpos >= 0) & (qpos >= kpos))
            ts.append(jnp.clip(i - jl, 0, 2))
            rows.append(pl.ds(pl.multiple_of(jk * BLK, BLK), BLK))
        ok2 = jnp.concatenate(oks, axis=0)
        raw = [[_mm(jnp.concatenate([k_ref[r, g * BLK:(g + 1) * BLK] for r in rows], axis=0), qbd_ref[mp, g])
                for g in range(B_GROUPS)] for mp, k_ref in enumerate((k1_ref, k2_ref))]
        bias2 = [jnp.concatenate([bias_ref[g, t] for t in ts], axis=0) for g in range(B_GROUPS)]
        new_ms = []
        ps = [[None] * B_GROUPS for _ in range(2)]
        alphas = [[None] * B_GROUPS for _ in range(2)]
        for mp in range(2):
            row_m = []
            for g in range(B_GROUPS):
                s = jnp.where(ok2, raw[mp][g] + bias2[g], NEG)
                m_old = ms[mp][g]
                m_new = jnp.maximum(m_old, _colmax(s))
                row_m.append(m_new)
                ps[mp][g] = _exp_bf16(s - m_new)
                alphas[mp][g] = jnp.exp(m_old - m_new)
            new_ms.append(row_m)
        for h in range(B_HEADS):
            g, hs = h // 2, slice((h % 2) * BLK, (h % 2 + 1) * BLK)
            alpha = jnp.concatenate([alphas[0][g][:, hs], alphas[1][g][:, hs]], axis=1)
            p12 = jnp.concatenate([ps[0][g][:, hs], ps[1][g][:, hs]], axis=1)
            vt2 = jnp.concatenate([vt_ref[jk, h * B_VROWS:(h + 1) * B_VROWS, :] for _, jk in tiles], axis=1)
            acc_ref[h] = alpha * acc_ref[h] + _mm(vt2, p12)
        return new_ms

    _tile_cascade(i + 1, step, [[jnp.full((1, w2), NEG, F32) for _ in range(B_GROUPS)] for _ in range(2)])

    lf = lam_ref[...]
    lam = (jnp.exp(jnp.sum(lf[0:1] * lf[1:2], axis=-1, keepdims=True))
           - jnp.exp(jnp.sum(lf[2:3] * lf[3:4], axis=-1, keepdims=True)) + lam_init)
    outs = []
    for h in range(B_HEADS):
        acc = acc_ref[h]
        num, den = acc[:B_V_DIM], acc[B_V_DIM:B_V_DIM + 1]
        o = num[:, :BLK] / den[:, :BLK] - lam * (num[:, BLK:] / den[:, BLK:])
        ms_ = jnp.mean(o * o, axis=0, keepdims=True)
        outs.append(o * lax.rsqrt(ms_ + EPS) * subln_ref[...] * (1.0 - lam_init))
    o_ref[...] = jnp.concatenate(outs, axis=0).T.astype(BF16)


def _mixer_b(proj, bvt, bias, lam, subln, bsz, nb, lam_init):
    n = proj.shape[0]
    p_rows = nb * BLK
    w = B_HEADS * B_QK_DIM
    rowq = lambda c: (lambda b, i: (b * nb + i, c))
    return pl.pallas_call(
        functools.partial(_mixer_b_kernel, lam_init=lam_init, nb=nb),
        grid=(bsz, nb),
        in_specs=[
            pl.BlockSpec((BLK, w), rowq(COL_BQ1 // w)),
            pl.BlockSpec((BLK, w), rowq(COL_BQ2 // w)),
            pl.BlockSpec((p_rows, w), lambda b, i: (b, COL_BK1 // w)),
            pl.BlockSpec((p_rows, w), lambda b, i: (b, COL_BK2 // w)),
            pl.BlockSpec((nb, B_HEADS * B_VROWS, BLK), lambda b, i: (b, 0, 0)),
            pl.BlockSpec((B_GROUPS, 3, BLK, 2 * BLK), lambda b, i: (0, 0, 0, 0)),
            pl.BlockSpec((None, 4, B_QK_DIM), lambda b, i: (0, 0, 0)),
            pl.BlockSpec((None, B_V_DIM, BLK), lambda b, i: (0, 0, 0)),
        ],
        out_specs=pl.BlockSpec((BLK, B_HEADS * B_V_DIM), lambda b, i: (b * nb + i, 0)),
        out_shape=jax.ShapeDtypeStruct((n, B_HEADS * B_V_DIM), BF16),
        scratch_shapes=[
            pltpu.VMEM((2, B_GROUPS, BLK, 2 * BLK), BF16),
            pltpu.VMEM((B_HEADS, B_VROWS, 2 * BLK), F32),
        ],
        name="mixer_b",
        compiler_params=pltpu.CompilerParams(dimension_semantics=("parallel", "arbitrary")),
    )(proj, proj, proj, proj, bvt, bias, lam, subln)


C_LANES = C_HEADS * BLK
C_CHUNK = 2 * BLK


def _ordered_bits_to_float(u):
    bits = jnp.where(u < 0, u ^ jnp.int32(-2 ** 31), ~u)
    return lax.bitcast_convert_type(bits, F32)


def _mixer_c_kernel(cq_ref, iq_ref, kk_ref, vt_ref, iwt_ref, bias_ref, o_ref,
                    score_ref, sel_ref, qst_ref, iqst_ref, acc_ref, *, topk, nb):
    i = pl.program_id(1)
    npair = (i + 2) // 2
    kl, ql = _tile_iotas()
    qpos = BLK * i - N_PAD + ql
    kf = jnp.float32(topk)

    qst_ref[...] = jnp.zeros_like(qst_ref)
    iqst_ref[...] = jnp.zeros_like(iqst_ref)
    for g in range(C_HEADS // 2):
        cols = slice(g * BLK, (g + 1) * BLK)
        qt = (cq_ref[:, cols].astype(F32) * QK_SCALE).T.astype(BF16)
        it = iq_ref[:, cols].astype(F32).T.astype(BF16)
        for hh in range(2):
            lanes = slice((2 * g + hh) * BLK, (2 * g + hh + 1) * BLK)
            qst_ref[0:HEAD_DIM, lanes] = qt[hh * HEAD_DIM:(hh + 1) * HEAD_DIM, :]
            iqst_ref[HEAD_DIM:BLK, lanes] = it[hh * IDX_DIM:(hh + 1) * IDX_DIM, :]
    iw = iwt_ref[...] * QK_SCALE
    iw_all = jnp.concatenate([iw[h:h + 1, :] for h in range(IDX_HEADS)], axis=1)

    def admissible(jl):
        kpos = BLK * jl - N_PAD + kl
        return (kpos >= 0) & (qpos >= kpos)

    def score_tiles(jls, carry):
        kk = jnp.concatenate([kk_ref[pl.ds(pl.multiple_of(jnp.minimum(jl, nb - 1) * BLK, BLK), BLK), :]
                              for jl in jls], axis=0)
        sc = None
        for c in range(C_LANES // C_CHUNK):
            ln = slice(c * C_CHUNK, (c + 1) * C_CHUNK)
            rel = jnp.maximum(_mm(kk, iqst_ref[:, ln]), 0.0) * iw_all[:, ln]
            for h in range(C_CHUNK // BLK):
                part = rel[:, h * BLK:(h + 1) * BLK]
                sc = part if sc is None else sc + part
        for t, jl in enumerate(jls):
            score_ref[jl] = jnp.where(admissible(jl), sc[t * BLK:(t + 1) * BLK] * IDX_HEADS ** -0.5, NEG)
        return carry

    _tile_cascade(i + 1, score_tiles, 0)

    def count(pred):
        def body(p, cs):
            xa = jnp.where(pred(score_ref[2 * p]), 1.0, 0.0)
            xb = jnp.where(pred(score_ref[2 * p + 1]), 1.0, 0.0)
            return [c + (xa[r * 8:(r + 1) * 8, :] + xb[r * 8:(r + 1) * 8, :]) for r, c in enumerate(cs)]
        parts = lax.fori_loop(0, npair, body, [jnp.zeros((8, BLK), F32)] * (BLK // 8))
        while len(parts) > 1:
            parts = [parts[t] + parts[t + 1] for t in range(0, len(parts), 2)]
        return _colsum(parts[0])

    def search(it, u):
        cand = u | lax.shift_left(jnp.int32(1), 31 - it)
        thr_c = _ordered_bits_to_float(cand)
        return jnp.where(count(lambda s: s >= thr_c) >= kf, cand, u)

    u = lax.fori_loop(0, 32, search, jnp.zeros((1, BLK), jnp.int32))
    found = (u < 0) | (u >= jnp.int32(0x00800000))
    thr = jnp.where(found, _ordered_bits_to_float(u), -jnp.inf)
    need = kf - count(lambda s: s > thr)
    n_ge = count(lambda s: s >= thr)
    has_tie = jnp.max(jnp.where((n_ge > kf) & (thr > NEG), 1.0, 0.0)) > 0.5

    def select_plain():
        def body(p, carry):
            for jl in (2 * p, 2 * p + 1):
                sel_ref[jl] = jnp.where(admissible(jl) & (score_ref[jl] >= thr), 0.0, NEG)
            return carry
        lax.fori_loop(0, npair, body, 0)

    def select_ties():
        tri = (kl >= ql).astype(BF16)

        def body(jl, seen):
            sc = score_ref[jl]
            eq = sc == thr
            eqf = jnp.where(eq, 1.0, 0.0)
            rank = seen + _mm(tri, eqf.astype(BF16))
            sel = admissible(jl) & ((sc > thr) | (eq & (rank <= need)))
            sel_ref[jl] = jnp.where(sel, 0.0, NEG)
            return seen + _colsum(eqf)
        lax.fori_loop(0, 2 * npair, body, jnp.zeros((1, BLK), F32))

    lax.cond(has_tie, select_ties, select_plain)

    acc_ref[...] = jnp.zeros_like(acc_ref)

    def attend_tiles(jls, m_old):
        jks = [jnp.minimum(jl, nb - 1) for jl in jls]
        kk = jnp.concatenate([kk_ref[pl.ds(pl.multiple_of(jk * BLK, BLK), BLK), :] for jk in jks], axis=0)
        vt = jnp.concatenate([vt_ref[jk] for jk in jks], axis=1)
        extra = jnp.concatenate([bias_ref[jnp.clip(i - jl, 0, 2)] + jnp.tile(sel_ref[jl], (1, C_HEADS))
                                 for jl in jls], axis=0)
        s = _mm(kk, qst_ref[...]) + extra
        m_new = jnp.maximum(m_old, _colmax(s))
        acc_ref[...] = jnp.exp(m_old - m_new) * acc_ref[...] + _mm(vt, _exp_bf16(s - m_new))
        return m_new

    _tile_cascade(i + 1, attend_tiles, jnp.full((1, C_LANES), NEG, F32))
    o = acc_ref[0:HEAD_DIM, :] / acc_ref[HEAD_DIM:HEAD_DIM + 1, :]
    o_rows = jnp.concatenate([o[:, h * BLK:(h + 1) * BLK] for h in range(C_HEADS)], axis=0)
    o_ref[...] = o_rows.T.astype(BF16)


def _mixer_c(proj, cvt, iwt, bias, bsz, nb, topk):
    n = proj.shape[0]
    p_rows = nb * BLK
    w = C_HEADS * HEAD_DIM
    return pl.pallas_call(
        functools.partial(_mixer_c_kernel, topk=topk, nb=nb),
        grid=(bsz, nb),
        in_specs=[
            pl.BlockSpec((BLK, w), lambda b, i: (b * nb + i, COL_CQ // w)),
            pl.BlockSpec((BLK, w), lambda b, i: (b * nb + i, COL_IQ // w)),
            pl.BlockSpec((p_rows, BLK), lambda b, i: (b, COL_CKIK // BLK)),
            pl.BlockSpec((nb, C_VROWS, BLK), lambda b, i: (b, 0, 0)),
            pl.BlockSpec((None, IDX_HEADS, BLK), lambda b, i: (b * nb + i, 0, 0)),
            pl.BlockSpec((3, BLK, C_LANES), lambda b, i: (0, 0, 0)),
        ],
        out_specs=pl.BlockSpec((BLK, w), lambda b, i: (b * nb + i, 0)),
        out_shape=jax.ShapeDtypeStruct((n, w), BF16),
        scratch_shapes=[
            pltpu.VMEM((nb + 1, BLK, BLK), F32),
            pltpu.VMEM((nb + 1, BLK, BLK), F32),
            pltpu.VMEM((BLK, C_LANES), BF16),
            pltpu.VMEM((BLK, C_LANES), BF16),
            pltpu.VMEM((C_VROWS, C_LANES), F32),
        ],
        name="mixer_c",
        compiler_params=pltpu.CompilerParams(dimension_semantics=("parallel", "arbitrary")),
    )(proj, proj, proj, cvt, iwt, bias)


def _merge_kernel(h_ref, g_ref, oa_ref, ob_ref, oc_ref, ga_ref, gb_ref, gc_ref, wbr_ref, wo_ref,
                  o_ref, xn_ref, acc_ref, *, nc):
    c = pl.program_id(1)

    @pl.when(c == 0)
    def _():
        xn_ref[...] = _rms(h_ref[...], g_ref[...]).astype(BF16)
        acc_ref[...] = jnp.zeros_like(acc_ref)

    xn = xn_ref[...]
    y = (jax.nn.sigmoid(_mm(xn, ga_ref[...])) * _mm(oa_ref[...], wbr_ref[0])
         + jax.nn.sigmoid(_mm(xn, gb_ref[...])) * _mm(ob_ref[...], wbr_ref[1])
         + jax.nn.sigmoid(_mm(xn, gc_ref[...])) * _mm(oc_ref[...], wbr_ref[2]))
    acc_ref[...] += _mm(y.astype(BF16), wo_ref[...])

    @pl.when(c == nc - 1)
    def _():
        o_ref[...] = h_ref[...] + acc_ref[...]


def _merge(h, gain, oa, ob, oc, wgates, wbr, wout, layer):
    n = h.shape[0]
    tc = TC_MERGE
    nc = D_MODEL // tc
    bw = 512
    row = lambda i, c: (i, 0)
    wgate = lambda br: pl.BlockSpec((None, D_MODEL, tc), lambda i, c: (layer, 0, br * nc + c))
    return pl.pallas_call(
        functools.partial(_merge_kernel, nc=nc),
        grid=(n // TM, nc),
        in_specs=[
            pl.BlockSpec((TM, D_MODEL), row),
            pl.BlockSpec((None, 1, D_MODEL), lambda i, c: (layer, 0, 0)),
            pl.BlockSpec((TM, bw), row), pl.BlockSpec((TM, bw), row), pl.BlockSpec((TM, bw), row),
            wgate(0), wgate(1), wgate(2),
            pl.BlockSpec((None, 3, bw, tc), lambda i, c: (layer, 0, 0, c)),
            pl.BlockSpec((None, tc, D_MODEL), lambda i, c: (layer, c, 0)),
        ],
        out_specs=pl.BlockSpec((TM, D_MODEL), row),
        out_shape=jax.ShapeDtypeStruct((n, D_MODEL), F32),
        scratch_shapes=[pltpu.VMEM((TM, D_MODEL), BF16), pltpu.VMEM((TM, D_MODEL), F32)],
        input_output_aliases={0: 0},
        name="branch_merge",
        compiler_params=pltpu.CompilerParams(
            dimension_semantics=("parallel", "arbitrary"), vmem_limit_bytes=VMEM_LIMIT),
    )(h, gain, oa, ob, oc, wgates, wgates, wgates, wbr, wout)


def _final_kernel(h_ref, g_ref, o_ref):
    o_ref[...] = _rms(h_ref[...], g_ref[...])


def _final_norm(h, gain, bsz, nb):
    seq = (nb - 1) * BLK
    tr = 512 if seq % 512 == 0 else BLK
    return pl.pallas_call(
        _final_kernel,
        grid=(bsz, seq // tr),
        in_specs=[
            pl.BlockSpec((pl.Element(tr), pl.Element(D_MODEL)),
                         lambda b, r: (pl.multiple_of((b * nb + 1) * BLK + r * tr, BLK), 0)),
            pl.BlockSpec((1, D_MODEL), lambda b, r: (0, 0)),
        ],
        out_specs=pl.BlockSpec((None, tr, D_MODEL), lambda b, r: (b, r, 0)),
        out_shape=jax.ShapeDtypeStruct((bsz, seq, D_MODEL), F32),
        name="final_norm",
        compiler_params=pltpu.CompilerParams(dimension_semantics=("parallel", "parallel")),
    )(h, gain)


def _bucket_np(d):
    max_exact = N_BUCKETS // 2
    d = np.maximum(d, 0)
    df = np.maximum(d, 1).astype(np.float32)
    large = max_exact + (np.log(df / max_exact) / math.log(MAX_DISTANCE / max_exact)
                         * (N_BUCKETS - max_exact)).astype(np.int32)
    return np.where(d < max_exact, d, np.minimum(large, N_BUCKETS - 1)).astype(np.int32)


def _bias_tiles(table):
    kl = np.arange(BLK)[:, None]
    ql = np.arange(BLK)[None, :]
    idx = np.stack([_bucket_np(ql - kl), _bucket_np(BLK + ql - kl),
                    np.full((BLK, BLK), N_BUCKETS - 1, np.int32)])
    onehot = (idx.reshape(-1, 1) == np.arange(N_BUCKETS)[None, :]).astype(np.float32)
    tiles = jnp.dot(onehot, table.astype(F32), precision=lax.Precision.HIGHEST)
    return jnp.transpose(tiles.reshape(3, BLK, BLK, -1), (3, 0, 1, 2))


def _pack_w_in(w_in):
    o = np.cumsum([0, 512, 128, 128, 256, 256, 256, 256, 512, 512, 64, 64, 512, 64, 8, 3 * D_MODEL])
    w_in = w_in.astype(BF16)
    (aq, ak, av, bq1, bq2, bk1, bk2, bv, cq, ck, cv, iq, ik, iw, gates) = [
        w_in[:, :, o[t]:o[t + 1]] for t in range(15)]
    pad = jnp.zeros(w_in.shape[:2] + (N_PROJ - COL_CKIK - BLK,), BF16)
    w = jnp.concatenate([aq, cq, iq, bq1, bq2, bk1, bk2, ak, ck, ik, pad], axis=-1)
    padt = jnp.zeros(w_in.shape[:2] + (N_PROJ_T - ROW_IW - IDX_HEADS,), BF16)
    wt = jnp.swapaxes(lax.optimization_barrier(jnp.concatenate([av, bv, cv, iw, padt], axis=-1)), 1, 2)
    return w, wt, gates


def kernel(x, meta_tokens, rel_bias_table, norm_ffn1, ffn1_w_gate, ffn1_w_up, ffn1_w_down, norm_mix, w_in,
           attn_sinks, diff_lambda, diff_subln, w_branch, w_out, norm_ffn2, ffn2_w_gate, ffn2_w_up,
           ffn2_w_down, norm_final):
    bsz, seq, _ = x.shape
    nb = seq // BLK + 1
    p_rows = nb * BLK
    n = bsz * p_rows
    assert seq % BLK == 0 and n % TM_FFN == 0
    topk = min(TOPK_MAX, seq // 4)

    lead = jnp.concatenate([jnp.zeros((N_PAD, D_MODEL), x.dtype), meta_tokens.astype(x.dtype)], axis=0)

    bias = _bias_tiles(rel_bias_table)
    bias_a = jnp.transpose(bias[:A_HEADS].reshape(A_KV_HEADS, A_GRP, 3, BLK, BLK),
                           (0, 2, 3, 1, 4)).reshape(A_KV_HEADS, 3, BLK, A_LANES)
    bias_b = jnp.transpose(bias[A_HEADS:A_HEADS + B_HEADS].reshape(B_GROUPS, 2, 3, BLK, BLK),
                           (0, 2, 3, 1, 4)).reshape(B_GROUPS, 3, BLK, 2 * BLK)
    bias_c = jnp.transpose(bias[A_HEADS + B_HEADS:], (1, 2, 0, 3)).reshape(3, BLK, C_LANES)
    w_proj, w_proj_t, w_gates = _pack_w_in(w_in)
    g1, gm, g2 = (t.reshape(DEPTH, 1, D_MODEL) for t in (norm_ffn1, norm_mix, norm_ffn2))
    f1 = [t.astype(BF16) for t in (ffn1_w_gate, ffn1_w_up, ffn1_w_down)]
    f2 = [t.astype(BF16) for t in (ffn2_w_gate, ffn2_w_up, ffn2_w_down)]
    wbr, wout = w_branch.astype(BF16), w_out.astype(BF16)
    sinks = jnp.broadcast_to(attn_sinks.astype(F32)[:, :, None],
                             (DEPTH, A_HEADS, BLK)).reshape(DEPTH, A_KV_HEADS, 1, A_LANES)
    subln = jnp.broadcast_to(diff_subln.astype(F32)[:, :, None], (DEPTH, B_V_DIM, BLK))
    lam_f = diff_lambda.astype(F32)

    for l in range(DEPTH):
        h = _ffn_first(x, lead, g1, *f1, nb) if l == 0 else _ffn(h, g1, *f1, l)
        proj, avt, bvt, cvt, iwt = _proj(h, gm, w_proj, w_proj_t, l)
        o_a = _mixer_a(proj, avt, bias_a, sinks[l], bsz, nb)
        lam_init = 0.8 - 0.6 * math.exp(-0.3 * l)
        o_b = _mixer_b(proj, bvt, bias_b, lam_f[l:l + 1], subln[l:l + 1], bsz, nb, lam_init)
        o_c = _mixer_c(proj, cvt, iwt, bias_c, bsz, nb, topk)
        h = _merge(h, gm, o_a, o_b, o_c, w_gates, wbr, wout, l)
        h = _ffn(h, g2, *f2, l)

    return _final_norm(h, norm_final.reshape(1, D_MODEL), bsz, nb)
```

```python
import functools
import math

import jax
import jax.numpy as jnp
import numpy as np
from jax import lax
from jax.experimental import pallas as pl
from jax.experimental.pallas import tpu as pltpu

F32 = jnp.float32
BF16 = jnp.bfloat16

D_MODEL = 2048
D_FF = 5632
DEPTH = 2
BLK = 128
N_META = 16
N_PAD = BLK - N_META
WINDOW = 128
assert WINDOW == BLK
HEAD_DIM = 64
A_HEADS, A_KV_HEADS = 8, 2
B_HEADS, B_QK_DIM, B_V_DIM = 4, 64, 128
C_HEADS = 8
IDX_HEADS, IDX_DIM = 8, 64
TOPK_MAX = 256
N_BUCKETS, MAX_DISTANCE = 32, 128
EPS = 1e-6
NEG = -1e30

COL_AQ, COL_CQ, COL_IQ = 0, 512, 1024
COL_BQ1, COL_BQ2, COL_BK1, COL_BK2 = 1536, 1792, 2048, 2304
COL_AK, COL_CKIK = 2560, 2688
N_PROJ = 3072
ROW_AV, ROW_BV, ROW_CV, ROW_IW, N_PROJ_T = 0, 128, 640, 704, 720

V_ONES = 16
A_VROWS, B_VROWS, C_VROWS = HEAD_DIM + V_ONES, B_V_DIM + V_ONES, HEAD_DIM + V_ONES

TM = 512
TM_FFN = 1024
TM_PROJ = 1024
TF = 512
FFN_VMEM_LIMIT = 62 * 1024 * 1024
TN = 1024
TC_MERGE = 512
assert HEAD_DIM == B_QK_DIM == IDX_DIM == 64
QK_SCALE = 0.125
VMEM_LIMIT = 56 * 1024 * 1024


def _nt(a, b):
    return lax.dot_general(a, b, (((1,), (1,)), ((), ())), preferred_element_type=F32)


def _mm(a, b):
    return jnp.dot(a, b, preferred_element_type=F32)


def _rms(x, g):
    ms = jnp.mean(x * x, axis=-1, keepdims=True)
    return x * lax.rsqrt(ms + EPS) * g


def _ffn_accumulate(xn_ref, wg_ref, wu_ref, wd_ref, o_ref):
    for r in range(TM_FFN // TM):
        rows = slice(r * TM, (r + 1) * TM)
        xn = xn_ref[rows, :]
        a = _mm(xn, wg_ref[...])
        b = _mm(xn, wu_ref[...])
        t = (a * jax.nn.sigmoid(a)) * b
        o_ref[rows, :] += 0.5 * _mm(t.astype(BF16), wd_ref[...])


def _ffn_kernel(h_ref, g_ref, wg_ref, wu_ref, wd_ref, o_ref, xn_ref):
    @pl.when(pl.program_id(1) == 0)
    def _():
        h = h_ref[...]
        xn_ref[...] = _rms(h, g_ref[...]).astype(BF16)
        o_ref[...] = h

    _ffn_accumulate(xn_ref, wg_ref, wu_ref, wd_ref, o_ref)


def _ffn_first_kernel(*refs, nb):
    nblk = TM_FFN // BLK
    x_refs = refs[:nblk]
    lead_ref, g_ref, wg_ref, wu_ref, wd_ref, o_ref, xn_ref = refs[nblk:]

    @pl.when(pl.program_id(1) == 0)
    def _():
        first = pl.program_id(0) * nblk
        for t in range(nblk):
            rows = slice(t * BLK, (t + 1) * BLK)
            o_ref[rows, :] = x_refs[t][...]

            @pl.when(lax.rem(first + t, nb) == 0)
            def _():
                o_ref[rows, :] = lead_ref[...]

        xn_ref[...] = _rms(o_ref[...], g_ref[...]).astype(BF16)

    _ffn_accumulate(xn_ref, wg_ref, wu_ref, wd_ref, o_ref)


def _ffn_weight_specs(layer):
    return [
        pl.BlockSpec((None, 1, D_MODEL), lambda i, k: (layer, 0, 0)),
        pl.BlockSpec((None, D_MODEL, TF), lambda i, k: (layer, 0, k)),
        pl.BlockSpec((None, D_MODEL, TF), lambda i, k: (layer, 0, k)),
        pl.BlockSpec((None, TF, D_MODEL), lambda i, k: (layer, k, 0)),
    ]


def _ffn(h, gain, wg, wu, wd, layer):
    n = h.shape[0]
    return pl.pallas_call(
        _ffn_kernel,
        grid=(n // TM_FFN, D_FF // TF),
        in_specs=[pl.BlockSpec((TM_FFN, D_MODEL), lambda i, k: (i, 0))] + _ffn_weight_specs(layer),
        out_specs=pl.BlockSpec((TM_FFN, D_MODEL), lambda i, k: (i, 0)),
        out_shape=jax.ShapeDtypeStruct((n, D_MODEL), F32),
        scratch_shapes=[pltpu.VMEM((TM_FFN, D_MODEL), BF16)],
        name="ffn",
        compiler_params=pltpu.CompilerParams(
            dimension_semantics=("parallel", "arbitrary"), vmem_limit_bytes=FFN_VMEM_LIMIT,
            allow_input_fusion=[False, False, True, True, True]),
    )(h, gain, wg, wu, wd)


def _ffn_first(x, lead, gain, wg, wu, wd, nb):
    bsz = x.shape[0]
    n = bsz * nb * BLK
    nblk = TM_FFN // BLK

    def x_spec(t):
        return pl.BlockSpec((None, BLK, D_MODEL),
                            lambda i, k: ((i * nblk + t) // nb, jnp.maximum((i * nblk + t) % nb - 1, 0), 0))

    return pl.pallas_call(
        functools.partial(_ffn_first_kernel, nb=nb),
        grid=(n // TM_FFN, D_FF // TF),
        in_specs=[x_spec(t) for t in range(nblk)]
        + [pl.BlockSpec((BLK, D_MODEL), lambda i, k: (0, 0))] + _ffn_weight_specs(0),
        out_specs=pl.BlockSpec((TM_FFN, D_MODEL), lambda i, k: (i, 0)),
        out_shape=jax.ShapeDtypeStruct((n, D_MODEL), F32),
        scratch_shapes=[pltpu.VMEM((TM_FFN, D_MODEL), BF16)],
        name="ffn_first",
        compiler_params=pltpu.CompilerParams(
            dimension_semantics=("parallel", "arbitrary"), vmem_limit_bytes=FFN_VMEM_LIMIT,
            allow_input_fusion=[False] * (nblk + 2) + [True, True, True]),
    )(*([x] * nblk), lead, gain, wg, wu, wd)


def _proj_kernel(h_ref, g_ref, w_ref, wt_ref, o_ref, avt_ref, bvt_ref, cvt_ref, iwt_ref, xn_ref):
    j = pl.program_id(1)

    @pl.when(j == 0)
    def _():
        xn = _rms(h_ref[...], g_ref[...]).astype(BF16)
        xn_ref[...] = xn
        vt = _nt(wt_ref[...], xn)
        for t in range(TM_PROJ // BLK):
            cols = slice(t * BLK, (t + 1) * BLK)
            ones = jnp.ones((V_ONES, BLK), BF16)
            for g in range(A_KV_HEADS):
                avt_ref[t, g * A_VROWS:g * A_VROWS + HEAD_DIM] = (
                    vt[ROW_AV + g * HEAD_DIM:ROW_AV + (g + 1) * HEAD_DIM, cols].astype(BF16))
                avt_ref[t, g * A_VROWS + HEAD_DIM:(g + 1) * A_VROWS] = ones
            for h in range(B_HEADS):
                bvt_ref[t, h * B_VROWS:h * B_VROWS + B_V_DIM] = (
                    vt[ROW_BV + h * B_V_DIM:ROW_BV + (h + 1) * B_V_DIM, cols].astype(BF16))
                bvt_ref[t, h * B_VROWS + B_V_DIM:(h + 1) * B_VROWS] = ones
            cvt_ref[t, 0:HEAD_DIM] = vt[ROW_CV:ROW_IW, cols].astype(BF16)
            cvt_ref[t, HEAD_DIM:C_VROWS] = ones
            iwt_ref[t] = vt[ROW_IW:ROW_IW + IDX_HEADS, cols]

    o_ref[...] = _mm(xn_ref[...], w_ref[...]).astype(BF16)


def _proj(h, gain, w, wt, layer):
    n = h.shape[0]
    nt = n // BLK
    tpb = TM_PROJ // BLK
    return pl.pallas_call(
        _proj_kernel,
        grid=(n // TM_PROJ, N_PROJ // TN),
        in_specs=[
            pl.BlockSpec((TM_PROJ, D_MODEL), lambda i, j: (i, 0)),
            pl.BlockSpec((None, 1, D_MODEL), lambda i, j: (layer, 0, 0)),
            pl.BlockSpec((None, D_MODEL, TN), lambda i, j: (layer, 0, j)),
            pl.BlockSpec((None, N_PROJ_T, D_MODEL), lambda i, j: (layer, 0, 0)),
        ],
        out_specs=[
            pl.BlockSpec((TM_PROJ, TN), lambda i, j: (i, j)),
            pl.BlockSpec((tpb, A_KV_HEADS * A_VROWS, BLK), lambda i, j: (i, 0, 0)),
            pl.BlockSpec((tpb, B_HEADS * B_VROWS, BLK), lambda i, j: (i, 0, 0)),
            pl.BlockSpec((tpb, C_VROWS, BLK), lambda i, j: (i, 0, 0)),
            pl.BlockSpec((tpb, IDX_HEADS, BLK), lambda i, j: (i, 0, 0)),
        ],
        out_shape=[
            jax.ShapeDtypeStruct((n, N_PROJ), BF16),
            jax.ShapeDtypeStruct((nt, A_KV_HEADS * A_VROWS, BLK), BF16),
            jax.ShapeDtypeStruct((nt, B_HEADS * B_VROWS, BLK), BF16),
            jax.ShapeDtypeStruct((nt, C_VROWS, BLK), BF16),
            jax.ShapeDtypeStruct((nt, IDX_HEADS, BLK), F32),
        ],
        scratch_shapes=[pltpu.VMEM((TM_PROJ, D_MODEL), BF16)],
        name="mixer_proj",
        compiler_params=pltpu.CompilerParams(
            dimension_semantics=("parallel", "arbitrary"), vmem_limit_bytes=VMEM_LIMIT),
    )(h, gain, w, wt)


def _tile_iotas():
    kl = lax.broadcasted_iota(jnp.int32, (BLK, BLK), 0)
    ql = lax.broadcasted_iota(jnp.int32, (BLK, BLK), 1)
    return kl, ql


def _colmax(x):
    return jnp.max(x, axis=0, keepdims=True)


def _colsum(x):
    return jnp.sum(x, axis=0, keepdims=True)


def _exp_bf16(x):
    return jnp.exp(x.astype(BF16))


def _tile_cascade(nt, step, carry):
    start = 0
    for width in (8, 4, 2):
        stop = (nt + 1) // 2 if width == 2 else nt // width
        carry = lax.fori_loop(start, stop, lambda s, c, w=width: step([w * s + t for t in range(w)], c), carry)
        start = 2 * stop
    return carry


A_GRP = A_HEADS // A_KV_HEADS
A_LANES = A_GRP * BLK


def _mixer_a_kernel(q_ref, k0_ref, k1_ref, k2_ref, v0_ref, v1_ref, v2_ref, bias_ref, sink_ref, o_ref, qst_ref):
    i = pl.program_id(1)
    kl = lax.broadcasted_iota(jnp.int32, (BLK, A_LANES), 0)
    ql = lax.broadcasted_iota(jnp.int32, (BLK, A_LANES), 1) & (BLK - 1)
    qpos = BLK * i - N_PAD + ql
    ok_meta = (kl >= N_PAD) & (qpos >= kl - N_PAD)
    ok_prev = (BLK * (i - 1) - N_PAD + kl >= N_META) & (ql < kl)
    ok_cur = (BLK * i - N_PAD + kl >= N_META) & (ql >= kl)
    t_meta = jnp.minimum(i, 2)
    qst_ref[...] = jnp.zeros_like(qst_ref)
    for h in range(A_HEADS):
        g, r = divmod(h, A_GRP)
        qst_ref[g, r * BLK:(r + 1) * BLK, g * HEAD_DIM:(g + 1) * HEAD_DIM] = (
            q_ref[:, h * HEAD_DIM:(h + 1) * HEAD_DIM] * QK_SCALE)
    outs = []
    for g in range(A_KV_HEADS):
        qs = qst_ref[g]
        s0 = jnp.where(ok_meta, _nt(k0_ref[...], qs) + bias_ref[g, t_meta], NEG)
        s1 = jnp.where(ok_prev, _nt(k1_ref[...], qs) + bias_ref[g, 1], NEG)
        s2 = jnp.where(ok_cur, _nt(k2_ref[...], qs) + bias_ref[g, 0], NEG)
        sink = sink_ref[g]
        m = jnp.maximum(jnp.maximum(_colmax(s0), _colmax(s1)), jnp.maximum(_colmax(s2), sink))
        vs = slice(g * A_VROWS, (g + 1) * A_VROWS)
        ot = (_mm(v0_ref[vs, :], _exp_bf16(s0 - m)) + _mm(v1_ref[vs, :], _exp_bf16(s1 - m))
              + _mm(v2_ref[vs, :], _exp_bf16(s2 - m)))
        ot = ot[:HEAD_DIM] / (ot[HEAD_DIM:HEAD_DIM + 1] + jnp.exp(sink - m))
        outs.extend(ot[:, r * BLK:(r + 1) * BLK] for r in range(A_GRP))
    o_ref[...] = jnp.concatenate(outs, axis=0).T.astype(BF16)


def _mixer_a(proj, avt, bias, sinks, bsz, nb):
    n = proj.shape[0]
    kcol = COL_AK // BLK
    rowq = lambda b, i: (b * nb + i, COL_AQ // (A_HEADS * HEAD_DIM))
    spec_k = lambda f: pl.BlockSpec((BLK, BLK), lambda b, i: (b * nb + f(i), kcol))
    spec_v = lambda f: pl.BlockSpec((None, A_KV_HEADS * A_VROWS, BLK), lambda b, i: (b * nb + f(i), 0, 0))
    first = lambda i: 0
    prev = lambda i: jnp.maximum(i - 1, 0)
    cur = lambda i: i
    return pl.pallas_call(
        _mixer_a_kernel,
        grid=(bsz, nb),
        in_specs=[
            pl.BlockSpec((BLK, A_HEADS * HEAD_DIM), rowq),
            spec_k(first), spec_k(prev), spec_k(cur),
            spec_v(first), spec_v(prev), spec_v(cur),
            pl.BlockSpec((A_KV_HEADS, 3, BLK, A_LANES), lambda b, i: (0, 0, 0, 0)),
            pl.BlockSpec((A_KV_HEADS, 1, A_LANES), lambda b, i: (0, 0, 0)),
        ],
        out_specs=pl.BlockSpec((BLK, A_HEADS * HEAD_DIM), lambda b, i: (b * nb + i, 0)),
        out_shape=jax.ShapeDtypeStruct((n, A_HEADS * HEAD_DIM), BF16),
        scratch_shapes=[pltpu.VMEM((A_KV_HEADS, A_LANES, BLK), BF16)],
        name="mixer_a",
        compiler_params=pltpu.CompilerParams(dimension_semantics=("parallel", "parallel")),
    )(proj, proj, proj, proj, avt, avt, avt, bias, sinks)


B_GROUPS = B_HEADS // 2


def _mixer_b_kernel(q1_ref, q2_ref, k1_ref, k2_ref, vt_ref, bias_ref, lam_ref, subln_ref, o_ref,
                    qbd_ref, acc_ref, *, lam_init, nb):
    i = pl.program_id(1)
    w2 = 2 * BLK
    kl = lax.broadcasted_iota(jnp.int32, (BLK, w2), 0)
    qpos = BLK * i - N_PAD + (lax.broadcasted_iota(jnp.int32, (BLK, w2), 1) & (BLK - 1))

    feat = lax.broadcasted_iota(jnp.int32, (BLK, BLK), 0)
    for mp, q_ref in enumerate((q1_ref, q2_ref)):
        for g in range(B_GROUPS):
            qt = (q_ref[:, g * BLK:(g + 1) * BLK].astype(F32) * QK_SCALE).T
            qbd_ref[mp, g] = jnp.concatenate(
                [jnp.where(feat < B_QK_DIM, qt, 0.0), jnp.where(feat >= B_QK_DIM, qt, 0.0)], axis=1).astype(BF16)
    acc_ref[...] = jnp.zeros_like(acc_ref)

    def step(jls, ms):
        tiles = [(jl, jnp.minimum(jl, nb - 1)) for jl in jls]
        oks, ts, rows = [], [], []
        for jl, jk in tiles:
            kpos = BLK * jl - N_PAD + kl
            oks.append((kpos >= 0) & (qpos >= kpos))
            ts.append(jnp.clip(i - jl, 0, 2))
            rows.append(pl.ds(pl.multiple_of(jk * BLK, BLK), BLK))
        ok2 = jnp.concatenate(oks, axis=0)
        raw = [[_mm(jnp.concatenate([k_ref[r, g * BLK:(g + 1) * BLK] for r in rows], axis=0), qbd_ref[mp, g])
                for g in range(B_GROUPS)] for mp, k_ref in enumerate((k1_ref, k2_ref))]
        bias2 = [jnp.concatenate([bias_ref[g, t] for t in ts], axis=0) for g in range(B_GROUPS)]
        new_ms = []
        ps = [[None] * B_GROUPS for _ in range(2)]
        alphas = [[None] * B_GROUPS for _ in range(2)]
        for mp in range(2):
            row_m = []
            for g in range(B_GROUPS):
                s = jnp.where(ok2, raw[mp][g] + bias2[g], NEG)
                m_old = ms[mp][g]
                m_new = jnp.maximum(m_old, _colmax(s))
                row_m.append(m_new)
                ps[mp][g] = _exp_bf16(s - m_new)
                alphas[mp][g] = jnp.exp(m_old - m_new)
            new_ms.append(row_m)
        for h in range(B_HEADS):
            g, hs = h // 2, slice((h % 2) * BLK, (h % 2 + 1) * BLK)
            alpha = jnp.concatenate([alphas[0][g][:, hs], alphas[1][g][:, hs]], axis=1)
            p12 = jnp.concatenate([ps[0][g][:, hs], ps[1][g][:, hs]], axis=1)
            vt2 = jnp.concatenate([vt_ref[jk, h * B_VROWS:(h + 1) * B_VROWS, :] for _, jk in tiles], axis=1)
            acc_ref[h] = alpha * acc_ref[h] + _mm(vt2, p12)
        return new_ms

    _tile_cascade(i + 1, step, [[jnp.full((1, w2), NEG, F32) for _ in range(B_GROUPS)] for _ in range(2)])

    lf = lam_ref[...]
    lam = (jnp.exp(jnp.sum(lf[0:1] * lf[1:2], axis=-1, keepdims=True))
           - jnp.exp(jnp.sum(lf[2:3] * lf[3:4], axis=-1, keepdims=True)) + lam_init)
    outs = []
    for h in range(B_HEADS):
        acc = acc_ref[h]
        num, den = acc[:B_V_DIM], acc[B_V_DIM:B_V_DIM + 1]
        o = num[:, :BLK] / den[:, :BLK] - lam * (num[:, BLK:] / den[:, BLK:])
        ms_ = jnp.mean(o * o, axis=0, keepdims=True)
        outs.append(o * lax.rsqrt(ms_ + EPS) * subln_ref[...] * (1.0 - lam_init))
    o_ref[...] = jnp.concatenate(outs, axis=0).T.astype(BF16)


def _mixer_b(proj, bvt, bias, lam, subln, bsz, nb, lam_init):
    n = proj.shape[0]
    p_rows = nb * BLK
    w = B_HEADS * B_QK_DIM
    rowq = lambda c: (lambda b, i: (b * nb + i, c))
    return pl.pallas_call(
        functools.partial(_mixer_b_kernel, lam_init=lam_init, nb=nb),
        grid=(bsz, nb),
        in_specs=[
            pl.BlockSpec((BLK, w), rowq(COL_BQ1 // w)),
            pl.BlockSpec((BLK, w), rowq(COL_BQ2 // w)),
            pl.BlockSpec((p_rows, w), lambda b, i: (b, COL_BK1 // w)),
            pl.BlockSpec((p_rows, w), lambda b, i: (b, COL_BK2 // w)),
            pl.BlockSpec((nb, B_HEADS * B_VROWS, BLK), lambda b, i: (b, 0, 0)),
            pl.BlockSpec((B_GROUPS, 3, BLK, 2 * BLK), lambda b, i: (0, 0, 0, 0)),
            pl.BlockSpec((None, 4, B_QK_DIM), lambda b, i: (0, 0, 0)),
            pl.BlockSpec((None, B_V_DIM, BLK), lambda b, i: (0, 0, 0)),
        ],
        out_specs=pl.BlockSpec((BLK, B_HEADS * B_V_DIM), lambda b, i: (b * nb + i, 0)),
        out_shape=jax.ShapeDtypeStruct((n, B_HEADS * B_V_DIM), BF16),
        scratch_shapes=[
            pltpu.VMEM((2, B_GROUPS, BLK, 2 * BLK), BF16),
            pltpu.VMEM((B_HEADS, B_VROWS, 2 * BLK), F32),
        ],
        name="mixer_b",
        compiler_params=pltpu.CompilerParams(dimension_semantics=("parallel", "arbitrary")),
    )(proj, proj, proj, proj, bvt, bias, lam, subln)


C_LANES = C_HEADS * BLK
C_CHUNK = 2 * BLK


def _ordered_bits_to_float(u):
    bits = jnp.where(u < 0, u ^ jnp.int32(-2 ** 31), ~u)
    return lax.bitcast_convert_type(bits, F32)


def _mixer_c_kernel(cq_ref, iq_ref, kk_ref, vt_ref, iwt_ref, bias_ref, o_ref,
                    score_ref, sel_ref, qst_ref, iqst_ref, acc_ref, *, topk, nb):
    i = pl.program_id(1)
    npair = (i + 2) // 2
    kl, ql = _tile_iotas()
    qpos = BLK * i - N_PAD + ql
    kf = jnp.float32(topk)

    qst_ref[...] = jnp.zeros_like(qst_ref)
    iqst_ref[...] = jnp.zeros_like(iqst_ref)
    for g in range(C_HEADS // 2):
        cols = slice(g * BLK, (g + 1) * BLK)
        qt = (cq_ref[:, cols].astype(F32) * QK_SCALE).T.astype(BF16)
        it = iq_ref[:, cols].astype(F32).T.astype(BF16)
        for hh in range(2):
            lanes = slice((2 * g + hh) * BLK, (2 * g + hh + 1) * BLK)
            qst_ref[0:HEAD_DIM, lanes] = qt[hh * HEAD_DIM:(hh + 1) * HEAD_DIM, :]
            iqst_ref[HEAD_DIM:BLK, lanes] = it[hh * IDX_DIM:(hh + 1) * IDX_DIM, :]
    iw = iwt_ref[...] * QK_SCALE
    iw_all = jnp.concatenate([iw[h:h + 1, :] for h in range(IDX_HEADS)], axis=1)

    def admissible(jl):
        kpos = BLK * jl - N_PAD + kl
        return (kpos >= 0) & (qpos >= kpos)

    def score_tiles(jls, carry):
        kk = jnp.concatenate([kk_ref[pl.ds(pl.multiple_of(jnp.minimum(jl, nb - 1) * BLK, BLK), BLK), :]
                              for jl in jls], axis=0)
        sc = None
        for c in range(C_LANES // C_CHUNK):
            ln = slice(c * C_CHUNK, (c + 1) * C_CHUNK)
            rel = jnp.maximum(_mm(kk, iqst_ref[:, ln]), 0.0) * iw_all[:, ln]
            for h in range(C_CHUNK // BLK):
                part = rel[:, h * BLK:(h + 1) * BLK]
                sc = part if sc is None else sc + part
        for t, jl in enumerate(jls):
            score_ref[jl] = jnp.where(admissible(jl), sc[t * BLK:(t + 1) * BLK] * IDX_HEADS ** -0.5, NEG)
        return carry

    _tile_cascade(i + 1, score_tiles, 0)

    def count(pred):
        def body(p, cs):
            xa = jnp.where(pred(score_ref[2 * p]), 1.0, 0.0)
            xb = jnp.where(pred(score_ref[2 * p + 1]), 1.0, 0.0)
            return [c + (xa[r * 8:(r + 1) * 8, :] + xb[r * 8:(r + 1) * 8, :]) for r, c in enumerate(cs)]
        parts = lax.fori_loop(0, npair, body, [jnp.zeros((8, BLK), F32)] * (BLK // 8))
        while len(parts) > 1:
            parts = [parts[t] + parts[t + 1] for t in range(0, len(parts), 2)]
        return _colsum(parts[0])

    def search(it, u):
        cand = u | lax.shift_left(jnp.int32(1), 31 - it)
        thr_c = _ordered_bits_to_float(cand)
        return jnp.where(count(lambda s: s >= thr_c) >= kf, cand, u)

    u = lax.fori_loop(0, 32, search, jnp.zeros((1, BLK), jnp.int32))
    found = (u < 0) | (u >= jnp.int32(0x00800000))
    thr = jnp.where(found, _ordered_bits_to_float(u), -jnp.inf)
    need = kf - count(lambda s: s > thr)
    n_ge = count(lambda s: s >= thr)
    has_tie = jnp.max(jnp.where((n_ge > kf) & (thr > NEG), 1.0, 0.0)) > 0.5

    def select_plain():
        def body(p, carry):
            for jl in (2 * p, 2 * p + 1):
                sel_ref[jl] = jnp.where(admissible(jl) & (score_ref[jl] >= thr), 0.0, NEG)
            return carry
        lax.fori_loop(0, npair, body, 0)

    def select_ties():
        tri = (kl >= ql).astype(BF16)

        def body(jl, seen):
            sc = score_ref[jl]
            eq = sc == thr
            eqf = jnp.where(eq, 1.0, 0.0)
            rank = seen + _mm(tri, eqf.astype(BF16))
            sel = admissible(jl) & ((sc > thr) | (eq & (rank <= need)))
            sel_ref[jl] = jnp.where(sel, 0.0, NEG)
            return seen + _colsum(eqf)
        lax.fori_loop(0, 2 * npair, body, jnp.zeros((1, BLK), F32))

    lax.cond(has_tie, select_ties, select_plain)

    acc_ref[...] = jnp.zeros_like(acc_ref)

    def attend_tiles(jls, m_old):
        jks = [jnp.minimum(jl, nb - 1) for jl in jls]
        kk = jnp.concatenate([kk_ref[pl.ds(pl.multiple_of(jk * BLK, BLK), BLK), :] for jk in jks], axis=0)
        vt = jnp.concatenate([vt_ref[jk] for jk in jks], axis=1)
        extra = jnp.concatenate([bias_ref[jnp.clip(i - jl, 0, 2)] + jnp.tile(sel_ref[jl], (1, C_HEADS))
                                 for jl in jls], axis=0)
        s = _mm(kk, qst_ref[...]) + extra
        m_new = jnp.maximum(m_old, _colmax(s))
        acc_ref[...] = jnp.exp(m_old - m_new) * acc_ref[...] + _mm(vt, _exp_bf16(s - m_new))
        return m_new

    _tile_cascade(i + 1, attend_tiles, jnp.full((1, C_LANES), NEG, F32))
    o = acc_ref[0:HEAD_DIM, :] / acc_ref[HEAD_DIM:HEAD_DIM + 1, :]
    o_rows = jnp.concatenate([o[:, h * BLK:(h + 1) * BLK] for h in range(C_HEADS)], axis=0)
    o_ref[...] = o_rows.T.astype(BF16)


def _mixer_c(proj, cvt, iwt, bias, bsz, nb, topk):
    n = proj.shape[0]
    p_rows = nb * BLK
    w = C_HEADS * HEAD_DIM
    return pl.pallas_call(
        functools.partial(_mixer_c_kernel, topk=topk, nb=nb),
        grid=(bsz, nb),
        in_specs=[
            pl.BlockSpec((BLK, w), lambda b, i: (b * nb + i, COL_CQ // w)),
            pl.BlockSpec((BLK, w), lambda b, i: (b * nb + i, COL_IQ // w)),
            pl.BlockSpec((p_rows, BLK), lambda b, i: (b, COL_CKIK // BLK)),
            pl.BlockSpec((nb, C_VROWS, BLK), lambda b, i: (b, 0, 0)),
            pl.BlockSpec((None, IDX_HEADS, BLK), lambda b, i: (b * nb + i, 0, 0)),
            pl.BlockSpec((3, BLK, C_LANES), lambda b, i: (0, 0, 0)),
        ],
        out_specs=pl.BlockSpec((BLK, w), lambda b, i: (b * nb + i, 0)),
        out_shape=jax.ShapeDtypeStruct((n, w), BF16),
        scratch_shapes=[
            pltpu.VMEM((nb + 1, BLK, BLK), F32),
            pltpu.VMEM((nb + 1, BLK, BLK), F32),
            pltpu.VMEM((BLK, C_LANES), BF16),
            pltpu.VMEM((BLK, C_LANES), BF16),
            pltpu.VMEM((C_VROWS, C_LANES), F32),
        ],
        name="mixer_c",
        compiler_params=pltpu.CompilerParams(dimension_semantics=("parallel", "arbitrary")),
    )(proj, proj, proj, cvt, iwt, bias)


def _merge_kernel(h_ref, g_ref, oa_ref, ob_ref, oc_ref, ga_ref, gb_ref, gc_ref, wbr_ref, wo_ref,
                  o_ref, xn_ref, acc_ref, *, nc):
    c = pl.program_id(1)

    @pl.when(c == 0)
    def _():
        xn_ref[...] = _rms(h_ref[...], g_ref[...]).astype(BF16)
        acc_ref[...] = jnp.zeros_like(acc_ref)

    xn = xn_ref[...]
    y = (jax.nn.sigmoid(_mm(xn, ga_ref[...])) * _mm(oa_ref[...], wbr_ref[0])
         + jax.nn.sigmoid(_mm(xn, gb_ref[...])) * _mm(ob_ref[...], wbr_ref[1])
         + jax.nn.sigmoid(_mm(xn, gc_ref[...])) * _mm(oc_ref[...], wbr_ref[2]))
    acc_ref[...] += _mm(y.astype(BF16), wo_ref[...])

    @pl.when(c == nc - 1)
    def _():
        o_ref[...] = h_ref[...] + acc_ref[...]


def _merge(h, gain, oa, ob, oc, wgates, wbr, wout, layer):
    n = h.shape[0]
    tc = TC_MERGE
    nc = D_MODEL // tc
    bw = 512
    row = lambda i, c: (i, 0)
    wgate = lambda br: pl.BlockSpec((None, D_MODEL, tc), lambda i, c: (layer, 0, br * nc + c))
    return pl.pallas_call(
        functools.partial(_merge_kernel, nc=nc),
        grid=(n // TM, nc),
        in_specs=[
            pl.BlockSpec((TM, D_MODEL), row),
            pl.BlockSpec((None, 1, D_MODEL), lambda i, c: (layer, 0, 0)),
            pl.BlockSpec((TM, bw), row), pl.BlockSpec((TM, bw), row), pl.BlockSpec((TM, bw), row),
            wgate(0), wgate(1), wgate(2),
            pl.BlockSpec((None, 3, bw, tc), lambda i, c: (layer, 0, 0, c)),
            pl.BlockSpec((None, tc, D_MODEL), lambda i, c: (layer, c, 0)),
        ],
        out_specs=pl.BlockSpec((TM, D_MODEL), row),
        out_shape=jax.ShapeDtypeStruct((n, D_MODEL), F32),
        scratch_shapes=[pltpu.VMEM((TM, D_MODEL), BF16), pltpu.VMEM((TM, D_MODEL), F32)],
        input_output_aliases={0: 0},
        name="branch_merge",
        compiler_params=pltpu.CompilerParams(
            dimension_semantics=("parallel", "arbitrary"), vmem_limit_bytes=VMEM_LIMIT),
    )(h, gain, oa, ob, oc, wgates, wgates, wgates, wbr, wout)


def _final_kernel(h_ref, g_ref, o_ref):
    o_ref[...] = _rms(h_ref[...], g_ref[...])


def _final_norm(h, gain, bsz, nb):
    seq = (nb - 1) * BLK
    tr = 512 if seq % 512 == 0 else BLK
    return pl.pallas_call(
        _final_kernel,
        grid=(bsz, seq // tr),
        in_specs=[
            pl.BlockSpec((pl.Element(tr), pl.Element(D_MODEL)),
                         lambda b, r: (pl.multiple_of((b * nb + 1) * BLK + r * tr, BLK), 0)),
            pl.BlockSpec((1, D_MODEL), lambda b, r: (0, 0)),
        ],
        out_specs=pl.BlockSpec((None, tr, D_MODEL), lambda b, r: (b, r, 0)),
        out_shape=jax.ShapeDtypeStruct((bsz, seq, D_MODEL), F32),
        name="final_norm",
        compiler_params=pltpu.CompilerParams(dimension_semantics=("parallel", "parallel")),
    )(h, gain)


def _bucket_np(d):
    max_exact = N_BUCKETS // 2
    d = np.maximum(d, 0)
    df = np.maximum(d, 1).astype(np.float32)
    large = max_exact + (np.log(df / max_exact) / math.log(MAX_DISTANCE / max_exact)
                         * (N_BUCKETS - max_exact)).astype(np.int32)
    return np.where(d < max_exact, d, np.minimum(large, N_BUCKETS - 1)).astype(np.int32)


def _bias_tiles(table):
    kl = np.arange(BLK)[:, None]
    ql = np.arange(BLK)[None, :]
    idx = np.stack([_bucket_np(ql - kl), _bucket_np(BLK + ql - kl),
                    np.full((BLK, BLK), N_BUCKETS - 1, np.int32)])
    onehot = (idx.reshape(-1, 1) == np.arange(N_BUCKETS)[None, :]).astype(np.float32)
    tiles = jnp.dot(onehot, table.astype(F32), precision=lax.Precision.HIGHEST)
    return jnp.transpose(tiles.reshape(3, BLK, BLK, -1), (3, 0, 1, 2))


def _pack_w_in(w_in):
    o = np.cumsum([0, 512, 128, 128, 256, 256, 256, 256, 512, 512, 64, 64, 512, 64, 8, 3 * D_MODEL])
    w_in = w_in.astype(BF16)
    (aq, ak, av, bq1, bq2, bk1, bk2, bv, cq, ck, cv, iq, ik, iw, gates) = [
        w_in[:, :, o[t]:o[t + 1]] for t in range(15)]
    pad = jnp.zeros(w_in.shape[:2] + (N_PROJ - COL_CKIK - BLK,), BF16)
    w = jnp.concatenate([aq, cq, iq, bq1, bq2, bk1, bk2, ak, ck, ik, pad], axis=-1)
    padt = jnp.zeros(w_in.shape[:2] + (N_PROJ_T - ROW_IW - IDX_HEADS,), BF16)
    wt = jnp.swapaxes(lax.optimization_barrier(jnp.concatenate([av, bv, cv, iw, padt], axis=-1)), 1, 2)
    return w, wt, gates


def kernel(x, meta_tokens, rel_bias_table, norm_ffn1, ffn1_w_gate, ffn1_w_up, ffn1_w_down, norm_mix, w_in,
           attn_sinks, diff_lambda, diff_subln, w_branch, w_out, norm_ffn2, ffn2_w_gate, ffn2_w_up,
           ffn2_w_down, norm_final):
    bsz, seq, _ = x.shape
    nb = seq // BLK + 1
    p_rows = nb * BLK
    n = bsz * p_rows
    assert seq % BLK == 0 and n % TM_FFN == 0
    topk = min(TOPK_MAX, seq // 4)

    lead = jnp.concatenate([jnp.zeros((N_PAD, D_MODEL), x.dtype), meta_tokens.astype(x.dtype)], axis=0)

    bias = _bias_tiles(rel_bias_table)
    bias_a = jnp.transpose(bias[:A_HEADS].reshape(A_KV_HEADS, A_GRP, 3, BLK, BLK),
                           (0, 2, 3, 1, 4)).reshape(A_KV_HEADS, 3, BLK, A_LANES)
    bias_b = jnp.transpose(bias[A_HEADS:A_HEADS + B_HEADS].reshape(B_GROUPS, 2, 3, BLK, BLK),
                           (0, 2, 3, 1, 4)).reshape(B_GROUPS, 3, BLK, 2 * BLK)
    bias_c = jnp.transpose(bias[A_HEADS + B_HEADS:], (1, 2, 0, 3)).reshape(3, BLK, C_LANES)
    w_proj, w_proj_t, w_gates = _pack_w_in(w_in)
    g1, gm, g2 = (t.reshape(DEPTH, 1, D_MODEL) for t in (norm_ffn1, norm_mix, norm_ffn2))
    f1 = [t.astype(BF16) for t in (ffn1_w_gate, ffn1_w_up, ffn1_w_down)]
    f2 = [t.astype(BF16) for t in (ffn2_w_gate, ffn2_w_up, ffn2_w_down)]
    wbr, wout = w_branch.astype(BF16), w_out.astype(BF16)
    sinks = jnp.broadcast_to(attn_sinks.astype(F32)[:, :, None],
                             (DEPTH, A_HEADS, BLK)).reshape(DEPTH, A_KV_HEADS, 1, A_LANES)
    subln = jnp.broadcast_to(diff_subln.astype(F32)[:, :, None], (DEPTH, B_V_DIM, BLK))
    lam_f = diff_lambda.astype(F32)

    for l in range(DEPTH):
        h = _ffn_first(x, lead, g1, *f1, nb) if l == 0 else _ffn(h, g1, *f1, l)
        proj, avt, bvt, cvt, iwt = _proj(h, gm, w_proj, w_proj_t, l)
        o_a = _mixer_a(proj, avt, bias_a, sinks[l], bsz, nb)
        lam_init = 0.8 - 0.6 * math.exp(-0.3 * l)
        o_b = _mixer_b(proj, bvt, bias_b, lam_f[l:l + 1], subln[l:l + 1], bsz, nb, lam_init)
        o_c = _mixer_c(proj, cvt, iwt, bias_c, bsz, nb, topk)
        h = _merge(h, gm, o_a, o_b, o_c, w_gates, wbr, wout, l)
        h = _ffn(h, g2, *f2, l)

    return _final_norm(h, norm_final.reshape(1, D_MODEL), bsz, nb)
```
